```python
import math
import jax, jax.numpy as jnp
from jax import lax
import numpy as np

D_MODEL = 1024
BATCH = 4
SEQ = 8192
DEPTH = 1

CHUNK = 64
N_MEM = 256
D_MIX = D_MODEL
GLA_HEADS = 4
GLA_DV = (D_MIX // 2) // GLA_HEADS
GLA_DK = GLA_DV // 2
GLA_GATE_RANK = 16
GLA_GATE_TAU = 16.0
RET_HEADS = 4
RET_DV = (D_MIX // 2) // RET_HEADS
RET_DK = RET_DV // 2
RET_DECAY_EXP_LO = 5.0
RET_DECAY_EXP_HI = 12.0
ROPE_BASE = 10000.0
XATTN_HEADS = 4
XATTN_DH = D_MODEL // XATTN_HEADS
D_FF = 4 * D_MODEL
EPS = 1e-6
IN_SPLITS = (GLA_HEADS * GLA_DK, GLA_HEADS * GLA_DK, GLA_HEADS * GLA_DV, GLA_GATE_RANK, GLA_HEADS * GLA_DV,
             RET_HEADS * RET_DK, RET_HEADS * RET_DK, RET_HEADS * RET_DV, RET_HEADS * RET_DV)
D_IN_PROJ = sum(IN_SPLITS)

kernel_name = "hybrid_gla_retnet_parallel_heads"


def rmsnorm(x, g):
    xf = x.astype(jnp.float32)
    y = xf * lax.rsqrt(jnp.mean(xf * xf, axis=-1, keepdims=True) + EPS)
    return (y * g.astype(jnp.float32)).astype(x.dtype)


def group_norm(x, g):
    xf = x.astype(jnp.float32)
    mu = jnp.mean(xf, axis=-1, keepdims=True)
    var = jnp.mean(jnp.square(xf - mu), axis=-1, keepdims=True)
    return ((xf - mu) * lax.rsqrt(var + EPS) * g.astype(jnp.float32)).astype(x.dtype)


def to_chunks(t, heads, d):
    B, S = t.shape[0], t.shape[1]
    t = t.reshape(B, S // CHUNK, CHUNK, heads, d)
    return jnp.transpose(t, (0, 3, 1, 2, 4))


def from_chunks(t):
    B, H, N, C, d = t.shape
    return jnp.transpose(t, (0, 2, 3, 1, 4)).reshape(B, N * C, H, d)


def carry_states(decay, contrib):
    decay = jnp.broadcast_to(decay, contrib.shape[:-1] + (1,)).astype(contrib.dtype)
    def step(state, inp):
        dec, u = inp
        return dec * state + u, state
    s0 = jnp.zeros(contrib.shape[1:], contrib.dtype)
    _, s_prev = lax.scan(step, s0, (decay, contrib))
    return jnp.moveaxis(s_prev, 0, 2)


def gla_group(q, k, v, g_lr, w_gate, b_gate):
    log_a = jax.nn.log_sigmoid((g_lr @ w_gate + b_gate).astype(jnp.float32)) / GLA_GATE_TAU
    qc = to_chunks(q, GLA_HEADS, GLA_DK).astype(jnp.float32) * (GLA_DK ** -0.5)
    kc = to_chunks(k, GLA_HEADS, GLA_DK).astype(jnp.float32)
    vc = to_chunks(v, GLA_HEADS, GLA_DV).astype(jnp.float32)
    bcum = jnp.cumsum(to_chunks(log_a, GLA_HEADS, GLA_DK), axis=3)
    b_last = bcum[:, :, :, -1:, :]
    eb, ebi = jnp.exp(bcum), jnp.exp(-bcum)
    lower = jnp.tril(jnp.ones((CHUNK, CHUNK), dtype=bool))
    a_lo = jnp.einsum('bhnik,bhnjk->bhnij', qc * eb, kc * ebi)
    a_up = jnp.einsum('bhnik,bhnjk->bhnij', qc * ebi, kc * eb)
    attn = jnp.where(lower, a_lo, a_up)
    intra = jnp.einsum('bhnij,bhnjv->bhniv', attn, vc)
    contrib = jnp.einsum('bhnjk,bhnjv->bhnkv', kc * jnp.exp(b_last - bcum), vc)
    decay = jnp.moveaxis(jnp.exp(b_last[:, :, :, 0, :]), 2, 0)[..., None]
    s_prev = carry_states(decay, jnp.moveaxis(contrib, 2, 0))
    inter = jnp.einsum('bhnik,bhnkv->bhniv', qc * eb, s_prev)
    return from_chunks(intra + inter).astype(v.dtype)


def rope(t, cos, sin):
    half = t.shape[-1] // 2
    t1, t2 = t[..., :half], t[..., half:]
    c, s = cos[None, :, None, :], sin[None, :, None, :]
    return jnp.concatenate([t1 * c - t2 * s, t1 * s + t2 * c], axis=-1).astype(t.dtype)


def retention_group(q, k, v, cos, sin):
    B, S, _ = q.shape
    qh = rope(q.reshape(B, S, RET_HEADS, RET_DK).astype(jnp.float32), cos, sin)
    kh = rope(k.reshape(B, S, RET_HEADS, RET_DK).astype(jnp.float32), cos, sin) * (RET_DK ** -0.5)
    qc, kc = to_chunks(qh, RET_HEADS, RET_DK), to_chunks(kh, RET_HEADS, RET_DK)
    vc = to_chunks(v, RET_HEADS, RET_DV).astype(jnp.float32)
    gamma = 1.0 - jnp.exp2(-jnp.linspace(RET_DECAY_EXP_LO, RET_DECAY_EXP_HI, RET_HEADS, dtype=jnp.float32))
    log_g = jnp.log(gamma)
    pos = jnp.arange(CHUNK, dtype=jnp.float32)
    dmat = jnp.exp(jnp.abs(pos[:, None] - pos[None, :])[None] * log_g[:, None, None])
    scores = jnp.einsum('bhnik,bhnjk->bhnij', qc, kc) * dmat[None, :, None]
    intra = jnp.einsum('bhnij,bhnjv->bhniv', scores, vc)
    k_dec = jnp.exp((CHUNK - 1.0 - pos)[None, :] * log_g[:, None])
    contrib = jnp.einsum('bhnjk,bhnjv->bhnkv', kc * k_dec[None, :, None, :, None], vc)
    decay = jnp.exp(CHUNK * log_g)[None, None, :, None, None]
    s_prev = carry_states(decay, jnp.moveaxis(contrib, 2, 0))
    q_dec = jnp.exp((pos + 1.0)[None, :] * log_g[:, None])
    inter = jnp.einsum('bhnik,bhnkv->bhniv', qc, s_prev) * q_dec[None, :, None, :, None]
    return from_chunks(intra + inter).astype(v.dtype)


def parallel_mixer(xn, w_in, gla_w_gate, gla_b_gate, gla_norm_g, ret_norm_g, w_out, cos, sin):
    B, S, _ = xn.shape
    proj = xn @ w_in
    idx, acc = [], 0
    for sz in IN_SPLITS[:-1]:
        acc += sz
        idx.append(acc)
    gq, gk, gv, glr, gg, rq, rk, rv, rg = jnp.split(proj, idx, axis=-1)
    o_gla = gla_group(gq, gk, gv, glr, gla_w_gate, gla_b_gate)
    o_gla = rmsnorm(o_gla, gla_norm_g) * jax.nn.silu(gg.reshape(B, S, GLA_HEADS, GLA_DV))
    o_ret = retention_group(rq, rk, rv, cos, sin)
    o_ret = group_norm(o_ret, ret_norm_g) * jax.nn.silu(rg.reshape(B, S, RET_HEADS, RET_DV))
    y = jnp.concatenate([o_gla.reshape(B, S, -1), o_ret.reshape(B, S, -1)], axis=-1)
    return y @ w_out


def cross_attention(hn, mem_n, w_q, w_kv, w_o):
    B, S, _ = hn.shape
    M = mem_n.shape[1]
    q = (hn @ w_q).reshape(B, S, XATTN_HEADS, XATTN_DH)
    k, v = jnp.split(mem_n @ w_kv, 2, axis=-1)
    k = k.reshape(B, M, XATTN_HEADS, XATTN_DH)
    v = v.reshape(B, M, XATTN_HEADS, XATTN_DH)
    s = jnp.einsum('bshd,bmhd->bhsm', q, k).astype(jnp.float32) * (XATTN_DH ** -0.5)
    p = jax.nn.softmax(s, axis=-1).astype(v.dtype)
    o = jnp.einsum('bhsm,bmhd->bshd', p, v).reshape(B, S, D_MODEL)
    return o @ w_o


def squared_relu_mlp(hn, w1, w2):
    return jnp.square(jax.nn.relu(hn @ w1)) @ w2


def setup_inputs(seed: int = 0) -> dict:
    key = jax.random.key(seed)
    ks = jax.random.split(key, 20)

    def w(k, shape, fan_in):
        return jax.random.normal(k, shape, jnp.float32) * (fan_in ** -0.5)

    def gain(k, shape):
        return 1.0 + 0.02 * jax.random.normal(k, shape, jnp.float32)

    L = DEPTH
    return {
        "x": jax.random.normal(ks[0], (BATCH, SEQ, D_MODEL), jnp.float32),
        "mem": jax.random.normal(ks[1], (BATCH, N_MEM, D_MODEL), jnp.float32),
        "norm_mix_g": gain(ks[2], (L, D_MODEL)),
        "w_in": w(ks[3], (L, D_MODEL, D_IN_PROJ), D_MODEL),
        "gla_w_gate": w(ks[4], (L, GLA_GATE_RANK, GLA_HEADS * GLA_DK), GLA_GATE_RANK),
        "gla_b_gate": 0.01 * jax.random.normal(ks[5], (L, GLA_HEADS * GLA_DK), jnp.float32),
        "gla_norm_g": gain(ks[6], (L, GLA_HEADS, GLA_DV)),
        "ret_norm_g": gain(ks[7], (L, RET_HEADS, RET_DV)),
        "w_out": w(ks[8], (L, D_MIX, D_MODEL), D_MIX),
        "norm_x_g": gain(ks[9], (L, D_MODEL)),
        "norm_mem_g": gain(ks[10], (L, D_MODEL)),
        "xa_w_q": w(ks[11], (L, D_MODEL, D_MODEL), D_MODEL),
        "xa_w_kv": w(ks[12], (L, D_MODEL, 2 * D_MODEL), D_MODEL),
        "xa_w_o": w(ks[13], (L, D_MODEL, D_MODEL), D_MODEL),
        "norm_ffn_g": gain(ks[14], (L, D_MODEL)),
        "ffn_w1": w(ks[15], (L, D_MODEL, D_FF), D_MODEL),
        "ffn_w2": w(ks[16], (L, D_FF, D_MODEL), D_FF),
        "final_norm_g": gain(ks[17], (D_MODEL,)),
    }


def reference(x, mem, norm_mix_g, w_in, gla_w_gate, gla_b_gate, gla_norm_g, ret_norm_g, w_out,
              norm_x_g, norm_mem_g, xa_w_q, xa_w_kv, xa_w_o, norm_ffn_g, ffn_w1, ffn_w2, final_norm_g):
    S = x.shape[1]
    inv_freq = ROPE_BASE ** (-jnp.arange(0, RET_DK, 2, dtype=jnp.float32) / RET_DK)
    ang = jnp.arange(S, dtype=jnp.float32)[:, None] * inv_freq[None, :]
    cos, sin = jnp.cos(ang), jnp.sin(ang)
    h = x
    for l in range(DEPTH):
        h = h + parallel_mixer(rmsnorm(h, norm_mix_g[l]), w_in[l], gla_w_gate[l], gla_b_gate[l],
                               gla_norm_g[l], ret_norm_g[l], w_out[l], cos, sin)
        h = h + cross_attention(rmsnorm(h, norm_x_g[l]), rmsnorm(mem, norm_mem_g[l]),
                                xa_w_q[l], xa_w_kv[l], xa_w_o[l])
        h = h + squared_relu_mlp(rmsnorm(h, norm_ffn_g[l]), ffn_w1[l], ffn_w2[l])
    return rmsnorm(h, final_norm_g)
```

```python
import functools

import numpy as np
import jax
import jax.numpy as jnp
from jax import lax
from jax.experimental import pallas as pl
from jax.experimental.pallas import tpu as pltpu

D_MODEL = 1024
CHUNK = 64
HEADS = 4
DK = 64
DV = 128
GATE_RANK = 16
GATE_RANK_PAD = 128
GATE_TAU = 16.0
RET_DECAY_EXP_LO = 5.0
RET_DECAY_EXP_HI = 12.0
ROPE_BASE = 10000.0
XATTN_HEADS = 4
XATTN_DH = D_MODEL // XATTN_HEADS
D_FF = 4 * D_MODEL
FF_BLOCK = 1024
EPS = 1e-6

HK = HEADS * DK
HV = HEADS * DV
C_GQ, C_GK, C_GV, C_GG = 0, HK, 2 * HK, 2 * HK + HV
C_RQ = C_GG + HV
C_RK = C_RQ + HK
C_RV = C_RK + HK
C_RG = C_RV + HV
C_LR = C_RG + HV
D_PROJ = C_LR + GATE_RANK_PAD

SEQ_TILE = 512
VMEM_LIMIT_BYTES = 56 * 1024 * 1024

F32 = jnp.float32
BF16 = jnp.bfloat16


def _dot(a, b):
    return jnp.dot(a, b, preferred_element_type=F32)


def _dot_nt(a, b):
    return lax.dot_general(a, b, (((1,), (1,)), ((), ())), preferred_element_type=F32)


def _rms(x, g):
    return x * lax.rsqrt(jnp.mean(x * x, axis=-1, keepdims=True) + EPS) * g


def _stack_heads(t):
    lane_head = lax.broadcasted_iota(jnp.int32, t.shape, 1) // DK
    zero = jnp.zeros_like(t)
    return jnp.concatenate([jnp.where(lane_head == h, t, zero) for h in range(HEADS)], axis=0)


def _mem_kv_kernel(mem_ref, g_ref, wkv_ref, kt_ref, v_ref):
    mn = _rms(mem_ref[0], g_ref[...]).astype(BF16)
    kv = _dot(mn, wkv_ref[...])
    for h in range(XATTN_HEADS):
        kt_ref[0, h] = kv[:, h * XATTN_DH:(h + 1) * XATTN_DH].T.astype(BF16)
    v_ref[0] = kv[:, D_MODEL:].astype(BF16)


def _mixer_kernel(x_ref, cos_ref, sin_ref, g_ref, win_ref, wg_ref, bg_ref, glan_ref, retn_ref,
                  wout_ref, dmat_ref, qdec_ref, kdec_ref, sdec_ref,
                  o_ref, proj_ref, osc_ref, sg_ref, sr_ref, *, n_chunks):
    @pl.when(pl.program_id(1) == 0)
    def _():
        sg_ref[...] = jnp.zeros_like(sg_ref)
        sr_ref[...] = jnp.zeros_like(sr_ref)

    x = x_ref[0]
    proj_ref[...] = _dot(_rms(x, g_ref[...]).astype(BF16), win_ref[...])

    ii = lax.broadcasted_iota(jnp.int32, (CHUNK, CHUNK), 0)
    jj = lax.broadcasted_iota(jnp.int32, (CHUNK, CHUNK), 1)
    tril = (jj <= ii).astype(BF16)
    si = lax.broadcasted_iota(jnp.int32, (HEADS * CHUNK, CHUNK), 0) % CHUNK
    sj = lax.broadcasted_iota(jnp.int32, (HEADS * CHUNK, CHUNK), 1)
    lower = sj <= si
    first_half = (lax.broadcasted_iota(jnp.int32, (CHUNK, HK), 1) % DK) < (DK // 2)

    def head_products(attn, kdt, v, inter, s_old, s_decay, rows, col0):
        contribs = []
        for h in range(HEADS):
            hs = slice(h * CHUNK, (h + 1) * CHUNK)
            lhs = jnp.concatenate([attn[hs], kdt[hs]], axis=0)
            r = _dot(lhs, v[:, h * DV:(h + 1) * DV])
            osc_ref[rows, col0 + h * DV:col0 + (h + 1) * DV] = r[:CHUNK] + inter[hs]
            contribs.append(r[CHUNK:])
        return s_old * s_decay + jnp.concatenate(contribs, axis=0)

    for c in range(n_chunks):
        rows = slice(c * CHUNK, (c + 1) * CHUNK)

        q = proj_ref[rows, C_GQ:C_GQ + HK] * (DK ** -0.5)
        k = proj_ref[rows, C_GK:C_GK + HK]
        v = proj_ref[rows, C_GV:C_GV + HV].astype(BF16)
        z = _dot(proj_ref[rows, C_LR:C_LR + GATE_RANK_PAD].astype(BF16), wg_ref[...]) + bg_ref[...]
        log_a = (jnp.minimum(z, 0.0) - jnp.log1p(jnp.exp(-jnp.abs(z)))) * (1.0 / GATE_TAU)
        la_hi = log_a.astype(BF16)
        la_lo = (log_a - la_hi.astype(F32)).astype(BF16)
        bcum = _dot(tril, la_hi) + _dot(tril, la_lo)
        b_last = bcum[CHUNK - 1:CHUNK, :]
        eb, ebi = jnp.exp(bcum), jnp.exp(-bcum)
        qe = _stack_heads((q * eb).astype(BF16))
        qei = _stack_heads((q * ebi).astype(BF16))
        a_lo = _dot_nt(qe, (k * ebi).astype(BF16))
        a_up = _dot_nt(qei, (k * eb).astype(BF16))
        attn = jnp.where(lower, a_lo, a_up).astype(BF16)
        s_old = sg_ref[...]
        inter = _dot(qe, s_old.astype(BF16))
        kdt = (k * jnp.exp(b_last - bcum)).T.astype(BF16)
        s_decay = jnp.exp(bcum.T[:, CHUNK - 1:CHUNK])
        sg_ref[...] = head_products(attn, kdt, v, inter, s_old, s_decay, rows, 0)

        cs, sn = cos_ref[rows, :], sin_ref[rows, :]

        def rope(t):
            partner = jnp.where(first_half, pltpu.roll(t, HK - DK // 2, 1), pltpu.roll(t, DK // 2, 1))
            return t * cs + partner * sn

        qr = rope(proj_ref[rows, C_RQ:C_RQ + HK])
        kr = rope(proj_ref[rows, C_RK:C_RK + HK]) * (DK ** -0.5)
        v = proj_ref[rows, C_RV:C_RV + HV].astype(BF16)
        qm = _stack_heads(qr.astype(BF16))
        scores = (_dot_nt(qm, kr.astype(BF16)) * dmat_ref[...]).astype(BF16)
        s_old = sr_ref[...]
        inter = _dot(qm, s_old.astype(BF16)) * qdec_ref[...]
        kdt = (kr * kdec_ref[...]).T.astype(BF16)
        sr_ref[...] = head_products(scores, kdt, v, inter, s_old, sdec_ref[...], rows, HV)

    ys = []
    for h in range(2 * HEADS):
        o = osc_ref[:, h * DV:(h + 1) * DV]
        if h < HEADS:
            gate = proj_ref[:, C_GG + h * DV:C_GG + (h + 1) * DV]
            yn = o * lax.rsqrt(jnp.mean(o * o, axis=-1, keepdims=True) + EPS) * glan_ref[h:h + 1, :]
        else:
            hh = h - HEADS
            gate = proj_ref[:, C_RG + hh * DV:C_RG + (hh + 1) * DV]
            d = o - jnp.mean(o, axis=-1, keepdims=True)
            yn = d * lax.rsqrt(jnp.mean(d * d, axis=-1, keepdims=True) + EPS) * retn_ref[hh:hh + 1, :]
        ys.append((yn * (gate / (1.0 + jnp.exp(-gate)))).astype(BF16))
    y = jnp.concatenate(ys, axis=1)
    o_ref[0] = x + _dot(y, wout_ref[...])


def _xattn_mlp_kernel(h_ref, kt_ref, v_ref, gx_ref, wq_ref, wo_ref, gf_ref, w1_ref, w2_ref, gfin_ref,
                      o_ref):
    h = h_ref[0]
    hn = _rms(h, gx_ref[...]).astype(BF16)
    q = (_dot(hn, wq_ref[...]) * (XATTN_DH ** -0.5)).astype(BF16)
    heads = []
    for a in range(XATTN_HEADS):
        cols = slice(a * XATTN_DH, (a + 1) * XATTN_DH)
        s = _dot(q[:, cols], kt_ref[0, a])
        e = jnp.exp(s - jnp.max(s, axis=-1, keepdims=True))
        pv = _dot(e.astype(BF16), v_ref[0, :, cols])
        heads.append((pv / jnp.sum(e, axis=-1, keepdims=True)).astype(BF16))
    h = h + _dot(jnp.concatenate(heads, axis=1), wo_ref[...])

    hn = _rms(h, gf_ref[...]).astype(BF16)
    for c in range(D_FF // FF_BLOCK):
        cols = slice(c * FF_BLOCK, (c + 1) * FF_BLOCK)
        u = jnp.maximum(_dot(hn, w1_ref[:, cols]), 0.0)
        h = h + _dot((u * u).astype(BF16), w2_ref[cols, :])
    o_ref[0] = _rms(h, gfin_ref[...])


def _const_spec(shape):
    n = len(shape)
    return pl.BlockSpec(shape, lambda *_: (0,) * n, pipeline_mode=pl.Buffered(1))


def _retention_tables():
    gamma = 1.0 - np.exp2(-np.linspace(RET_DECAY_EXP_LO, RET_DECAY_EXP_HI, HEADS, dtype=np.float32))
    log_g = np.log(gamma.astype(np.float32)).astype(np.float32)
    pos = np.arange(CHUNK, dtype=np.float32)
    dmat = np.exp(np.abs(pos[:, None] - pos[None, :])[None] * log_g[:, None, None])
    k_dec = np.exp((CHUNK - 1.0 - pos)[None, :] * log_g[:, None])
    q_dec = np.exp((pos + 1.0)[None, :] * log_g[:, None])
    s_dec = np.exp(CHUNK * log_g)
    dmat_s = dmat.reshape(HEADS * CHUNK, CHUNK)
    qdec_s = np.broadcast_to(q_dec.reshape(HEADS * CHUNK, 1), (HEADS * CHUNK, DV))
    kdec_s = np.repeat(k_dec.T, DK, axis=1)
    sdec_s = np.broadcast_to(np.repeat(s_dec, DK)[:, None], (HEADS * DK, DV))
    as32 = lambda a: jnp.asarray(np.ascontiguousarray(a), dtype=F32)
    return as32(dmat_s), as32(qdec_s), as32(kdec_s), as32(sdec_s)


def kernel(x, mem, norm_mix_g, w_in, gla_w_gate, gla_b_gate, gla_norm_g, ret_norm_g, w_out, norm_x_g,
           norm_mem_g, xa_w_q, xa_w_kv, xa_w_o, norm_ffn_g, ffn_w1, ffn_w2, final_norm_g):
    B, S, D = x.shape
    n_mem = mem.shape[1]
    assert D == D_MODEL and S % SEQ_TILE == 0 and w_in.shape[0] == 1
    ts = SEQ_TILE
    n_tiles = S // ts

    inv_freq = ROPE_BASE ** (-jnp.arange(0, DK, 2, dtype=F32) / DK)
    ang = jnp.arange(S, dtype=F32)[:, None] * inv_freq[None, :]
    cos, sin = jnp.cos(ang), jnp.sin(ang)
    cos_t = jnp.tile(jnp.concatenate([cos, cos], axis=1), (1, HEADS))
    sin_t = jnp.tile(jnp.concatenate([-sin, sin], axis=1), (1, HEADS))

    wi = w_in[0]
    o_gq, o_gk, o_gv = 0, HK, 2 * HK
    o_lr = o_gv + HV
    o_gg = o_lr + GATE_RANK
    o_rq = o_gg + HV
    w_in_r = jnp.concatenate(
        [wi[:, o_gq:o_lr], wi[:, o_gg:o_gg + HV], wi[:, o_rq:],
         wi[:, o_lr:o_lr + GATE_RANK], jnp.zeros((D, GATE_RANK_PAD - GATE_RANK), wi.dtype)],
        axis=1).astype(BF16)
    w_gate_p = jnp.concatenate(
        [gla_w_gate[0], jnp.zeros((GATE_RANK_PAD - GATE_RANK, HK), F32)], axis=0).astype(BF16)
    dmat_s, qdec_s, kdec_s, sdec_s = _retention_tables()

    params = pltpu.CompilerParams(dimension_semantics=("arbitrary", "arbitrary"),
                                  vmem_limit_bytes=VMEM_LIMIT_BYTES)

    kt, v = pl.pallas_call(
        _mem_kv_kernel,
        grid=(B,),
        in_specs=[pl.BlockSpec((1, n_mem, D), lambda b: (b, 0, 0)),
                  _const_spec((1, D)), _const_spec((D, 2 * D))],
        out_specs=[pl.BlockSpec((1, XATTN_HEADS, XATTN_DH, n_mem), lambda b: (b, 0, 0, 0)),
                   pl.BlockSpec((1, n_mem, D), lambda b: (b, 0, 0))],
        out_shape=[jax.ShapeDtypeStruct((B, XATTN_HEADS, XATTN_DH, n_mem), BF16),
                   jax.ShapeDtypeStruct((B, n_mem, D), BF16)],
        compiler_params=pltpu.CompilerParams(dimension_semantics=("arbitrary",),
                                             vmem_limit_bytes=VMEM_LIMIT_BYTES),
        name="mem_kv",
    )(mem, norm_mem_g[0][None, :], xa_w_kv[0].astype(BF16))

    h1 = pl.pallas_call(
        functools.partial(_mixer_kernel, n_chunks=ts // CHUNK),
        grid=(B, n_tiles),
        in_specs=[pl.BlockSpec((1, ts, D), lambda b, t: (b, t, 0)),
                  pl.BlockSpec((ts, HK), lambda b, t: (t, 0)),
                  pl.BlockSpec((ts, HK), lambda b, t: (t, 0)),
                  _const_spec((1, D)), _const_spec((D, D_PROJ)),
                  _const_spec((GATE_RANK_PAD, HK)), _const_spec((1, HK)),
                  _const_spec((HEADS, DV)), _const_spec((HEADS, DV)),
                  _const_spec((D, D)),
                  _const_spec((HEADS * CHUNK, CHUNK)), _const_spec((HEADS * CHUNK, DV)),
                  _const_spec((CHUNK, HK)), _const_spec((HEADS * DK, DV))],
        out_specs=pl.BlockSpec((1, ts, D), lambda b, t: (b, t, 0)),
        out_shape=jax.ShapeDtypeStruct((B, S, D), F32),
        scratch_shapes=[pltpu.VMEM((ts, D_PROJ), F32), pltpu.VMEM((ts, 2 * HV), F32),
                        pltpu.VMEM((HEADS * DK, DV), F32), pltpu.VMEM((HEADS * DK, DV), F32)],
        compiler_params=params,
        name="mixer",
    )(x, cos_t, sin_t, norm_mix_g[0][None, :], w_in_r, w_gate_p, gla_b_gate[0][None, :],
      gla_norm_g[0], ret_norm_g[0], w_out[0].astype(BF16), dmat_s, qdec_s, kdec_s, sdec_s)

    out = pl.pallas_call(
        _xattn_mlp_kernel,
        grid=(B, n_tiles),
        in_specs=[pl.BlockSpec((1, ts, D), lambda b, t: (b, t, 0)),
                  pl.BlockSpec((1, XATTN_HEADS, XATTN_DH, n_mem), lambda b, t: (b, 0, 0, 0)),
                  pl.BlockSpec((1, n_mem, D), lambda b, t: (b, 0, 0)),
                  _const_spec((1, D)), _const_spec((D, D)), _const_spec((D, D)),
                  _const_spec((1, D)), _const_spec((D, D_FF)), _const_spec((D_FF, D)),
                  _const_spec((1, D))],
        out_specs=pl.BlockSpec((1, ts, D), lambda b, t: (b, t, 0)),
        out_shape=jax.ShapeDtypeStruct((B, S, D), F32),
        compiler_params=params,
        name="xattn_mlp",
    )(h1, kt, v, norm_x_g[0][None, :], xa_w_q[0].astype(BF16), xa_w_o[0].astype(BF16),
      norm_ffn_g[0][None, :], ffn_w1[0].astype(BF16), ffn_w2[0].astype(BF16), final_norm_g[None, :])
    return out
```

```python
import functools

import numpy as np
import jax
import jax.numpy as jnp
from jax import lax
from jax.experimental import pallas as pl
from jax.experimental.pallas import tpu as pltpu

D_MODEL = 1024
CHUNK = 64
HEADS = 4
DK = 64
DV = 128
GATE_RANK = 16
GATE_RANK_PAD = 128
GATE_TAU = 16.0
RET_DECAY_EXP_LO = 5.0
RET_DECAY_EXP_HI = 12.0
ROPE_BASE = 10000.0
XATTN_HEADS = 4
XATTN_DH = D_MODEL // XATTN_HEADS
D_FF = 4 * D_MODEL
FF_BLOCK = 1024
EPS = 1e-6

HK = HEADS * DK
HV = HEADS * DV
C_GQ, C_GK, C_GV, C_GG = 0, HK, 2 * HK, 2 * HK + HV
C_RQ = C_GG + HV
C_RK = C_RQ + HK
C_RV = C_RK + HK
C_RG = C_RV + HV
C_LR = C_RG + HV
D_PROJ = C_LR + GATE_RANK_PAD

SEQ_TILE = 512
PROJ_PIECE = 256
CUM_BLOCK = 256
VMEM_LIMIT_BYTES = 56 * 1024 * 1024

F32 = jnp.float32
BF16 = jnp.bfloat16


def _dot(a, b):
    return jnp.dot(a, b, preferred_element_type=F32)


def _rms(x, g):
    return x * lax.rsqrt(jnp.mean(x * x, axis=-1, keepdims=True) + EPS) * g


def _swish(g):
    return g / (1.0 + jnp.exp(-g))


def _stack_heads(t):
    lane_head = lax.broadcasted_iota(jnp.int32, t.shape, 1) // DK
    zero = jnp.zeros_like(t)
    return jnp.concatenate([jnp.where(lane_head == h, t, zero) for h in range(HEADS)], axis=0)


def _mem_kv_kernel(mem_ref, g_ref, wkv_ref, kt_ref, v_ref):
    mn = _rms(mem_ref[0], g_ref[...]).astype(BF16)
    kv = _dot(mn, wkv_ref[...])
    for h in range(XATTN_HEADS):
        kt_ref[0, h] = kv[:, h * XATTN_DH:(h + 1) * XATTN_DH].T.astype(BF16)
    v_ref[0] = kv[:, D_MODEL:].astype(BF16)


def _mixer_kernel(xnext_ref, xres_ref, cos_ref, sin_ref, g_ref, win_ref, wg_ref, bg_ref, glan_ref,
                  retn_ref, wout_ref, dmat_ref, qdec_ref, kdect_ref, sdec_ref,
                  o_ref, proj_a_ref, proj_b_ref, xn_ref, y_ref, sg_ref, sr_ref, *, n_chunks, n_tiles):
    s = pl.program_id(0)
    prev_tile = jnp.maximum(s - 1, 0)

    @pl.when(s == 0)
    def _():
        proj_b_ref[...] = jnp.zeros_like(proj_b_ref)

    @pl.when(prev_tile % n_tiles == 0)
    def _():
        sg_ref[...] = jnp.zeros_like(sg_ref)
        sr_ref[...] = jnp.zeros_like(sr_ref)

    xn_ref[...] = _rms(xnext_ref[0], g_ref[...]).astype(BF16)

    chunk_pass = functools.partial(
        _chunk_pass, xn_ref=xn_ref, cos_ref=cos_ref, sin_ref=sin_ref, win_ref=win_ref, wg_ref=wg_ref,
        bg_ref=bg_ref, glan_ref=glan_ref, retn_ref=retn_ref, dmat_ref=dmat_ref, qdec_ref=qdec_ref,
        kdect_ref=kdect_ref, sdec_ref=sdec_ref, y_ref=y_ref, sg_ref=sg_ref, sr_ref=sr_ref,
        n_chunks=n_chunks)

    @pl.when(s % 2 == 0)
    def _():
        chunk_pass(proj_a_ref, proj_b_ref)

    @pl.when(s % 2 == 1)
    def _():
        chunk_pass(proj_b_ref, proj_a_ref)

    o_ref[0] = xres_ref[0] + _dot(y_ref[...], wout_ref[...])


def _chunk_pass(cur_ref, prv_ref, *, xn_ref, cos_ref, sin_ref, win_ref, wg_ref, bg_ref, glan_ref,
                retn_ref, dmat_ref, qdec_ref, kdect_ref, sdec_ref, y_ref, sg_ref, sr_ref, n_chunks):
    ts = n_chunks * CHUNK
    chunk_rows = [slice(c * CHUNK, (c + 1) * CHUNK) for c in range(n_chunks)]
    head_rows = [slice(h * CHUNK, (h + 1) * CHUNK) for h in range(HEADS)]
    head_cols = [slice(h * DV, (h + 1) * DV) for h in range(HEADS)]
    pieces = iter([(i * PROJ_PIECE, min((i + 1) * PROJ_PIECE, D_PROJ))
                   for i in range(-(-D_PROJ // PROJ_PIECE))])

    def emit_proj(n):
        for lo, hi in [p for p in (next(pieces, None) for _ in range(n)) if p is not None]:
            cur_ref[:, lo:hi] = _dot(xn_ref[...], win_ref[:, lo:hi])

    bi = lax.broadcasted_iota(jnp.int32, (CUM_BLOCK, CUM_BLOCK), 0)
    bj = lax.broadcasted_iota(jnp.int32, (CUM_BLOCK, CUM_BLOCK), 1)
    tril_blocks = ((bi // CHUNK == bj // CHUNK) & (bj <= bi)).astype(BF16)
    si = lax.broadcasted_iota(jnp.int32, (HEADS * CHUNK, CHUNK), 0) % CHUNK
    sj = lax.broadcasted_iota(jnp.int32, (HEADS * CHUNK, CHUNK), 1)
    lower = sj <= si
    first_half = (lax.broadcasted_iota(jnp.int32, (ts, HK), 1) % DK) < (DK // 2)
    lane_pad = jnp.zeros((HK, CHUNK), BF16)

    def contributions(kdt, v):
        return jnp.concatenate([_dot(kdt[head_rows[h]], v[:, head_cols[h]]) for h in range(HEADS)],
                               axis=0)

    def scan_states(s_ref, decays, contribs):
        state, starts = s_ref[...], []
        for dec, con in zip(decays, contribs):
            starts.append(state.astype(BF16))
            state = state * dec + con
        s_ref[...] = state
        return starts

    z = _dot(prv_ref[:, C_LR:C_LR + GATE_RANK_PAD].astype(BF16), wg_ref[...]) + bg_ref[...]
    emit_proj(1)
    log_a = (jnp.minimum(z, 0.0) - jnp.log1p(jnp.exp(-jnp.abs(z)))) * (1.0 / GATE_TAU)
    la_hi = log_a.astype(BF16)
    la_lo = (log_a - la_hi.astype(F32)).astype(BF16)
    la_split = jnp.concatenate([la_hi, la_lo], axis=1)
    bcum = []
    for b in range(ts // CUM_BLOCK):
        r = _dot(tril_blocks, la_split[b * CUM_BLOCK:(b + 1) * CUM_BLOCK])
        bcum.append(r[:, :HK] + r[:, HK:])
    bcum = jnp.concatenate(bcum, axis=0)
    emit_proj(1)
    q = prv_ref[:, C_GQ:C_GQ + HK] * (DK ** -0.5)
    qeb = (q * jnp.exp(bcum)).astype(BF16)
    qebi = (q * jnp.exp(-bcum)).astype(BF16)

    kebt, kebit, g_decay, g_contrib = [], [], [], []
    for c, rows in enumerate(chunk_rows):
        kt = prv_ref[rows, C_GK:C_GK + HK].T
        bct = bcum[rows].T
        blt = bct[:, CHUNK - 1:CHUNK]
        kebt.append((kt * jnp.exp(bct)).astype(BF16))
        kebit.append((kt * jnp.exp(-bct)).astype(BF16))
        kdt = (kt * jnp.exp(blt - bct)).astype(BF16)
        g_decay.append(jnp.exp(blt))
        g_contrib.append(contributions(kdt, prv_ref[rows, C_GV:C_GV + HV].astype(BF16)))
        if c % 4 == 3:
            emit_proj(1)

    cs, sn = cos_ref[...], sin_ref[...]

    def rope(t):
        partner = jnp.where(first_half, pltpu.roll(t, HK - DK // 2, 1), pltpu.roll(t, DK // 2, 1))
        return t * cs + partner * sn

    qrb = rope(prv_ref[:, C_RQ:C_RQ + HK]).astype(BF16)
    kr = rope(prv_ref[:, C_RK:C_RK + HK]) * (DK ** -0.5)
    krt, r_contrib = [], []
    for c, rows in enumerate(chunk_rows):
        kt = kr[rows].T
        krt.append(kt.astype(BF16))
        kdt = (kt * kdect_ref[...]).astype(BF16)
        r_contrib.append(contributions(kdt, prv_ref[rows, C_RV:C_RV + HV].astype(BF16)))
        if c % 4 == 3:
            emit_proj(1)

    g_start = scan_states(sg_ref, g_decay, g_contrib)
    r_start = scan_states(sr_ref, [sdec_ref[...]] * n_chunks, r_contrib)

    for c, rows in enumerate(chunk_rows):
        r = _dot(_stack_heads(qeb[rows]), jnp.concatenate([g_start[c], kebit[c], lane_pad], axis=1))
        a_up = _dot(_stack_heads(qebi[rows]), kebt[c])
        attn = jnp.where(lower, r[:, DV:DV + CHUNK], a_up).astype(BF16)
        v = prv_ref[rows, C_GV:C_GV + HV].astype(BF16)
        for h in range(HEADS):
            o = _dot(attn[head_rows[h]], v[:, head_cols[h]]) + r[head_rows[h], :DV]
            gate = prv_ref[rows, C_GG + h * DV:C_GG + (h + 1) * DV]
            yn = o * lax.rsqrt(jnp.mean(o * o, axis=-1, keepdims=True) + EPS) * glan_ref[h:h + 1, :]
            y_ref[rows, h * DV:(h + 1) * DV] = (yn * _swish(gate)).astype(BF16)
        if c % 2 == 1:
            emit_proj(1)

    for c, rows in enumerate(chunk_rows):
        r = _dot(_stack_heads(qrb[rows]), jnp.concatenate([r_start[c], krt[c], lane_pad], axis=1))
        inter = r[:, :DV] * qdec_ref[...]
        scores = (r[:, DV:DV + CHUNK] * dmat_ref[...]).astype(BF16)
        v = prv_ref[rows, C_RV:C_RV + HV].astype(BF16)
        for h in range(HEADS):
            o = _dot(scores[head_rows[h]], v[:, head_cols[h]]) + inter[head_rows[h]]
            gate = prv_ref[rows, C_RG + h * DV:C_RG + (h + 1) * DV]
            d = o - jnp.mean(o, axis=-1, keepdims=True)
            yn = d * lax.rsqrt(jnp.mean(d * d, axis=-1, keepdims=True) + EPS) * retn_ref[h:h + 1, :]
            y_ref[rows, HV + h * DV:HV + (h + 1) * DV] = (yn * _swish(gate)).astype(BF16)
        if c % 2 == 1:
            emit_proj(1)
    emit_proj(D_PROJ // PROJ_PIECE + 1)


def _xattn_mlp_kernel(h_ref, kt_ref, v_ref, gx_ref, wq_ref, wo_ref, gf_ref, w1_ref, w2_ref, gfin_ref,
                      o_ref):
    h = h_ref[0]
    hn = _rms(h, gx_ref[...]).astype(BF16)
    q = (_dot(hn, wq_ref[...]) * (XATTN_DH ** -0.5)).astype(BF16)
    heads = []
    for a in range(XATTN_HEADS):
        cols = slice(a * XATTN_DH, (a + 1) * XATTN_DH)
        s = _dot(q[:, cols], kt_ref[0, a])
        e = jnp.exp(s - jnp.max(s, axis=-1, keepdims=True))
        pv = _dot(e.astype(BF16), v_ref[0, :, cols])
        heads.append((pv / jnp.sum(e, axis=-1, keepdims=True)).astype(BF16))
    h = h + _dot(jnp.concatenate(heads, axis=1), wo_ref[...])

    hn = _rms(h, gf_ref[...]).astype(BF16)
    for c in range(D_FF // FF_BLOCK):
        cols = slice(c * FF_BLOCK, (c + 1) * FF_BLOCK)
        u = jnp.maximum(_dot(hn, w1_ref[:, cols]), 0.0)
        h = h + _dot((u * u).astype(BF16), w2_ref[cols, :])
    o_ref[0] = _rms(h, gfin_ref[...])


def _const_spec(shape):
    n = len(shape)
    return pl.BlockSpec(shape, lambda *_: (0,) * n, pipeline_mode=pl.Buffered(1))


def _retention_tables():
    gamma = 1.0 - np.exp2(-np.linspace(RET_DECAY_EXP_LO, RET_DECAY_EXP_HI, HEADS, dtype=np.float32))
    log_g = np.log(gamma.astype(np.float32)).astype(np.float32)
    pos = np.arange(CHUNK, dtype=np.float32)
    dmat = np.exp(np.abs(pos[:, None] - pos[None, :])[None] * log_g[:, None, None])
    k_dec = np.exp((CHUNK - 1.0 - pos)[None, :] * log_g[:, None])
    q_dec = np.exp((pos + 1.0)[None, :] * log_g[:, None])
    s_dec = np.exp(CHUNK * log_g)
    dmat_s = dmat.reshape(HEADS * CHUNK, CHUNK)
    qdec_s = np.broadcast_to(q_dec.reshape(HEADS * CHUNK, 1), (HEADS * CHUNK, DV))
    kdect_s = np.repeat(k_dec, DK, axis=0)
    sdec_s = np.broadcast_to(np.repeat(s_dec, DK)[:, None], (HEADS * DK, DV))
    as32 = lambda a: jnp.asarray(np.ascontiguousarray(a), dtype=F32)
    return as32(dmat_s), as32(qdec_s), as32(kdect_s), as32(sdec_s)


def kernel(x, mem, norm_mix_g, w_in, gla_w_gate, gla_b_gate, gla_norm_g, ret_norm_g, w_out, norm_x_g,
           norm_mem_g, xa_w_q, xa_w_kv, xa_w_o, norm_ffn_g, ffn_w1, ffn_w2, final_norm_g):
    B, S, D = x.shape
    n_mem = mem.shape[1]
    assert D == D_MODEL and S % SEQ_TILE == 0 and w_in.shape[0] == 1
    ts = SEQ_TILE
    n_tiles = S // ts

    inv_freq = ROPE_BASE ** (-jnp.arange(0, DK, 2, dtype=F32) / DK)
    ang = jnp.arange(S, dtype=F32)[:, None] * inv_freq[None, :]
    cos, sin = jnp.cos(ang), jnp.sin(ang)
    cos_t = jnp.tile(jnp.concatenate([cos, cos], axis=1), (1, HEADS))
    sin_t = jnp.tile(jnp.concatenate([-sin, sin], axis=1), (1, HEADS))

    wi = w_in[0]
    o_gq, o_gk, o_gv = 0, HK, 2 * HK
    o_lr = o_gv + HV
    o_gg = o_lr + GATE_RANK
    o_rq = o_gg + HV
    w_in_r = jnp.concatenate(
        [wi[:, o_gq:o_lr], wi[:, o_gg:o_gg + HV], wi[:, o_rq:],
         wi[:, o_lr:o_lr + GATE_RANK], jnp.zeros((D, GATE_RANK_PAD - GATE_RANK), wi.dtype)],
        axis=1).astype(BF16)
    w_gate_p = jnp.concatenate(
        [gla_w_gate[0], jnp.zeros((GATE_RANK_PAD - GATE_RANK, HK), F32)], axis=0).astype(BF16)
    dmat_s, qdec_s, kdect_s, sdec_s = _retention_tables()

    params = pltpu.CompilerParams(dimension_semantics=("arbitrary", "arbitrary"),
                                  vmem_limit_bytes=VMEM_LIMIT_BYTES)

    kt, v = pl.pallas_call(
        _mem_kv_kernel,
        grid=(B,),
        in_specs=[pl.BlockSpec((1, n_mem, D), lambda b: (b, 0, 0)),
                  _const_spec((1, D)), _const_spec((D, 2 * D))],
        out_specs=[pl.BlockSpec((1, XATTN_HEADS, XATTN_DH, n_mem), lambda b: (b, 0, 0, 0)),
                   pl.BlockSpec((1, n_mem, D), lambda b: (b, 0, 0))],
        out_shape=[jax.ShapeDtypeStruct((B, XATTN_HEADS, XATTN_DH, n_mem), BF16),
                   jax.ShapeDtypeStruct((B, n_mem, D), BF16)],
        compiler_params=pltpu.CompilerParams(dimension_semantics=("arbitrary",),
                                             vmem_limit_bytes=VMEM_LIMIT_BYTES),
        name="mem_kv",
    )(mem, norm_mem_g[0][None, :], xa_w_kv[0].astype(BF16))

    n_steps = B * n_tiles + 1

    def next_tile(s):
        p = jnp.minimum(s, n_steps - 2)
        return (p // n_tiles, p % n_tiles, 0)

    def prev_tile(s):
        q = jnp.maximum(s - 1, 0)
        return (q // n_tiles, q % n_tiles, 0)

    def prev_rope(s):
        return (jnp.maximum(s - 1, 0) % n_tiles, 0)

    h1 = pl.pallas_call(
        functools.partial(_mixer_kernel, n_chunks=ts // CHUNK, n_tiles=n_tiles),
        grid=(n_steps,),
        in_specs=[pl.BlockSpec((1, ts, D), next_tile),
                  pl.BlockSpec((1, ts, D), prev_tile),
                  pl.BlockSpec((ts, HK), prev_rope),
                  pl.BlockSpec((ts, HK), prev_rope),
                  _const_spec((1, D)), _const_spec((D, D_PROJ)),
                  _const_spec((GATE_RANK_PAD, HK)), _const_spec((1, HK)),
                  _const_spec((HEADS, DV)), _const_spec((HEADS, DV)),
                  _const_spec((D, D)),
                  _const_spec((HEADS * CHUNK, CHUNK)), _const_spec((HEADS * CHUNK, DV)),
                  _const_spec((HEADS * DK, CHUNK)), _const_spec((HEADS * DK, DV))],
        out_specs=pl.BlockSpec((1, ts, D), prev_tile),
        out_shape=jax.ShapeDtypeStruct((B, S, D), F32),
        scratch_shapes=[pltpu.VMEM((ts, D_PROJ), F32), pltpu.VMEM((ts, D_PROJ), F32),
                        pltpu.VMEM((ts, D), BF16), pltpu.VMEM((ts, D), BF16),
                        pltpu.VMEM((HEADS * DK, DV), F32), pltpu.VMEM((HEADS * DK, DV), F32)],
        compiler_params=pltpu.CompilerParams(dimension_semantics=("arbitrary",),
                                             vmem_limit_bytes=VMEM_LIMIT_BYTES),
        name="mixer",
    )(x, x, cos_t, sin_t, norm_mix_g[0][None, :], w_in_r, w_gate_p, gla_b_gate[0][None, :],
      gla_norm_g[0], ret_norm_g[0], w_out[0].astype(BF16), dmat_s, qdec_s, kdect_s, sdec_s)

    out = pl.pallas_call(
        _xattn_mlp_kernel,
        grid=(B, n_tiles),
        in_specs=[pl.BlockSpec((1, ts, D), lambda b, t: (b, t, 0)),
                  pl.BlockSpec((1, XATTN_HEADS, XATTN_DH, n_mem), lambda b, t: (b, 0, 0, 0)),
                  pl.BlockSpec((1, n_mem, D), lambda b, t: (b, 0, 0)),
                  _const_spec((1, D)), _const_spec((D, D)), _const_spec((D, D)),
                  _const_spec((1, D)), _const_spec((D, D_FF)), _const_spec((D_FF, D)),
                  _const_spec((1, D))],
        out_specs=pl.BlockSpec((1, ts, D), lambda b, t: (b, t, 0)),
        out_shape=jax.ShapeDtypeStruct((B, S, D), F32),
        compiler_params=params,
        name="xattn_mlp",
    )(h1, kt, v, norm_x_g[0][None, :], xa_w_q[0].astype(BF16), xa_w_o[0].astype(BF16),
      norm_ffn_g[0][None, :], ffn_w1[0].astype(BF16), ffn_w2[0].astype(BF16), final_norm_g[None, :])
    return out
```

```python
import functools

import numpy as np
import jax
import jax.numpy as jnp
from jax import lax
from jax.experimental import pallas as pl
from jax.experimental.pallas import tpu as pltpu

D_MODEL = 1024
CHUNK = 64
HEADS = 4
DK = 64
DV = 128
GATE_RANK = 16
GATE_RANK_PAD = 128
GATE_TAU = 16.0
RET_DECAY_EXP_LO = 5.0
RET_DECAY_EXP_HI = 12.0
ROPE_BASE = 10000.0
XATTN_HEADS = 4
XATTN_DH = D_MODEL // XATTN_HEADS
D_FF = 4 * D_MODEL
FF_BLOCK = 1024
EPS = 1e-6

HK = HEADS * DK
HV = HEADS * DV
C_GQ, C_GK, C_GV, C_GG = 0, HK, 2 * HK, 2 * HK + HV
C_RQ = C_GG + HV
C_RK = C_RQ + HK
C_RV = C_RK + HK
C_RG = C_RV + HV
C_LR = C_RG + HV
D_PROJ = C_LR + GATE_RANK_PAD

SEQ_TILE = 512
PROJ_PIECE = 256
CUM_BLOCK = 256
OUT_ROWS = 128
VMEM_LIMIT_BYTES = 56 * 1024 * 1024

F32 = jnp.float32
BF16 = jnp.bfloat16


def _dot(a, b):
    return jnp.dot(a, b, preferred_element_type=F32)


def _rms(x, g):
    return x * lax.rsqrt(jnp.mean(x * x, axis=-1, keepdims=True) + EPS) * g


def _swish(g):
    return g / (1.0 + jnp.exp(-g))


def _stack_heads(t):
    lane_head = lax.broadcasted_iota(jnp.int32, t.shape, 1) // DK
    zero = jnp.zeros_like(t)
    return jnp.concatenate([jnp.where(lane_head == h, t, zero) for h in range(HEADS)], axis=0)


def _mem_kv_kernel(mem_ref, g_ref, wkv_ref, kt_ref, v_ref):
    mn = _rms(mem_ref[0], g_ref[...]).astype(BF16)
    kv = _dot(mn, wkv_ref[...])
    for h in range(XATTN_HEADS):
        kt_ref[0, h] = kv[:, h * XATTN_DH:(h + 1) * XATTN_DH].T.astype(BF16)
    v_ref[0] = kv[:, D_MODEL:].astype(BF16)


def _mixer_kernel(xnext_ref, xres_ref, cosb_ref, sinb_ref, coso_ref, sino_ref, g_ref, win_ref, wg_ref,
                  bg_ref, glan_ref,
                  retn_ref, wout_ref, dmat_ref, qdec_ref, kdect_ref, sdec_ref,
                  o_ref, proj_a_ref, proj_b_ref, xn_ref, y_ref, kt_ref, sg_ref, sr_ref, *,
                  n_chunks, n_tiles):
    s = pl.program_id(0)
    prev_tile = jnp.maximum(s - 1, 0)

    @pl.when(s == 0)
    def _():
        proj_b_ref[...] = jnp.zeros_like(proj_b_ref)

    @pl.when(prev_tile % n_tiles == 0)
    def _():
        sg_ref[...] = jnp.zeros_like(sg_ref)
        sr_ref[...] = jnp.zeros_like(sr_ref)

    xn_ref[...] = _rms(xnext_ref[0], g_ref[...]).astype(BF16)

    chunk_pass = functools.partial(
        _chunk_pass, xn_ref=xn_ref, rope_refs=(cosb_ref, sinb_ref, coso_ref, sino_ref),
        win_ref=win_ref, wg_ref=wg_ref,
        bg_ref=bg_ref, glan_ref=glan_ref, retn_ref=retn_ref, dmat_ref=dmat_ref, qdec_ref=qdec_ref,
        kdect_ref=kdect_ref, sdec_ref=sdec_ref, y_ref=y_ref, kt_ref=kt_ref, sg_ref=sg_ref,
        sr_ref=sr_ref,
        xres_ref=xres_ref, wout_ref=wout_ref, o_ref=o_ref, n_chunks=n_chunks)

    @pl.when(s % 2 == 0)
    def _():
        chunk_pass(proj_a_ref, proj_b_ref)

    @pl.when(s % 2 == 1)
    def _():
        chunk_pass(proj_b_ref, proj_a_ref)


def _chunk_pass(cur_ref, prv_ref, *, xn_ref, rope_refs, win_ref, wg_ref, bg_ref, glan_ref,
                retn_ref, dmat_ref, qdec_ref, kdect_ref, sdec_ref, y_ref, kt_ref, sg_ref, sr_ref,
                xres_ref, wout_ref, o_ref, n_chunks):
    ts = n_chunks * CHUNK
    chunk_rows = [slice(c * CHUNK, (c + 1) * CHUNK) for c in range(n_chunks)]
    head_rows = [slice(h * CHUNK, (h + 1) * CHUNK) for h in range(HEADS)]
    head_cols = [slice(h * DV, (h + 1) * DV) for h in range(HEADS)]
    pieces = iter([(i * PROJ_PIECE, min((i + 1) * PROJ_PIECE, D_PROJ))
                   for i in range(-(-D_PROJ // PROJ_PIECE))])

    def emit_proj(n):
        for lo, hi in [p for p in (next(pieces, None) for _ in range(n)) if p is not None]:
            cur_ref[:, lo:hi] = _dot(xn_ref[...], win_ref[:, lo:hi])

    bi = lax.broadcasted_iota(jnp.int32, (CUM_BLOCK, CUM_BLOCK), 0)
    bj = lax.broadcasted_iota(jnp.int32, (CUM_BLOCK, CUM_BLOCK), 1)
    tril_blocks = ((bi // CHUNK == bj // CHUNK) & (bj <= bi)).astype(BF16)
    si = lax.broadcasted_iota(jnp.int32, (HEADS * CHUNK, CHUNK), 0) % CHUNK
    sj = lax.broadcasted_iota(jnp.int32, (HEADS * CHUNK, CHUNK), 1)
    lower = sj <= si
    first_half = (lax.broadcasted_iota(jnp.int32, (ts, HK), 1) % DK) < (DK // 2)
    lane_pad = jnp.zeros((HK, CHUNK), BF16)

    def contributions(kdt, v):
        return jnp.concatenate([_dot(kdt[head_rows[h]], v[:, head_cols[h]]) for h in range(HEADS)],
                               axis=0)

    def scan_states(s_ref, decays, contribs):
        state, starts = s_ref[...], []
        for dec, con in zip(decays, contribs):
            starts.append(state.astype(BF16))
            state = state * dec + con
        s_ref[...] = state
        return starts

    z = _dot(prv_ref[:, C_LR:C_LR + GATE_RANK_PAD].astype(BF16), wg_ref[...]) + bg_ref[...]
    emit_proj(1)
    log_a = (jnp.minimum(z, 0.0) - jnp.log1p(jnp.exp(-jnp.abs(z)))) * (1.0 / GATE_TAU)
    la_hi = log_a.astype(BF16)
    la_lo = (log_a - la_hi.astype(F32)).astype(BF16)
    la_split = jnp.concatenate([la_hi, la_lo], axis=1)
    bcum = []
    for b in range(ts // CUM_BLOCK):
        r = _dot(tril_blocks, la_split[b * CUM_BLOCK:(b + 1) * CUM_BLOCK])
        bcum.append(r[:, :HK] + r[:, HK:])
    bcum = jnp.concatenate(bcum, axis=0)
    emit_proj(1)
    q = prv_ref[:, C_GQ:C_GQ + HK] * (DK ** -0.5)
    qeb = (q * jnp.exp(bcum)).astype(BF16)
    qebi = (q * jnp.exp(-bcum)).astype(BF16)

    g_decay, g_contrib = [], []
    for c, rows in enumerate(chunk_rows):
        kt = prv_ref[rows, C_GK:C_GK + HK].T
        bct = bcum[rows].T
        blt = bct[:, CHUNK - 1:CHUNK]
        kt_ref[0, c] = (kt * jnp.exp(bct)).astype(BF16)
        kt_ref[1, c] = (kt * jnp.exp(-bct)).astype(BF16)
        kdt = (kt * jnp.exp(blt - bct)).astype(BF16)
        g_decay.append(jnp.exp(blt))
        g_contrib.append(contributions(kdt, prv_ref[rows, C_GV:C_GV + HV].astype(BF16)))
        if c % 4 == 3:
            emit_proj(1)

    cosb, sinb, coso, sino = [r[...] for r in rope_refs]
    cosb, sinb = cosb[0], sinb[0]
    cs = cosb * coso - sinb * sino
    sn = sinb * coso + cosb * sino
    sn = jnp.where(first_half, -sn, sn)

    def rope(t):
        partner = jnp.where(first_half, pltpu.roll(t, HK - DK // 2, 1), pltpu.roll(t, DK // 2, 1))
        return t * cs + partner * sn

    qrb = rope(prv_ref[:, C_RQ:C_RQ + HK]).astype(BF16)
    kr = rope(prv_ref[:, C_RK:C_RK + HK]) * (DK ** -0.5)
    r_contrib = []
    for c, rows in enumerate(chunk_rows):
        kt = kr[rows].T
        kt_ref[2, c] = kt.astype(BF16)
        kdt = (kt * kdect_ref[...]).astype(BF16)
        r_contrib.append(contributions(kdt, prv_ref[rows, C_RV:C_RV + HV].astype(BF16)))
        if c % 4 == 3:
            emit_proj(1)

    g_start = scan_states(sg_ref, g_decay, g_contrib)
    r_start = scan_states(sr_ref, [sdec_ref[...]] * n_chunks, r_contrib)

    for c, rows in enumerate(chunk_rows):
        r = _dot(_stack_heads(qeb[rows]), jnp.concatenate([g_start[c], kt_ref[1, c], lane_pad], axis=1))
        a_up = _dot(_stack_heads(qebi[rows]), kt_ref[0, c])
        attn = jnp.where(lower, r[:, DV:DV + CHUNK], a_up).astype(BF16)
        v = prv_ref[rows, C_GV:C_GV + HV].astype(BF16)
        for h in range(HEADS):
            o = _dot(attn[head_rows[h]], v[:, head_cols[h]]) + r[head_rows[h], :DV]
            gate = prv_ref[rows, C_GG + h * DV:C_GG + (h + 1) * DV]
            yn = o * lax.rsqrt(jnp.mean(o * o, axis=-1, keepdims=True) + EPS) * glan_ref[h:h + 1, :]
            y_ref[rows, h * DV:(h + 1) * DV] = (yn * _swish(gate)).astype(BF16)
        emit_proj(1)

        r = _dot(_stack_heads(qrb[rows]), jnp.concatenate([r_start[c], kt_ref[2, c], lane_pad], axis=1))
        inter = r[:, :DV] * qdec_ref[...]
        scores = (r[:, DV:DV + CHUNK] * dmat_ref[...]).astype(BF16)
        v = prv_ref[rows, C_RV:C_RV + HV].astype(BF16)
        for h in range(HEADS):
            o = _dot(scores[head_rows[h]], v[:, head_cols[h]]) + inter[head_rows[h]]
            gate = prv_ref[rows, C_RG + h * DV:C_RG + (h + 1) * DV]
            d = o - jnp.mean(o, axis=-1, keepdims=True)
            yn = d * lax.rsqrt(jnp.mean(d * d, axis=-1, keepdims=True) + EPS) * retn_ref[h:h + 1, :]
            y_ref[rows, HV + h * DV:HV + (h + 1) * DV] = (yn * _swish(gate)).astype(BF16)
        if (c + 1) % (OUT_ROWS // CHUNK) == 0:
            done = slice((c + 1) * CHUNK - OUT_ROWS, (c + 1) * CHUNK)
            o_ref[0, done, :] = xres_ref[0, done, :] + _dot(y_ref[done, :], wout_ref[...])
    assert next(pieces, None) is None


def _xattn_mlp_kernel(h_ref, kt_ref, v_ref, gx_ref, wq_ref, wo_ref, gf_ref, w1_ref, w2_ref, gfin_ref,
                      o_ref):
    h = h_ref[0]
    hn = _rms(h, gx_ref[...]).astype(BF16)
    q = (_dot(hn, wq_ref[...]) * (XATTN_DH ** -0.5)).astype(BF16)
    heads = []
    for a in range(XATTN_HEADS):
        cols = slice(a * XATTN_DH, (a + 1) * XATTN_DH)
        s = _dot(q[:, cols], kt_ref[0, a])
        e = jnp.exp(s - jnp.max(s, axis=-1, keepdims=True))
        pv = _dot(e.astype(BF16), v_ref[0, :, cols])
        heads.append((pv / jnp.sum(e, axis=-1, keepdims=True)).astype(BF16))
    h = h + _dot(jnp.concatenate(heads, axis=1), wo_ref[...])

    hn = _rms(h, gf_ref[...]).astype(BF16)
    for c in range(D_FF // FF_BLOCK):
        cols = slice(c * FF_BLOCK, (c + 1) * FF_BLOCK)
        u = jnp.maximum(_dot(hn, w1_ref[:, cols]), 0.0)
        h = h + _dot((u * u).astype(BF16), w2_ref[cols, :])
    o_ref[0] = _rms(h, gfin_ref[...])


def _const_spec(shape):
    n = len(shape)
    return pl.BlockSpec(shape, lambda *_: (0,) * n, pipeline_mode=pl.Buffered(1))


def _retention_tables():
    gamma = 1.0 - np.exp2(-np.linspace(RET_DECAY_EXP_LO, RET_DECAY_EXP_HI, HEADS, dtype=np.float32))
    log_g = np.log(gamma.astype(np.float32)).astype(np.float32)
    pos = np.arange(CHUNK, dtype=np.float32)
    dmat = np.exp(np.abs(pos[:, None] - pos[None, :])[None] * log_g[:, None, None])
    k_dec = np.exp((CHUNK - 1.0 - pos)[None, :] * log_g[:, None])
    q_dec = np.exp((pos + 1.0)[None, :] * log_g[:, None])
    s_dec = np.exp(CHUNK * log_g)
    dmat_s = dmat.reshape(HEADS * CHUNK, CHUNK)
    qdec_s = np.broadcast_to(q_dec.reshape(HEADS * CHUNK, 1), (HEADS * CHUNK, DV))
    kdect_s = np.repeat(k_dec, DK, axis=0)
    sdec_s = np.broadcast_to(np.repeat(s_dec, DK)[:, None], (HEADS * DK, DV))
    as32 = lambda a: jnp.asarray(np.ascontiguousarray(a), dtype=F32)
    return as32(dmat_s), as32(qdec_s), as32(kdect_s), as32(sdec_s)


def kernel(x, mem, norm_mix_g, w_in, gla_w_gate, gla_b_gate, gla_norm_g, ret_norm_g, w_out, norm_x_g,
           norm_mem_g, xa_w_q, xa_w_kv, xa_w_o, norm_ffn_g, ffn_w1, ffn_w2, final_norm_g):
    B, S, D = x.shape
    n_mem = mem.shape[1]
    assert D == D_MODEL and S % SEQ_TILE == 0 and w_in.shape[0] == 1
    ts = SEQ_TILE
    n_tiles = S // ts

    inv_freq = ROPE_BASE ** (-jnp.arange(0, DK, 2, dtype=F32) / DK)
    inv_lane = jnp.tile(inv_freq, 2 * HEADS)
    ang_off = jnp.arange(ts, dtype=F32)[:, None] * inv_lane[None, :]
    ang_base = (jnp.arange(n_tiles, dtype=F32) * ts)[:, None, None] * inv_lane[None, None, :]
    cos_off, sin_off = jnp.cos(ang_off), jnp.sin(ang_off)
    cos_base, sin_base = jnp.cos(ang_base), jnp.sin(ang_base)

    wi = w_in[0]
    o_gq, o_gk, o_gv = 0, HK, 2 * HK
    o_lr = o_gv + HV
    o_gg = o_lr + GATE_RANK
    o_rq = o_gg + HV
    w_in_r = jnp.concatenate(
        [wi[:, o_gq:o_lr], wi[:, o_gg:o_gg + HV], wi[:, o_rq:],
         wi[:, o_lr:o_lr + GATE_RANK], jnp.zeros((D, GATE_RANK_PAD - GATE_RANK), wi.dtype)],
        axis=1).astype(BF16)
    w_gate_p = jnp.concatenate(
        [gla_w_gate[0], jnp.zeros((GATE_RANK_PAD - GATE_RANK, HK), F32)], axis=0).astype(BF16)
    dmat_s, qdec_s, kdect_s, sdec_s = _retention_tables()

    params = pltpu.CompilerParams(dimension_semantics=("arbitrary", "arbitrary"),
                                  vmem_limit_bytes=VMEM_LIMIT_BYTES)

    kt, v = pl.pallas_call(
        _mem_kv_kernel,
        grid=(B,),
        in_specs=[pl.BlockSpec((1, n_mem, D), lambda b: (b, 0, 0)),
                  _const_spec((1, D)), _const_spec((D, 2 * D))],
        out_specs=[pl.BlockSpec((1, XATTN_HEADS, XATTN_DH, n_mem), lambda b: (b, 0, 0, 0)),
                   pl.BlockSpec((1, n_mem, D), lambda b: (b, 0, 0))],
        out_shape=[jax.ShapeDtypeStruct((B, XATTN_HEADS, XATTN_DH, n_mem), BF16),
                   jax.ShapeDtypeStruct((B, n_mem, D), BF16)],
        compiler_params=pltpu.CompilerParams(dimension_semantics=("arbitrary",),
                                             vmem_limit_bytes=VMEM_LIMIT_BYTES),
        name="mem_kv",
    )(mem, norm_mem_g[0][None, :], xa_w_kv[0].astype(BF16))

    n_steps = B * n_tiles + 1

    def next_tile(s):
        p = jnp.minimum(s, n_steps - 2)
        return (p // n_tiles, p % n_tiles, 0)

    def prev_tile(s):
        q = jnp.maximum(s - 1, 0)
        return (q // n_tiles, q % n_tiles, 0)

    def prev_rope(s):
        return (jnp.maximum(s - 1, 0) % n_tiles, 0, 0)

    h1 = pl.pallas_call(
        functools.partial(_mixer_kernel, n_chunks=ts // CHUNK, n_tiles=n_tiles),
        grid=(n_steps,),
        in_specs=[pl.BlockSpec((1, ts, D), next_tile),
                  pl.BlockSpec((1, ts, D), prev_tile),
                  pl.BlockSpec((1, 1, HK), prev_rope), pl.BlockSpec((1, 1, HK), prev_rope),
                  _const_spec((ts, HK)), _const_spec((ts, HK)),
                  _const_spec((1, D)), _const_spec((D, D_PROJ)),
                  _const_spec((GATE_RANK_PAD, HK)), _const_spec((1, HK)),
                  _const_spec((HEADS, DV)), _const_spec((HEADS, DV)),
                  _const_spec((D, D)),
                  _const_spec((HEADS * CHUNK, CHUNK)), _const_spec((HEADS * CHUNK, DV)),
                  _const_spec((HEADS * DK, CHUNK)), _const_spec((HEADS * DK, DV))],
        out_specs=pl.BlockSpec((1, ts, D), prev_tile),
        out_shape=jax.ShapeDtypeStruct((B, S, D), F32),
        scratch_shapes=[pltpu.VMEM((ts, D_PROJ), F32), pltpu.VMEM((ts, D_PROJ), F32),
                        pltpu.VMEM((ts, D), BF16), pltpu.VMEM((ts, D), BF16),
                        pltpu.VMEM((3, ts // CHUNK, HK, CHUNK), BF16),
                        pltpu.VMEM((HEADS * DK, DV), F32), pltpu.VMEM((HEADS * DK, DV), F32)],
        compiler_params=pltpu.CompilerParams(dimension_semantics=("arbitrary",),
                                             vmem_limit_bytes=VMEM_LIMIT_BYTES),
        name="mixer",
    )(x, x, cos_base, sin_base, cos_off, sin_off, norm_mix_g[0][None, :], w_in_r, w_gate_p, gla_b_gate[0][None, :],
      gla_norm_g[0], ret_norm_g[0], w_out[0].astype(BF16), dmat_s, qdec_s, kdect_s, sdec_s)

    out = pl.pallas_call(
        _xattn_mlp_kernel,
        grid=(B, n_tiles),
        in_specs=[pl.BlockSpec((1, ts, D), lambda b, t: (b, t, 0)),
                  pl.BlockSpec((1, XATTN_HEADS, XATTN_DH, n_mem), lambda b, t: (b, 0, 0, 0)),
                  pl.BlockSpec((1, n_mem, D), lambda b, t: (b, 0, 0)),
                  _const_spec((1, D)), _const_spec((D, D)), _const_spec((D, D)),
                  _const_spec((1, D)), _const_spec((D, D_FF)), _const_spec((D_FF, D)),
                  _const_spec((1, D))],
        out_specs=pl.BlockSpec((1, ts, D), lambda b, t: (b, t, 0)),
        out_shape=jax.ShapeDtypeStruct((B, S, D), F32),
        compiler_params=params,
        name="xattn_mlp",
    )(h1, kt, v, norm_x_g[0][None, :], xa_w_q[0].astype(BF16), xa_w_o[0].astype(BF16),
      norm_ffn_g[0][None, :], ffn_w1[0].astype(BF16), ffn_w2[0].astype(BF16), final_norm_g[None, :])
    return out
```

```python
import functools

import numpy as np
import jax
import jax.numpy as jnp
from jax import lax
from jax.experimental import pallas as pl
from jax.experimental.pallas import tpu as pltpu

D_MODEL = 1024
CHUNK = 64
HEADS = 4
DK = 64
DV = 128
GATE_RANK = 16
GATE_RANK_PAD = 128
GATE_TAU = 16.0
RET_DECAY_EXP_LO = 5.0
RET_DECAY_EXP_HI = 12.0
ROPE_BASE = 10000.0
XATTN_HEADS = 4
XATTN_DH = D_MODEL // XATTN_HEADS
D_FF = 4 * D_MODEL
FF_BLOCK = 1024
EPS = 1e-6

HK = HEADS * DK
HV = HEADS * DV
C_GQ, C_GK, C_GV, C_GG = 0, HK, 2 * HK, 2 * HK + HV
C_RQ = C_GG + HV
C_RK = C_RQ + HK
C_RV = C_RK + HK
C_RG = C_RV + HV
C_LR = C_RG + HV
D_PROJ = C_LR + GATE_RANK_PAD

SEQ_TILE = 512
PROJ_PIECE = 256
CUM_BLOCK = 256
OUT_ROWS = 256
VMEM_LIMIT_BYTES = 56 * 1024 * 1024

F32 = jnp.float32
BF16 = jnp.bfloat16


def _dot(a, b):
    return jnp.dot(a, b, preferred_element_type=F32)


def _rms(x, g):
    return x * lax.rsqrt(jnp.mean(x * x, axis=-1, keepdims=True) + EPS) * g


def _swish(g):
    return g / (1.0 + jnp.exp(-g))


def _stack_heads(t):
    lane_head = lax.broadcasted_iota(jnp.int32, t.shape, 1) // DK
    zero = jnp.zeros_like(t)
    return jnp.concatenate([jnp.where(lane_head == h, t, zero) for h in range(HEADS)], axis=0)


def _mem_kv_kernel(mem_ref, g_ref, wkv_ref, kt_ref, v_ref):
    mn = _rms(mem_ref[0], g_ref[...]).astype(BF16)
    kv = _dot(mn, wkv_ref[...])
    for h in range(XATTN_HEADS):
        kt_ref[0, h] = kv[:, h * XATTN_DH:(h + 1) * XATTN_DH].T.astype(BF16)
    v_ref[0] = kv[:, D_MODEL:].astype(BF16)


def _mixer_kernel(xnext_ref, xres_ref, cosb_ref, sinb_ref, coso_ref, sino_ref, g_ref, win_ref, wg_ref,
                  bg_ref, glan_ref,
                  retn_ref, wout_ref, dmat_ref, qdec_ref, kdect_ref, sdec_ref,
                  o_ref, proj_a_ref, proj_b_ref, xn_ref, y_ref, kt_ref, sg_ref, sr_ref, *,
                  n_chunks, n_tiles):
    s = pl.program_id(0)
    prev_tile = jnp.maximum(s - 1, 0)

    @pl.when(s == 0)
    def _():
        proj_b_ref[...] = jnp.zeros_like(proj_b_ref)

    @pl.when(prev_tile % n_tiles == 0)
    def _():
        sg_ref[...] = jnp.zeros_like(sg_ref)
        sr_ref[...] = jnp.zeros_like(sr_ref)

    xn_ref[...] = _rms(xnext_ref[0], g_ref[...]).astype(BF16)

    chunk_pass = functools.partial(
        _chunk_pass, xn_ref=xn_ref, rope_refs=(cosb_ref, sinb_ref, coso_ref, sino_ref),
        win_ref=win_ref, wg_ref=wg_ref,
        bg_ref=bg_ref, glan_ref=glan_ref, retn_ref=retn_ref, dmat_ref=dmat_ref, qdec_ref=qdec_ref,
        kdect_ref=kdect_ref, sdec_ref=sdec_ref, y_ref=y_ref, kt_ref=kt_ref, sg_ref=sg_ref,
        sr_ref=sr_ref,
        xres_ref=xres_ref, wout_ref=wout_ref, o_ref=o_ref, n_chunks=n_chunks)

    @pl.when(s % 2 == 0)
    def _():
        chunk_pass(proj_a_ref, proj_b_ref)

    @pl.when(s % 2 == 1)
    def _():
        chunk_pass(proj_b_ref, proj_a_ref)


def _chunk_pass(cur_ref, prv_ref, *, xn_ref, rope_refs, win_ref, wg_ref, bg_ref, glan_ref,
                retn_ref, dmat_ref, qdec_ref, kdect_ref, sdec_ref, y_ref, kt_ref, sg_ref, sr_ref,
                xres_ref, wout_ref, o_ref, n_chunks):
    ts = n_chunks * CHUNK
    chunk_rows = [slice(c * CHUNK, (c + 1) * CHUNK) for c in range(n_chunks)]
    head_rows = [slice(h * CHUNK, (h + 1) * CHUNK) for h in range(HEADS)]
    head_cols = [slice(h * DV, (h + 1) * DV) for h in range(HEADS)]
    pieces = iter([(i * PROJ_PIECE, min((i + 1) * PROJ_PIECE, D_PROJ))
                   for i in range(-(-D_PROJ // PROJ_PIECE))])

    def emit_proj(n):
        for lo, hi in [p for p in (next(pieces, None) for _ in range(n)) if p is not None]:
            cur_ref[:, lo:hi] = _dot(xn_ref[...], win_ref[:, lo:hi])

    bi = lax.broadcasted_iota(jnp.int32, (CUM_BLOCK, CUM_BLOCK), 0)
    bj = lax.broadcasted_iota(jnp.int32, (CUM_BLOCK, CUM_BLOCK), 1)
    tril_blocks = ((bi // CHUNK == bj // CHUNK) & (bj <= bi)).astype(BF16)
    si = lax.broadcasted_iota(jnp.int32, (HEADS * CHUNK, CHUNK), 0) % CHUNK
    sj = lax.broadcasted_iota(jnp.int32, (HEADS * CHUNK, CHUNK), 1)
    lower = sj <= si
    first_half = (lax.broadcasted_iota(jnp.int32, (ts, HK), 1) % DK) < (DK // 2)
    lane_pad = jnp.zeros((HK, CHUNK), BF16)

    def contributions(kdt, v):
        return jnp.concatenate([_dot(kdt[head_rows[h]], v[:, head_cols[h]]) for h in range(HEADS)],
                               axis=0)

    def scan_states(s_ref, decays, contribs):
        state, starts = s_ref[...], []
        for dec, con in zip(decays, contribs):
            starts.append(state.astype(BF16))
            state = state * dec + con
        s_ref[...] = state
        return starts

    z = _dot(prv_ref[:, C_LR:C_LR + GATE_RANK_PAD].astype(BF16), wg_ref[...]) + bg_ref[...]
    emit_proj(1)
    log_a = (jnp.minimum(z, 0.0) - jnp.log1p(jnp.exp(-jnp.abs(z)))) * (1.0 / GATE_TAU)
    la_hi = log_a.astype(BF16)
    la_lo = (log_a - la_hi.astype(F32)).astype(BF16)
    la_split = jnp.concatenate([la_hi, la_lo], axis=1)
    bcum = []
    for b in range(ts // CUM_BLOCK):
        r = _dot(tril_blocks, la_split[b * CUM_BLOCK:(b + 1) * CUM_BLOCK])
        bcum.append(r[:, :HK] + r[:, HK:])
    bcum = jnp.concatenate(bcum, axis=0)
    emit_proj(1)
    q = prv_ref[:, C_GQ:C_GQ + HK] * (DK ** -0.5)
    qeb = (q * jnp.exp(bcum)).astype(BF16)
    qebi = (q * jnp.exp(-bcum)).astype(BF16)

    g_decay, g_contrib = [], []
    for c, rows in enumerate(chunk_rows):
        kt = prv_ref[rows, C_GK:C_GK + HK].T
        bct = bcum[rows].T
        blt = bct[:, CHUNK - 1:CHUNK]
        kt_ref[0, c] = (kt * jnp.exp(bct)).astype(BF16)
        kt_ref[1, c] = (kt * jnp.exp(-bct)).astype(BF16)
        kdt = (kt * jnp.exp(blt - bct)).astype(BF16)
        g_decay.append(jnp.exp(blt))
        g_contrib.append(contributions(kdt, prv_ref[rows, C_GV:C_GV + HV].astype(BF16)))
        if c % 4 == 3:
            emit_proj(1)

    cosb, sinb, coso, sino = [r[...] for r in rope_refs]
    cosb, sinb = cosb[0], sinb[0]
    cs = cosb * coso - sinb * sino
    sn = sinb * coso + cosb * sino
    sn = jnp.where(first_half, -sn, sn)

    def rope(t):
        partner = jnp.where(first_half, pltpu.roll(t, HK - DK // 2, 1), pltpu.roll(t, DK // 2, 1))
        return t * cs + partner * sn

    qrb = rope(prv_ref[:, C_RQ:C_RQ + HK]).astype(BF16)
    kr = rope(prv_ref[:, C_RK:C_RK + HK]) * (DK ** -0.5)
    r_contrib = []
    for c, rows in enumerate(chunk_rows):
        kt = kr[rows].T
        kt_ref[2, c] = kt.astype(BF16)
        kdt = (kt * kdect_ref[...]).astype(BF16)
        r_contrib.append(contributions(kdt, prv_ref[rows, C_RV:C_RV + HV].astype(BF16)))
        if c % 4 == 3:
            emit_proj(1)

    g_start = scan_states(sg_ref, g_decay, g_contrib)
    r_start = scan_states(sr_ref, [sdec_ref[...]] * n_chunks, r_contrib)

    for c, rows in enumerate(chunk_rows):
        r = _dot(_stack_heads(qeb[rows]), jnp.concatenate([g_start[c], kt_ref[1, c], lane_pad], axis=1))
        a_up = _dot(_stack_heads(qebi[rows]), kt_ref[0, c])
        attn = jnp.where(lower, r[:, DV:DV + CHUNK], a_up).astype(BF16)
        v = prv_ref[rows, C_GV:C_GV + HV].astype(BF16)
        for h in range(HEADS):
            o = _dot(attn[head_rows[h]], v[:, head_cols[h]]) + r[head_rows[h], :DV]
            gate = prv_ref[rows, C_GG + h * DV:C_GG + (h + 1) * DV]
            yn = o * lax.rsqrt(jnp.mean(o * o, axis=-1, keepdims=True) + EPS) * glan_ref[h:h + 1, :]
            y_ref[rows, h * DV:(h + 1) * DV] = (yn * _swish(gate)).astype(BF16)
        emit_proj(1)

        r = _dot(_stack_heads(qrb[rows]), jnp.concatenate([r_start[c], kt_ref[2, c], lane_pad], axis=1))
        inter = r[:, :DV] * qdec_ref[...]
        scores = (r[:, DV:DV + CHUNK] * dmat_ref[...]).astype(BF16)
        v = prv_ref[rows, C_RV:C_RV + HV].astype(BF16)
        for h in range(HEADS):
            o = _dot(scores[head_rows[h]], v[:, head_cols[h]]) + inter[head_rows[h]]
            gate = prv_ref[rows, C_RG + h * DV:C_RG + (h + 1) * DV]
            d = o - jnp.mean(o, axis=-1, keepdims=True)
            yn = d * lax.rsqrt(jnp.mean(d * d, axis=-1, keepdims=True) + EPS) * retn_ref[h:h + 1, :]
            y_ref[rows, HV + h * DV:HV + (h + 1) * DV] = (yn * _swish(gate)).astype(BF16)
        if (c + 1) % (OUT_ROWS // CHUNK) == 0:
            done = slice((c + 1) * CHUNK - OUT_ROWS, (c + 1) * CHUNK)
            o_ref[0, done, :] = xres_ref[0, done, :] + _dot(y_ref[done, :], wout_ref[...])
    assert next(pieces, None) is None


def _xattn_mlp_kernel(h_ref, kt_ref, v_ref, gx_ref, wq_ref, wo_ref, gf_ref, w1_ref, w2_ref, gfin_ref,
                      o_ref):
    h = h_ref[0]
    hn = _rms(h, gx_ref[...]).astype(BF16)
    q = (_dot(hn, wq_ref[...]) * (XATTN_DH ** -0.5)).astype(BF16)
    heads = []
    for a in range(XATTN_HEADS):
        cols = slice(a * XATTN_DH, (a + 1) * XATTN_DH)
        s = _dot(q[:, cols], kt_ref[0, a])
        e = jnp.exp(s - jnp.max(s, axis=-1, keepdims=True))
        pv = _dot(e.astype(BF16), v_ref[0, :, cols])
        heads.append((pv / jnp.sum(e, axis=-1, keepdims=True)).astype(BF16))
    h = h + _dot(jnp.concatenate(heads, axis=1), wo_ref[...])

    hn = _rms(h, gf_ref[...]).astype(BF16)
    for c in range(D_FF // FF_BLOCK):
        cols = slice(c * FF_BLOCK, (c + 1) * FF_BLOCK)
        u = jnp.maximum(_dot(hn, w1_ref[:, cols]), 0.0)
        h = h + _dot((u * u).astype(BF16), w2_ref[cols, :])
    o_ref[0] = _rms(h, gfin_ref[...])


def _const_spec(shape):
    n = len(shape)
    return pl.BlockSpec(shape, lambda *_: (0,) * n, pipeline_mode=pl.Buffered(1))


def _retention_tables():
    gamma = 1.0 - np.exp2(-np.linspace(RET_DECAY_EXP_LO, RET_DECAY_EXP_HI, HEADS, dtype=np.float32))
    log_g = np.log(gamma.astype(np.float32)).astype(np.float32)
    pos = np.arange(CHUNK, dtype=np.float32)
    dmat = np.exp(np.abs(pos[:, None] - pos[None, :])[None] * log_g[:, None, None])
    k_dec = np.exp((CHUNK - 1.0 - pos)[None, :] * log_g[:, None])
    q_dec = np.exp((pos + 1.0)[None, :] * log_g[:, None])
    s_dec = np.exp(CHUNK * log_g)
    dmat_s = dmat.reshape(HEADS * CHUNK, CHUNK)
    qdec_s = np.broadcast_to(q_dec.reshape(HEADS * CHUNK, 1), (HEADS * CHUNK, DV))
    kdect_s = np.repeat(k_dec, DK, axis=0)
    sdec_s = np.broadcast_to(np.repeat(s_dec, DK)[:, None], (HEADS * DK, DV))
    as32 = lambda a: jnp.asarray(np.ascontiguousarray(a), dtype=F32)
    return as32(dmat_s), as32(qdec_s), as32(kdect_s), as32(sdec_s)


def kernel(x, mem, norm_mix_g, w_in, gla_w_gate, gla_b_gate, gla_norm_g, ret_norm_g, w_out, norm_x_g,
           norm_mem_g, xa_w_q, xa_w_kv, xa_w_o, norm_ffn_g, ffn_w1, ffn_w2, final_norm_g):
    B, S, D = x.shape
    n_mem = mem.shape[1]
    assert D == D_MODEL and S % SEQ_TILE == 0 and w_in.shape[0] == 1
    ts = SEQ_TILE
    n_tiles = S // ts

    inv_freq = ROPE_BASE ** (-jnp.arange(0, DK, 2, dtype=F32) / DK)
    inv_lane = jnp.tile(inv_freq, 2 * HEADS)
    ang_off = jnp.arange(ts, dtype=F32)[:, None] * inv_lane[None, :]
    ang_base = (jnp.arange(n_tiles, dtype=F32) * ts)[:, None, None] * inv_lane[None, None, :]
    cos_off, sin_off = jnp.cos(ang_off), jnp.sin(ang_off)
    cos_base, sin_base = jnp.cos(ang_base), jnp.sin(ang_base)

    wi = w_in[0]
    o_gq, o_gk, o_gv = 0, HK, 2 * HK
    o_lr = o_gv + HV
    o_gg = o_lr + GATE_RANK
    o_rq = o_gg + HV
    w_in_r = jnp.concatenate(
        [wi[:, o_gq:o_lr], wi[:, o_gg:o_gg + HV], wi[:, o_rq:],
         wi[:, o_lr:o_lr + GATE_RANK], jnp.zeros((D, GATE_RANK_PAD - GATE_RANK), wi.dtype)],
        axis=1).astype(BF16)
    w_gate_p = jnp.concatenate(
        [gla_w_gate[0], jnp.zeros((GATE_RANK_PAD - GATE_RANK, HK), F32)], axis=0).astype(BF16)
    dmat_s, qdec_s, kdect_s, sdec_s = _retention_tables()

    params = pltpu.CompilerParams(dimension_semantics=("arbitrary", "arbitrary"),
                                  vmem_limit_bytes=VMEM_LIMIT_BYTES)

    kt, v = pl.pallas_call(
        _mem_kv_kernel,
        grid=(B,),
        in_specs=[pl.BlockSpec((1, n_mem, D), lambda b: (b, 0, 0)),
                  _const_spec((1, D)), _const_spec((D, 2 * D))],
        out_specs=[pl.BlockSpec((1, XATTN_HEADS, XATTN_DH, n_mem), lambda b: (b, 0, 0, 0)),
                   pl.BlockSpec((1, n_mem, D), lambda b: (b, 0, 0))],
        out_shape=[jax.ShapeDtypeStruct((B, XATTN_HEADS, XATTN_DH, n_mem), BF16),
                   jax.ShapeDtypeStruct((B, n_mem, D), BF16)],
        compiler_params=pltpu.CompilerParams(dimension_semantics=("arbitrary",),
                                             vmem_limit_bytes=VMEM_LIMIT_BYTES),
        name="mem_kv",
    )(mem, norm_mem_g[0][None, :], xa_w_kv[0].astype(BF16))

    n_steps = B * n_tiles + 1

    def next_tile(s):
        p = jnp.minimum(s, n_steps - 2)
        return (p // n_tiles, p % n_tiles, 0)

    def prev_tile(s):
        q = jnp.maximum(s - 1, 0)
        return (q // n_tiles, q % n_tiles, 0)

    def prev_rope(s):
        return (jnp.maximum(s - 1, 0) % n_tiles, 0, 0)

    h1 = pl.pallas_call(
        functools.partial(_mixer_kernel, n_chunks=ts // CHUNK, n_tiles=n_tiles),
        grid=(n_steps,),
        in_specs=[pl.BlockSpec((1, ts, D), next_tile),
                  pl.BlockSpec((1, ts, D), prev_tile),
                  pl.BlockSpec((1, 1, HK), prev_rope), pl.BlockSpec((1, 1, HK), prev_rope),
                  _const_spec((ts, HK)), _const_spec((ts, HK)),
                  _const_spec((1, D)), _const_spec((D, D_PROJ)),
                  _const_spec((GATE_RANK_PAD, HK)), _const_spec((1, HK)),
                  _const_spec((HEADS, DV)), _const_spec((HEADS, DV)),
                  _const_spec((D, D)),
                  _const_spec((HEADS * CHUNK, CHUNK)), _const_spec((HEADS * CHUNK, DV)),
                  _const_spec((HEADS * DK, CHUNK)), _const_spec((HEADS * DK, DV))],
        out_specs=pl.BlockSpec((1, ts, D), prev_tile),
        out_shape=jax.ShapeDtypeStruct((B, S, D), F32),
        scratch_shapes=[pltpu.VMEM((ts, D_PROJ), F32), pltpu.VMEM((ts, D_PROJ), F32),
                        pltpu.VMEM((ts, D), BF16), pltpu.VMEM((ts, D), BF16),
                        pltpu.VMEM((3, ts // CHUNK, HK, CHUNK), BF16),
                        pltpu.VMEM((HEADS * DK, DV), F32), pltpu.VMEM((HEADS * DK, DV), F32)],
        compiler_params=pltpu.CompilerParams(dimension_semantics=("arbitrary",),
                                             vmem_limit_bytes=VMEM_LIMIT_BYTES),
        name="mixer",
    )(x, x, cos_base, sin_base, cos_off, sin_off, norm_mix_g[0][None, :], w_in_r, w_gate_p, gla_b_gate[0][None, :],
      gla_norm_g[0], ret_norm_g[0], w_out[0].astype(BF16), dmat_s, qdec_s, kdect_s, sdec_s)

    out = pl.pallas_call(
        _xattn_mlp_kernel,
        grid=(B, n_tiles),
        in_specs=[pl.BlockSpec((1, ts, D), lambda b, t: (b, t, 0)),
                  pl.BlockSpec((1, XATTN_HEADS, XATTN_DH, n_mem), lambda b, t: (b, 0, 0, 0)),
                  pl.BlockSpec((1, n_mem, D), lambda b, t: (b, 0, 0)),
                  _const_spec((1, D)), _const_spec((D, D)), _const_spec((D, D)),
                  _const_spec((1, D)), _const_spec((D, D_FF)), _const_spec((D_FF, D)),
                  _const_spec((1, D))],
        out_specs=pl.BlockSpec((1, ts, D), lambda b, t: (b, t, 0)),
        out_shape=jax.ShapeDtypeStruct((B, S, D), F32),
        compiler_params=params,
        name="xattn_mlp",
    )(h1, kt, v, norm_x_g[0][None, :], xa_w_q[0].astype(BF16), xa_w_o[0].astype(BF16),
      norm_ffn_g[0][None, :], ffn_w1[0].astype(BF16), ffn_w2[0].astype(BF16), final_norm_g[None, :])
    return out
```

```python
import functools

import numpy as np
import jax
import jax.numpy as jnp
from jax import lax
from jax.experimental import pallas as pl
from jax.experimental.pallas import tpu as pltpu

D_MODEL = 1024
CHUNK = 64
HEADS = 4
DK = 64
DV = 128
GATE_RANK = 16
GATE_RANK_PAD = 128
GATE_TAU = 16.0
RET_DECAY_EXP_LO = 5.0
RET_DECAY_EXP_HI = 12.0
ROPE_BASE = 10000.0
XATTN_HEADS = 4
XATTN_DH = D_MODEL // XATTN_HEADS
D_FF = 4 * D_MODEL
FF_BLOCK = 1024
XATTN_ROWS = 256
EPS = 1e-6

HK = HEADS * DK
HV = HEADS * DV
C_GQ, C_GK, C_GV, C_GG = 0, HK, 2 * HK, 2 * HK + HV
C_RQ = C_GG + HV
C_RK = C_RQ + HK
C_RV = C_RK + HK
C_RG = C_RV + HV
C_LR = C_RG + HV
D_PROJ = C_LR + GATE_RANK_PAD

SEQ_TILE = 512
PROJ_PIECE = 256
CUM_BLOCK = 256
OUT_ROWS = 256
VMEM_LIMIT_BYTES = 56 * 1024 * 1024

F32 = jnp.float32
BF16 = jnp.bfloat16


def _dot(a, b):
    return jnp.dot(a, b, preferred_element_type=F32)


def _rms(x, g):
    return x * lax.rsqrt(jnp.mean(x * x, axis=-1, keepdims=True) + EPS) * g


def _swish(g):
    return g / (1.0 + jnp.exp(-g))


def _stack_heads(t):
    lane_head = lax.broadcasted_iota(jnp.int32, t.shape, 1) // DK
    zero = jnp.zeros_like(t)
    return jnp.concatenate([jnp.where(lane_head == h, t, zero) for h in range(HEADS)], axis=0)


def _mem_kv_kernel(mem_ref, g_ref, wkv_ref, kt_ref, v_ref):
    mn = _rms(mem_ref[0], g_ref[...]).astype(BF16)
    kv = _dot(mn, wkv_ref[...])
    for h in range(XATTN_HEADS):
        kt_ref[0, h] = kv[:, h * XATTN_DH:(h + 1) * XATTN_DH].T.astype(BF16)
    v_ref[0] = kv[:, D_MODEL:].astype(BF16)


def _mixer_kernel(xnext_ref, xres_ref, cosb_ref, sinb_ref, coso_ref, sino_ref, g_ref, win_ref, wg_ref,
                  bg_ref, glan_ref,
                  retn_ref, wout_ref, dmat_ref, qdec_ref, kdect_ref, sdec_ref,
                  o_ref, proj_a_ref, proj_b_ref, xn_ref, y_ref, kt_ref, sg_ref, sr_ref, *,
                  n_chunks, n_tiles):
    s = pl.program_id(0)
    prev_tile = jnp.maximum(s - 1, 0)

    @pl.when(s == 0)
    def _():
        proj_b_ref[...] = jnp.zeros_like(proj_b_ref)

    @pl.when(prev_tile % n_tiles == 0)
    def _():
        sg_ref[...] = jnp.zeros_like(sg_ref)
        sr_ref[...] = jnp.zeros_like(sr_ref)

    xn_ref[...] = _rms(xnext_ref[0], g_ref[...]).astype(BF16)

    chunk_pass = functools.partial(
        _chunk_pass, xn_ref=xn_ref, rope_refs=(cosb_ref, sinb_ref, coso_ref, sino_ref),
        win_ref=win_ref, wg_ref=wg_ref,
        bg_ref=bg_ref, glan_ref=glan_ref, retn_ref=retn_ref, dmat_ref=dmat_ref, qdec_ref=qdec_ref,
        kdect_ref=kdect_ref, sdec_ref=sdec_ref, y_ref=y_ref, kt_ref=kt_ref, sg_ref=sg_ref,
        sr_ref=sr_ref,
        xres_ref=xres_ref, wout_ref=wout_ref, o_ref=o_ref, n_chunks=n_chunks)

    @pl.when(s % 2 == 0)
    def _():
        chunk_pass(proj_a_ref, proj_b_ref)

    @pl.when(s % 2 == 1)
    def _():
        chunk_pass(proj_b_ref, proj_a_ref)


def _chunk_pass(cur_ref, prv_ref, *, xn_ref, rope_refs, win_ref, wg_ref, bg_ref, glan_ref,
                retn_ref, dmat_ref, qdec_ref, kdect_ref, sdec_ref, y_ref, kt_ref, sg_ref, sr_ref,
                xres_ref, wout_ref, o_ref, n_chunks):
    ts = n_chunks * CHUNK
    chunk_rows = [slice(c * CHUNK, (c + 1) * CHUNK) for c in range(n_chunks)]
    head_rows = [slice(h * CHUNK, (h + 1) * CHUNK) for h in range(HEADS)]
    head_cols = [slice(h * DV, (h + 1) * DV) for h in range(HEADS)]
    pieces = [(i * PROJ_PIECE, min((i + 1) * PROJ_PIECE, D_PROJ))
              for i in range(-(-D_PROJ // PROJ_PIECE))]

    def emit_proj(n):
        for _ in range(min(n, len(pieces))):
            lo, hi = pieces.pop(0)
            cur_ref[:, lo:hi] = _dot(xn_ref[...], win_ref[:, lo:hi])

    bi = lax.broadcasted_iota(jnp.int32, (CUM_BLOCK, CUM_BLOCK), 0)
    bj = lax.broadcasted_iota(jnp.int32, (CUM_BLOCK, CUM_BLOCK), 1)
    tril_blocks = ((bi // CHUNK == bj // CHUNK) & (bj <= bi)).astype(BF16)
    si = lax.broadcasted_iota(jnp.int32, (HEADS * CHUNK, CHUNK), 0) % CHUNK
    sj = lax.broadcasted_iota(jnp.int32, (HEADS * CHUNK, CHUNK), 1)
    lower = sj <= si
    first_half = (lax.broadcasted_iota(jnp.int32, (ts, HK), 1) % DK) < (DK // 2)
    lane_pad = jnp.zeros((HK, CHUNK), BF16)

    def contributions(kdt, v):
        return jnp.concatenate([_dot(kdt[head_rows[h]], v[:, head_cols[h]]) for h in range(HEADS)],
                               axis=0)

    def scan_states(s_ref, decays, contribs):
        state, starts = s_ref[...], []
        for dec, con in zip(decays, contribs):
            starts.append(state.astype(BF16))
            state = state * dec + con
        s_ref[...] = state
        return starts

    z = _dot(prv_ref[:, C_LR:C_LR + GATE_RANK_PAD].astype(BF16), wg_ref[...]) + bg_ref[...]
    emit_proj(1)
    log_a = (jnp.minimum(z, 0.0) - jnp.log1p(jnp.exp(-jnp.abs(z)))) * (1.0 / GATE_TAU)
    la_hi = log_a.astype(BF16)
    la_lo = (log_a - la_hi.astype(F32)).astype(BF16)
    la_split = jnp.concatenate([la_hi, la_lo], axis=1)
    bcum = []
    for b in range(ts // CUM_BLOCK):
        r = _dot(tril_blocks, la_split[b * CUM_BLOCK:(b + 1) * CUM_BLOCK])
        bcum.append(r[:, :HK] + r[:, HK:])
    bcum = jnp.concatenate(bcum, axis=0)
    emit_proj(1)
    q = prv_ref[:, C_GQ:C_GQ + HK] * (DK ** -0.5)
    qeb = (q * jnp.exp(bcum)).astype(BF16)
    qebi = (q * jnp.exp(-bcum)).astype(BF16)

    g_decay, g_contrib = [], []
    for c, rows in enumerate(chunk_rows):
        kt = prv_ref[rows, C_GK:C_GK + HK].T
        bct = bcum[rows].T
        blt = bct[:, CHUNK - 1:CHUNK]
        kt_ref[0, c] = (kt * jnp.exp(bct)).astype(BF16)
        kt_ref[1, c] = (kt * jnp.exp(-bct)).astype(BF16)
        kdt = (kt * jnp.exp(blt - bct)).astype(BF16)
        g_decay.append(jnp.exp(blt))
        g_contrib.append(contributions(kdt, prv_ref[rows, C_GV:C_GV + HV].astype(BF16)))
        if c % 2 == 1:
            emit_proj(1)

    cosb, sinb, coso, sino = [r[...] for r in rope_refs]
    cosb, sinb = cosb[0], sinb[0]
    cs = cosb * coso - sinb * sino
    sn = sinb * coso + cosb * sino
    sn = jnp.where(first_half, -sn, sn)

    def rope(t):
        partner = jnp.where(first_half, pltpu.roll(t, HK - DK // 2, 1), pltpu.roll(t, DK // 2, 1))
        return t * cs + partner * sn

    qrb = rope(prv_ref[:, C_RQ:C_RQ + HK]).astype(BF16)
    kr = rope(prv_ref[:, C_RK:C_RK + HK]) * (DK ** -0.5)
    r_contrib = []
    for c, rows in enumerate(chunk_rows):
        kt = kr[rows].T
        kt_ref[2, c] = kt.astype(BF16)
        kdt = (kt * kdect_ref[...]).astype(BF16)
        r_contrib.append(contributions(kdt, prv_ref[rows, C_RV:C_RV + HV].astype(BF16)))
        if c % 2 == 1:
            emit_proj(1)

    g_start = scan_states(sg_ref, g_decay, g_contrib)
    r_start = scan_states(sr_ref, [sdec_ref[...]] * n_chunks, r_contrib)

    def gla_scores(c):
        rows = chunk_rows[c]
        r = _dot(_stack_heads(qeb[rows]), jnp.concatenate([g_start[c], kt_ref[1, c], lane_pad], axis=1))
        a_up = _dot(_stack_heads(qebi[rows]), kt_ref[0, c])
        return r, a_up

    def ret_scores(c):
        return _dot(_stack_heads(qrb[chunk_rows[c]]),
                    jnp.concatenate([r_start[c], kt_ref[2, c], lane_pad], axis=1))

    def gla_values(c, r, a_up):
        rows = chunk_rows[c]
        attn = jnp.where(lower, r[:, DV:DV + CHUNK], a_up).astype(BF16)
        v = prv_ref[rows, C_GV:C_GV + HV].astype(BF16)
        for h in range(HEADS):
            o = _dot(attn[head_rows[h]], v[:, head_cols[h]]) + r[head_rows[h], :DV]
            gate = prv_ref[rows, C_GG + h * DV:C_GG + (h + 1) * DV]
            yn = o * lax.rsqrt(jnp.mean(o * o, axis=-1, keepdims=True) + EPS) * glan_ref[h:h + 1, :]
            y_ref[rows, h * DV:(h + 1) * DV] = (yn * _swish(gate)).astype(BF16)

    def ret_values(c, r):
        rows = chunk_rows[c]
        inter = r[:, :DV] * qdec_ref[...]
        scores = (r[:, DV:DV + CHUNK] * dmat_ref[...]).astype(BF16)
        v = prv_ref[rows, C_RV:C_RV + HV].astype(BF16)
        for h in range(HEADS):
            o = _dot(scores[head_rows[h]], v[:, head_cols[h]]) + inter[head_rows[h]]
            gate = prv_ref[rows, C_RG + h * DV:C_RG + (h + 1) * DV]
            d = o - jnp.mean(o, axis=-1, keepdims=True)
            yn = d * lax.rsqrt(jnp.mean(d * d, axis=-1, keepdims=True) + EPS) * retn_ref[h:h + 1, :]
            y_ref[rows, HV + h * DV:HV + (h + 1) * DV] = (yn * _swish(gate)).astype(BF16)

    def out_proj(done):
        o_ref[0, done, :] = xres_ref[0, done, :] + _dot(y_ref[done, :], wout_ref[...])

    finished = None
    ahead = (gla_scores(0), ret_scores(0))
    for c in range(n_chunks):
        (g_r, g_up), r_r = ahead
        if c + 1 < n_chunks:
            ahead = (gla_scores(c + 1), ret_scores(c + 1))
        if finished is not None:
            out_proj(finished)
            finished = None
        else:
            emit_proj(1)
        gla_values(c, g_r, g_up)
        ret_values(c, r_r)
        if (c + 1) % (OUT_ROWS // CHUNK) == 0:
            finished = slice((c + 1) * CHUNK - OUT_ROWS, (c + 1) * CHUNK)
    emit_proj(len(pieces))
    out_proj(finished)


def _xattn_mlp_kernel(h_ref, kt_ref, v_ref, gx_ref, wq_ref, wo_ref, gf_ref, w1_ref, w2_ref, gfin_ref,
                      o_ref):
    blocks = [slice(r, r + XATTN_ROWS) for r in range(0, h_ref.shape[1], XATTN_ROWS)]
    h = [h_ref[0, rows, :] for rows in blocks]
    q = [(_dot(_rms(hb, gx_ref[...]).astype(BF16), wq_ref[...]) * (XATTN_DH ** -0.5)).astype(BF16)
         for hb in h]
    att = []
    head_cols = [slice(a * XATTN_DH, (a + 1) * XATTN_DH) for a in range(XATTN_HEADS)]
    for qb in q:
        scores = [_dot(qb[:, cols], kt_ref[0, a]) for a, cols in enumerate(head_cols)]
        heads = []
        for s, cols in zip(scores, head_cols):
            e = jnp.exp(s - jnp.max(s, axis=-1, keepdims=True))
            pv = _dot(e.astype(BF16), v_ref[0, :, cols])
            heads.append((pv / jnp.sum(e, axis=-1, keepdims=True)).astype(BF16))
        att.append(jnp.concatenate(heads, axis=1))
    h = [hb + _dot(ab, wo_ref[...]) for hb, ab in zip(h, att)]

    hn = [_rms(hb, gf_ref[...]).astype(BF16) for hb in h]
    for c in range(D_FF // FF_BLOCK):
        cols = slice(c * FF_BLOCK, (c + 1) * FF_BLOCK)
        u = [jnp.maximum(_dot(hb, w1_ref[:, cols]), 0.0) for hb in hn]
        h = [hb + _dot((ub * ub).astype(BF16), w2_ref[cols, :]) for hb, ub in zip(h, u)]
    for rows, hb in zip(blocks, h):
        o_ref[0, rows, :] = _rms(hb, gfin_ref[...])


def _const_spec(shape):
    n = len(shape)
    return pl.BlockSpec(shape, lambda *_: (0,) * n, pipeline_mode=pl.Buffered(1))


def _retention_tables():
    gamma = 1.0 - np.exp2(-np.linspace(RET_DECAY_EXP_LO, RET_DECAY_EXP_HI, HEADS, dtype=np.float32))
    log_g = np.log(gamma.astype(np.float32)).astype(np.float32)
    pos = np.arange(CHUNK, dtype=np.float32)
    dmat = np.exp(np.abs(pos[:, None] - pos[None, :])[None] * log_g[:, None, None])
    k_dec = np.exp((CHUNK - 1.0 - pos)[None, :] * log_g[:, None])
    q_dec = np.exp((pos + 1.0)[None, :] * log_g[:, None])
    s_dec = np.exp(CHUNK * log_g)
    dmat_s = dmat.reshape(HEADS * CHUNK, CHUNK)
    qdec_s = np.broadcast_to(q_dec.reshape(HEADS * CHUNK, 1), (HEADS * CHUNK, DV))
    kdect_s = np.repeat(k_dec, DK, axis=0)
    sdec_s = np.broadcast_to(np.repeat(s_dec, DK)[:, None], (HEADS * DK, DV))
    as32 = lambda a: jnp.asarray(np.ascontiguousarray(a), dtype=F32)
    return as32(dmat_s), as32(qdec_s), as32(kdect_s), as32(sdec_s)


def kernel(x, mem, norm_mix_g, w_in, gla_w_gate, gla_b_gate, gla_norm_g, ret_norm_g, w_out, norm_x_g,
           norm_mem_g, xa_w_q, xa_w_kv, xa_w_o, norm_ffn_g, ffn_w1, ffn_w2, final_norm_g):
    B, S, D = x.shape
    n_mem = mem.shape[1]
    assert D == D_MODEL and S % SEQ_TILE == 0 and w_in.shape[0] == 1
    ts = SEQ_TILE
    n_tiles = S // ts

    inv_freq = ROPE_BASE ** (-jnp.arange(0, DK, 2, dtype=F32) / DK)
    inv_lane = jnp.tile(inv_freq, 2 * HEADS)
    ang_off = jnp.arange(ts, dtype=F32)[:, None] * inv_lane[None, :]
    ang_base = (jnp.arange(n_tiles, dtype=F32) * ts)[:, None, None] * inv_lane[None, None, :]
    cos_off, sin_off = jnp.cos(ang_off), jnp.sin(ang_off)
    cos_base, sin_base = jnp.cos(ang_base), jnp.sin(ang_base)

    wi = w_in[0]
    o_gq, o_gk, o_gv = 0, HK, 2 * HK
    o_lr = o_gv + HV
    o_gg = o_lr + GATE_RANK
    o_rq = o_gg + HV
    w_in_r = jnp.concatenate(
        [wi[:, o_gq:o_lr], wi[:, o_gg:o_gg + HV], wi[:, o_rq:],
         wi[:, o_lr:o_lr + GATE_RANK], jnp.zeros((D, GATE_RANK_PAD - GATE_RANK), wi.dtype)],
        axis=1).astype(BF16)
    w_gate_p = jnp.concatenate(
        [gla_w_gate[0], jnp.zeros((GATE_RANK_PAD - GATE_RANK, HK), F32)], axis=0).astype(BF16)
    dmat_s, qdec_s, kdect_s, sdec_s = _retention_tables()

    params = pltpu.CompilerParams(dimension_semantics=("arbitrary", "arbitrary"),
                                  vmem_limit_bytes=VMEM_LIMIT_BYTES)

    kt, v = pl.pallas_call(
        _mem_kv_kernel,
        grid=(B,),
        in_specs=[pl.BlockSpec((1, n_mem, D), lambda b: (b, 0, 0)),
                  _const_spec((1, D)), _const_spec((D, 2 * D))],
        out_specs=[pl.BlockSpec((1, XATTN_HEADS, XATTN_DH, n_mem), lambda b: (b, 0, 0, 0)),
                   pl.BlockSpec((1, n_mem, D), lambda b: (b, 0, 0))],
        out_shape=[jax.ShapeDtypeStruct((B, XATTN_HEADS, XATTN_DH, n_mem), BF16),
                   jax.ShapeDtypeStruct((B, n_mem, D), BF16)],
        compiler_params=pltpu.CompilerParams(dimension_semantics=("arbitrary",),
                                             vmem_limit_bytes=VMEM_LIMIT_BYTES),
        name="mem_kv",
    )(mem, norm_mem_g[0][None, :], xa_w_kv[0].astype(BF16))

    n_steps = B * n_tiles + 1

    def next_tile(s):
        p = jnp.minimum(s, n_steps - 2)
        return (p // n_tiles, p % n_tiles, 0)

    def prev_tile(s):
        q = jnp.maximum(s - 1, 0)
        return (q // n_tiles, q % n_tiles, 0)

    def prev_rope(s):
        return (jnp.maximum(s - 1, 0) % n_tiles, 0, 0)

    h1 = pl.pallas_call(
        functools.partial(_mixer_kernel, n_chunks=ts // CHUNK, n_tiles=n_tiles),
        grid=(n_steps,),
        in_specs=[pl.BlockSpec((1, ts, D), next_tile),
                  pl.BlockSpec((1, ts, D), prev_tile),
                  pl.BlockSpec((1, 1, HK), prev_rope), pl.BlockSpec((1, 1, HK), prev_rope),
                  _const_spec((ts, HK)), _const_spec((ts, HK)),
                  _const_spec((1, D)), _const_spec((D, D_PROJ)),
                  _const_spec((GATE_RANK_PAD, HK)), _const_spec((1, HK)),
                  _const_spec((HEADS, DV)), _const_spec((HEADS, DV)),
                  _const_spec((D, D)),
                  _const_spec((HEADS * CHUNK, CHUNK)), _const_spec((HEADS * CHUNK, DV)),
                  _const_spec((HEADS * DK, CHUNK)), _const_spec((HEADS * DK, DV))],
        out_specs=pl.BlockSpec((1, ts, D), prev_tile),
        out_shape=jax.ShapeDtypeStruct((B, S, D), F32),
        scratch_shapes=[pltpu.VMEM((ts, D_PROJ), F32), pltpu.VMEM((ts, D_PROJ), F32),
                        pltpu.VMEM((ts, D), BF16), pltpu.VMEM((ts, D), BF16),
                        pltpu.VMEM((3, ts // CHUNK, HK, CHUNK), BF16),
                        pltpu.VMEM((HEADS * DK, DV), F32), pltpu.VMEM((HEADS * DK, DV), F32)],
        compiler_params=pltpu.CompilerParams(dimension_semantics=("arbitrary",),
                                             vmem_limit_bytes=VMEM_LIMIT_BYTES),
        name="mixer",
    )(x, x, cos_base, sin_base, cos_off, sin_off, norm_mix_g[0][None, :], w_in_r, w_gate_p, gla_b_gate[0][None, :],
      gla_norm_g[0], ret_norm_g[0], w_out[0].astype(BF16), dmat_s, qdec_s, kdect_s, sdec_s)

    out = pl.pallas_call(
        _xattn_mlp_kernel,
        grid=(B, n_tiles),
        in_specs=[pl.BlockSpec((1, ts, D), lambda b, t: (b, t, 0)),
                  pl.BlockSpec((1, XATTN_HEADS, XATTN_DH, n_mem), lambda b, t: (b, 0, 0, 0)),
                  pl.BlockSpec((1, n_mem, D), lambda b, t: (b, 0, 0)),
                  _const_spec((1, D)), _const_spec((D, D)), _const_spec((D, D)),
                  _const_spec((1, D)), _const_spec((D, D_FF)), _const_spec((D_FF, D)),
                  _const_spec((1, D))],
        out_specs=pl.BlockSpec((1, ts, D), lambda b, t: (b, t, 0)),
        out_shape=jax.ShapeDtypeStruct((B, S, D), F32),
        compiler_params=params,
        name="xattn_mlp",
    )(h1, kt, v, norm_x_g[0][None, :], xa_w_q[0].astype(BF16), xa_w_o[0].astype(BF16),
      norm_ffn_g[0][None, :], ffn_w1[0].astype(BF16), ffn_w2[0].astype(BF16), final_norm_g[None, :])
    return out
```

```python
import functools

import numpy as np
import jax
import jax.numpy as jnp
from jax import lax
from jax.experimental import pallas as pl
from jax.experimental.pallas import tpu as pltpu

D_MODEL = 1024
CHUNK = 64
HEADS = 4
DK = 64
DV = 128
GATE_RANK = 16
GATE_RANK_PAD = 128
GATE_TAU = 16.0
RET_DECAY_EXP_LO = 5.0
RET_DECAY_EXP_HI = 12.0
ROPE_BASE = 10000.0
XATTN_HEADS = 4
XATTN_DH = D_MODEL // XATTN_HEADS
D_FF = 4 * D_MODEL
FF_BLOCK = 1024
XATTN_ROWS = 256
PREP_ROWS = 256
EPS = 1e-6

HK = HEADS * DK
HV = HEADS * DV
C_GQ, C_GK, C_GV, C_GG = 0, HK, 2 * HK, 2 * HK + HV
C_RQ = C_GG + HV
C_RK = C_RQ + HK
C_RV = C_RK + HK
C_RG = C_RV + HV
C_LR = C_RG + HV
D_PROJ = C_LR + GATE_RANK_PAD

SEQ_TILE = 512
PROJ_PIECE = 256
CUM_BLOCK = 256
OUT_ROWS = 256
VMEM_LIMIT_BYTES = 56 * 1024 * 1024

F32 = jnp.float32
BF16 = jnp.bfloat16


def _dot(a, b):
    return jnp.dot(a, b, preferred_element_type=F32)


def _dot_nt(a, b):
    return lax.dot_general(a, b, (((1,), (1,)), ((), ())), preferred_element_type=F32)


def _rms(x, g):
    return x * lax.rsqrt(jnp.mean(x * x, axis=-1, keepdims=True) + EPS) * g


def _swish(g):
    return g / (1.0 + jnp.exp(-g))


def _stack_heads(t):
    lane_head = lax.broadcasted_iota(jnp.int32, t.shape, 1) // DK
    zero = jnp.zeros_like(t)
    return jnp.concatenate([jnp.where(lane_head == h, t, zero) for h in range(HEADS)], axis=0)


def _prep_w_in_kernel(w_ref, o_ref):
    lr0 = 2 * HK + HV
    o_ref[:, :lr0] = w_ref[0, :, :lr0].astype(BF16)
    o_ref[:, lr0:C_LR] = w_ref[0, :, lr0 + GATE_RANK:].astype(BF16)
    tail = w_ref[0, :, lr0:lr0 + GATE_RANK_PAD]
    lane = lax.broadcasted_iota(jnp.int32, tail.shape, 1)
    o_ref[:, C_LR:] = jnp.where(lane < GATE_RANK, tail, 0.0).astype(BF16)


def _mem_kv_kernel(mem_ref, g_ref, wkv_ref, wq_ref, wo_ref, wqk_ref, wvo_ref):
    n_mem = mem_ref.shape[1]
    mn = _rms(mem_ref[0], g_ref[...]).astype(BF16)
    kv = _dot(mn, wkv_ref[...])
    for h in range(XATTN_HEADS):
        cols = slice(h * XATTN_DH, (h + 1) * XATTN_DH)
        mcols = slice(h * n_mem, (h + 1) * n_mem)
        k_h = kv[:, cols].astype(BF16)
        v_h = kv[:, D_MODEL + h * XATTN_DH:D_MODEL + (h + 1) * XATTN_DH].astype(BF16)
        wqk_ref[0, :, mcols] = (_dot_nt(wq_ref[:, cols], k_h) * (XATTN_DH ** -0.5)).astype(BF16)
        wvo_ref[0, mcols, :] = _dot(v_h, wo_ref[cols, :]).astype(BF16)


def _mixer_kernel(xnext_ref, xres_ref, cosb_ref, sinb_ref, coso_ref, sino_ref, g_ref, win_ref, wg_ref,
                  bg_ref, glan_ref,
                  retn_ref, wout_ref, dmat_ref, qdec_ref, kdect_ref, sdec_ref,
                  o_ref, proj_a_ref, proj_b_ref, xn_ref, y_ref, kt_ref, sg_ref, sr_ref, *,
                  n_chunks, n_tiles):
    s = pl.program_id(0)
    prev_tile = jnp.maximum(s - 1, 0)

    @pl.when(s == 0)
    def _():
        proj_b_ref[...] = jnp.zeros_like(proj_b_ref)

    @pl.when(prev_tile % n_tiles == 0)
    def _():
        sg_ref[...] = jnp.zeros_like(sg_ref)
        sr_ref[...] = jnp.zeros_like(sr_ref)

    xn_ref[...] = _rms(xnext_ref[0], g_ref[...]).astype(BF16)

    chunk_pass = functools.partial(
        _chunk_pass, xn_ref=xn_ref, rope_refs=(cosb_ref, sinb_ref, coso_ref, sino_ref),
        win_ref=win_ref, wg_ref=wg_ref,
        bg_ref=bg_ref, glan_ref=glan_ref, retn_ref=retn_ref, dmat_ref=dmat_ref, qdec_ref=qdec_ref,
        kdect_ref=kdect_ref, sdec_ref=sdec_ref, y_ref=y_ref, kt_ref=kt_ref, sg_ref=sg_ref,
        sr_ref=sr_ref,
        xres_ref=xres_ref, wout_ref=wout_ref, o_ref=o_ref, n_chunks=n_chunks)

    @pl.when(s % 2 == 0)
    def _():
        chunk_pass(proj_a_ref, proj_b_ref)

    @pl.when(s % 2 == 1)
    def _():
        chunk_pass(proj_b_ref, proj_a_ref)


def _chunk_pass(cur_ref, prv_ref, *, xn_ref, rope_refs, win_ref, wg_ref, bg_ref, glan_ref,
                retn_ref, dmat_ref, qdec_ref, kdect_ref, sdec_ref, y_ref, kt_ref, sg_ref, sr_ref,
                xres_ref, wout_ref, o_ref, n_chunks):
    ts = n_chunks * CHUNK
    chunk_rows = [slice(c * CHUNK, (c + 1) * CHUNK) for c in range(n_chunks)]
    head_rows = [slice(h * CHUNK, (h + 1) * CHUNK) for h in range(HEADS)]
    head_cols = [slice(h * DV, (h + 1) * DV) for h in range(HEADS)]
    pieces = [(i * PROJ_PIECE, min((i + 1) * PROJ_PIECE, D_PROJ))
              for i in range(-(-D_PROJ // PROJ_PIECE))]

    def emit_proj(n):
        for _ in range(min(n, len(pieces))):
            lo, hi = pieces.pop(0)
            cur_ref[:, lo:hi] = _dot(xn_ref[...], win_ref[:, lo:hi])

    bi = lax.broadcasted_iota(jnp.int32, (CUM_BLOCK, CUM_BLOCK), 0)
    bj = lax.broadcasted_iota(jnp.int32, (CUM_BLOCK, CUM_BLOCK), 1)
    tril_blocks = ((bi // CHUNK == bj // CHUNK) & (bj <= bi)).astype(BF16)
    si = lax.broadcasted_iota(jnp.int32, (HEADS * CHUNK, CHUNK), 0) % CHUNK
    sj = lax.broadcasted_iota(jnp.int32, (HEADS * CHUNK, CHUNK), 1)
    lower = sj <= si
    first_half = (lax.broadcasted_iota(jnp.int32, (ts, HK), 1) % DK) < (DK // 2)
    lane_pad = jnp.zeros((HK, CHUNK), BF16)

    def contributions(kdt, v):
        return jnp.concatenate([_dot(kdt[head_rows[h]], v[:, head_cols[h]]) for h in range(HEADS)],
                               axis=0)

    def scan_states(s_ref, decays, contribs):
        state, starts = s_ref[...], []
        for dec, con in zip(decays, contribs):
            starts.append(state.astype(BF16))
            state = state * dec + con
        s_ref[...] = state
        return starts

    z = _dot(prv_ref[:, C_LR:C_LR + GATE_RANK_PAD].astype(BF16), wg_ref[...]) + bg_ref[...]
    emit_proj(1)
    log_a = (jnp.minimum(z, 0.0) - jnp.log1p(jnp.exp(-jnp.abs(z)))) * (1.0 / GATE_TAU)
    la_hi = log_a.astype(BF16)
    la_lo = (log_a - la_hi.astype(F32)).astype(BF16)
    la_split = jnp.concatenate([la_hi, la_lo], axis=1)
    bcum = []
    for b in range(ts // CUM_BLOCK):
        r = _dot(tril_blocks, la_split[b * CUM_BLOCK:(b + 1) * CUM_BLOCK])
        bcum.append(r[:, :HK] + r[:, HK:])
    bcum = jnp.concatenate(bcum, axis=0)
    emit_proj(1)
    q = prv_ref[:, C_GQ:C_GQ + HK] * (DK ** -0.5)
    qeb = (q * jnp.exp(bcum)).astype(BF16)
    qebi = (q * jnp.exp(-bcum)).astype(BF16)

    g_decay, g_contrib = [], []
    for c, rows in enumerate(chunk_rows):
        kt = prv_ref[rows, C_GK:C_GK + HK].T
        bct = bcum[rows].T
        blt = bct[:, CHUNK - 1:CHUNK]
        kt_ref[0, c] = (kt * jnp.exp(bct)).astype(BF16)
        kt_ref[1, c] = (kt * jnp.exp(-bct)).astype(BF16)
        kdt = (kt * jnp.exp(blt - bct)).astype(BF16)
        g_decay.append(jnp.exp(blt))
        g_contrib.append(contributions(kdt, prv_ref[rows, C_GV:C_GV + HV].astype(BF16)))
        if c % 2 == 1:
            emit_proj(1)

    cosb, sinb, coso, sino = [r[...] for r in rope_refs]
    cosb, sinb = cosb[0], sinb[0]
    cs = cosb * coso - sinb * sino
    sn = sinb * coso + cosb * sino
    sn = jnp.where(first_half, -sn, sn)

    def rope(t):
        partner = jnp.where(first_half, pltpu.roll(t, HK - DK // 2, 1), pltpu.roll(t, DK // 2, 1))
        return t * cs + partner * sn

    qrb = rope(prv_ref[:, C_RQ:C_RQ + HK]).astype(BF16)
    kr = rope(prv_ref[:, C_RK:C_RK + HK]) * (DK ** -0.5)
    r_contrib = []
    for c, rows in enumerate(chunk_rows):
        kt = kr[rows].T
        kt_ref[2, c] = kt.astype(BF16)
        kdt = (kt * kdect_ref[...]).astype(BF16)
        r_contrib.append(contributions(kdt, prv_ref[rows, C_RV:C_RV + HV].astype(BF16)))
        if c % 2 == 1:
            emit_proj(1)

    g_start = scan_states(sg_ref, g_decay, g_contrib)
    r_start = scan_states(sr_ref, [sdec_ref[...]] * n_chunks, r_contrib)

    def gla_scores(c):
        rows = chunk_rows[c]
        r = _dot(_stack_heads(qeb[rows]), jnp.concatenate([g_start[c], kt_ref[1, c], lane_pad], axis=1))
        a_up = _dot(_stack_heads(qebi[rows]), kt_ref[0, c])
        return r, a_up

    def ret_scores(c):
        return _dot(_stack_heads(qrb[chunk_rows[c]]),
                    jnp.concatenate([r_start[c], kt_ref[2, c], lane_pad], axis=1))

    def gla_values(c, r, a_up):
        rows = chunk_rows[c]
        attn = jnp.where(lower, r[:, DV:DV + CHUNK], a_up).astype(BF16)
        v = prv_ref[rows, C_GV:C_GV + HV].astype(BF16)
        for h in range(HEADS):
            o = _dot(attn[head_rows[h]], v[:, head_cols[h]]) + r[head_rows[h], :DV]
            gate = prv_ref[rows, C_GG + h * DV:C_GG + (h + 1) * DV]
            yn = o * lax.rsqrt(jnp.mean(o * o, axis=-1, keepdims=True) + EPS) * glan_ref[h:h + 1, :]
            y_ref[rows, h * DV:(h + 1) * DV] = (yn * _swish(gate)).astype(BF16)

    def ret_values(c, r):
        rows = chunk_rows[c]
        inter = r[:, :DV] * qdec_ref[...]
        scores = (r[:, DV:DV + CHUNK] * dmat_ref[...]).astype(BF16)
        v = prv_ref[rows, C_RV:C_RV + HV].astype(BF16)
        for h in range(HEADS):
            o = _dot(scores[head_rows[h]], v[:, head_cols[h]]) + inter[head_rows[h]]
            gate = prv_ref[rows, C_RG + h * DV:C_RG + (h + 1) * DV]
            d = o - jnp.mean(o, axis=-1, keepdims=True)
            yn = d * lax.rsqrt(jnp.mean(d * d, axis=-1, keepdims=True) + EPS) * retn_ref[h:h + 1, :]
            y_ref[rows, HV + h * DV:HV + (h + 1) * DV] = (yn * _swish(gate)).astype(BF16)

    def out_proj(done):
        o_ref[0, done, :] = xres_ref[0, done, :] + _dot(y_ref[done, :], wout_ref[...])

    finished = None
    ahead = (gla_scores(0), ret_scores(0))
    for c in range(n_chunks):
        (g_r, g_up), r_r = ahead
        if c + 1 < n_chunks:
            ahead = (gla_scores(c + 1), ret_scores(c + 1))
        if finished is not None:
            out_proj(finished)
            finished = None
        else:
            emit_proj(1)
        gla_values(c, g_r, g_up)
        ret_values(c, r_r)
        if (c + 1) % (OUT_ROWS // CHUNK) == 0:
            finished = slice((c + 1) * CHUNK - OUT_ROWS, (c + 1) * CHUNK)
    emit_proj(len(pieces))
    out_proj(finished)


def _xattn_mlp_kernel(h_ref, wqk_ref, wvo_ref, gx_ref, gf_ref, w1_ref, w2_ref, gfin_ref, o_ref):
    blocks = [slice(r, r + XATTN_ROWS) for r in range(0, h_ref.shape[1], XATTN_ROWS)]
    n_mem = wqk_ref.shape[2] // XATTN_HEADS
    h = [h_ref[0, rows, :] for rows in blocks]
    scores = [_dot(_rms(hb, gx_ref[...]).astype(BF16), wqk_ref[0]) for hb in h]
    probs = []
    for sb in scores:
        heads = []
        for a in range(XATTN_HEADS):
            s = sb[:, a * n_mem:(a + 1) * n_mem]
            e = jnp.exp(s - jnp.max(s, axis=-1, keepdims=True))
            heads.append((e * (1.0 / jnp.sum(e, axis=-1, keepdims=True))).astype(BF16))
        probs.append(jnp.concatenate(heads, axis=1))
    h = [hb + _dot(pb, wvo_ref[0]) for hb, pb in zip(h, probs)]

    hn = [_rms(hb, gf_ref[...]).astype(BF16) for hb in h]
    for c in range(D_FF // FF_BLOCK):
        cols = slice(c * FF_BLOCK, (c + 1) * FF_BLOCK)
        u = [jnp.maximum(_dot(hb, w1_ref[:, cols]), 0.0) for hb in hn]
        h = [hb + _dot((ub * ub).astype(BF16), w2_ref[cols, :]) for hb, ub in zip(h, u)]
    for rows, hb in zip(blocks, h):
        o_ref[0, rows, :] = _rms(hb, gfin_ref[...])


def _const_spec(shape):
    n = len(shape)
    return pl.BlockSpec(shape, lambda *_: (0,) * n, pipeline_mode=pl.Buffered(1))


def _retention_tables():
    gamma = 1.0 - np.exp2(-np.linspace(RET_DECAY_EXP_LO, RET_DECAY_EXP_HI, HEADS, dtype=np.float32))
    log_g = np.log(gamma.astype(np.float32)).astype(np.float32)
    pos = np.arange(CHUNK, dtype=np.float32)
    dmat = np.exp(np.abs(pos[:, None] - pos[None, :])[None] * log_g[:, None, None])
    k_dec = np.exp((CHUNK - 1.0 - pos)[None, :] * log_g[:, None])
    q_dec = np.exp((pos + 1.0)[None, :] * log_g[:, None])
    s_dec = np.exp(CHUNK * log_g)
    dmat_s = dmat.reshape(HEADS * CHUNK, CHUNK)
    qdec_s = np.broadcast_to(q_dec.reshape(HEADS * CHUNK, 1), (HEADS * CHUNK, DV))
    kdect_s = np.repeat(k_dec, DK, axis=0)
    sdec_s = np.broadcast_to(np.repeat(s_dec, DK)[:, None], (HEADS * DK, DV))
    as32 = lambda a: jnp.asarray(np.ascontiguousarray(a), dtype=F32)
    return as32(dmat_s), as32(qdec_s), as32(kdect_s), as32(sdec_s)


def kernel(x, mem, norm_mix_g, w_in, gla_w_gate, gla_b_gate, gla_norm_g, ret_norm_g, w_out, norm_x_g,
           norm_mem_g, xa_w_q, xa_w_kv, xa_w_o, norm_ffn_g, ffn_w1, ffn_w2, final_norm_g):
    B, S, D = x.shape
    n_mem = mem.shape[1]
    assert D == D_MODEL and S % SEQ_TILE == 0 and w_in.shape[0] == 1
    ts = SEQ_TILE
    n_tiles = S // ts

    inv_freq = ROPE_BASE ** (-jnp.arange(0, DK, 2, dtype=F32) / DK)
    inv_lane = jnp.tile(inv_freq, 2 * HEADS)
    ang_off = jnp.arange(ts, dtype=F32)[:, None] * inv_lane[None, :]
    ang_base = (jnp.arange(n_tiles, dtype=F32) * ts)[:, None, None] * inv_lane[None, None, :]
    cos_off, sin_off = jnp.cos(ang_off), jnp.sin(ang_off)
    cos_base, sin_base = jnp.cos(ang_base), jnp.sin(ang_base)

    w_in_r = pl.pallas_call(
        _prep_w_in_kernel,
        grid=(D // PREP_ROWS,),
        in_specs=[pl.BlockSpec((1, PREP_ROWS, w_in.shape[2]), lambda i: (0, i, 0))],
        out_specs=pl.BlockSpec((PREP_ROWS, D_PROJ), lambda i: (i, 0)),
        out_shape=jax.ShapeDtypeStruct((D, D_PROJ), BF16),
        compiler_params=pltpu.CompilerParams(dimension_semantics=("arbitrary",),
                                             vmem_limit_bytes=VMEM_LIMIT_BYTES),
        name="prep_w_in",
    )(w_in)
    w_gate_p = jnp.concatenate(
        [gla_w_gate[0], jnp.zeros((GATE_RANK_PAD - GATE_RANK, HK), F32)], axis=0).astype(BF16)
    dmat_s, qdec_s, kdect_s, sdec_s = _retention_tables()

    params = pltpu.CompilerParams(dimension_semantics=("arbitrary", "arbitrary"),
                                  vmem_limit_bytes=VMEM_LIMIT_BYTES)

    hm = XATTN_HEADS * n_mem
    wqk, wvo = pl.pallas_call(
        _mem_kv_kernel,
        grid=(B,),
        in_specs=[pl.BlockSpec((1, n_mem, D), lambda b: (b, 0, 0)),
                  _const_spec((1, D)), _const_spec((D, 2 * D)), _const_spec((D, D)),
                  _const_spec((D, D))],
        out_specs=[pl.BlockSpec((1, D, hm), lambda b: (b, 0, 0)),
                   pl.BlockSpec((1, hm, D), lambda b: (b, 0, 0))],
        out_shape=[jax.ShapeDtypeStruct((B, D, hm), BF16), jax.ShapeDtypeStruct((B, hm, D), BF16)],
        compiler_params=pltpu.CompilerParams(dimension_semantics=("arbitrary",),
                                             vmem_limit_bytes=VMEM_LIMIT_BYTES),
        name="mem_kv",
    )(mem, norm_mem_g[0][None, :], xa_w_kv[0].astype(BF16), xa_w_q[0].astype(BF16),
      xa_w_o[0].astype(BF16))

    n_steps = B * n_tiles + 1

    def next_tile(s):
        p = jnp.minimum(s, n_steps - 2)
        return (p // n_tiles, p % n_tiles, 0)

    def prev_tile(s):
        q = jnp.maximum(s - 1, 0)
        return (q // n_tiles, q % n_tiles, 0)

    def prev_rope(s):
        return (jnp.maximum(s - 1, 0) % n_tiles, 0, 0)

    h1 = pl.pallas_call(
        functools.partial(_mixer_kernel, n_chunks=ts // CHUNK, n_tiles=n_tiles),
        grid=(n_steps,),
        in_specs=[pl.BlockSpec((1, ts, D), next_tile),
                  pl.BlockSpec((1, ts, D), prev_tile),
                  pl.BlockSpec((1, 1, HK), prev_rope), pl.BlockSpec((1, 1, HK), prev_rope),
                  _const_spec((ts, HK)), _const_spec((ts, HK)),
                  _const_spec((1, D)), _const_spec((D, D_PROJ)),
                  _const_spec((GATE_RANK_PAD, HK)), _const_spec((1, HK)),
                  _const_spec((HEADS, DV)), _const_spec((HEADS, DV)),
                  _const_spec((D, D)),
                  _const_spec((HEADS * CHUNK, CHUNK)), _const_spec((HEADS * CHUNK, DV)),
                  _const_spec((HEADS * DK, CHUNK)), _const_spec((HEADS * DK, DV))],
        out_specs=pl.BlockSpec((1, ts, D), prev_tile),
        out_shape=jax.ShapeDtypeStruct((B, S, D), F32),
        scratch_shapes=[pltpu.VMEM((ts, D_PROJ), F32), pltpu.VMEM((ts, D_PROJ), F32),
                        pltpu.VMEM((ts, D), BF16), pltpu.VMEM((ts, D), BF16),
                        pltpu.VMEM((3, ts // CHUNK, HK, CHUNK), BF16),
                        pltpu.VMEM((HEADS * DK, DV), F32), pltpu.VMEM((HEADS * DK, DV), F32)],
        compiler_params=pltpu.CompilerParams(dimension_semantics=("arbitrary",),
                                             vmem_limit_bytes=VMEM_LIMIT_BYTES),
        name="mixer",
    )(x, x, cos_base, sin_base, cos_off, sin_off, norm_mix_g[0][None, :], w_in_r, w_gate_p, gla_b_gate[0][None, :],
      gla_norm_g[0], ret_norm_g[0], w_out[0].astype(BF16), dmat_s, qdec_s, kdect_s, sdec_s)

    out = pl.pallas_call(
        _xattn_mlp_kernel,
        grid=(B, n_tiles),
        in_specs=[pl.BlockSpec((1, ts, D), lambda b, t: (b, t, 0)),
                  pl.BlockSpec((1, D, hm), lambda b, t: (b, 0, 0)),
                  pl.BlockSpec((1, hm, D), lambda b, t: (b, 0, 0)),
                  _const_spec((1, D)),
                  _const_spec((1, D)), _const_spec((D, D_FF)), _const_spec((D_FF, D)),
                  _const_spec((1, D))],
        out_specs=pl.BlockSpec((1, ts, D), lambda b, t: (b, t, 0)),
        out_shape=jax.ShapeDtypeStruct((B, S, D), F32),
        compiler_params=params,
        name="xattn_mlp",
    )(h1, wqk, wvo, norm_x_g[0][None, :], norm_ffn_g[0][None, :], ffn_w1[0].astype(BF16), ffn_w2[0].astype(BF16), final_norm_g[None, :])
    return out
```

```python
import functools

import numpy as np
import jax
import jax.numpy as jnp
from jax import lax
from jax.experimental import pallas as pl
from jax.experimental.pallas import tpu as pltpu

D_MODEL = 1024
CHUNK = 64
HEADS = 4
DK = 64
DV = 128
GATE_RANK = 16
GATE_RANK_PAD = 128
GATE_TAU = 16.0
RET_DECAY_EXP_LO = 5.0
RET_DECAY_EXP_HI = 12.0
ROPE_BASE = 10000.0
XATTN_HEADS = 4
XATTN_DH = D_MODEL // XATTN_HEADS
D_FF = 4 * D_MODEL
FF_BLOCK = 1024
XATTN_TILE = 1024
XATTN_ROWS = 256
PREP_ROWS = 256
EPS = 1e-6

HK = HEADS * DK
HV = HEADS * DV
C_GQ, C_GK, C_GV, C_GG = 0, HK, 2 * HK, 2 * HK + HV
C_RQ = C_GG + HV
C_RK = C_RQ + HK
C_RV = C_RK + HK
C_RG = C_RV + HV
C_LR = C_RG + HV
D_PROJ = C_LR + GATE_RANK_PAD

SEQ_TILE = 512
PROJ_PIECE = 256
CUM_BLOCK = 256
OUT_ROWS = 256
VMEM_LIMIT_BYTES = 56 * 1024 * 1024

F32 = jnp.float32
BF16 = jnp.bfloat16


def _dot(a, b):
    return jnp.dot(a, b, preferred_element_type=F32)


def _dot_nt(a, b):
    return lax.dot_general(a, b, (((1,), (1,)), ((), ())), preferred_element_type=F32)


def _rms(x, g):
    return x * lax.rsqrt(jnp.mean(x * x, axis=-1, keepdims=True) + EPS) * g


def _swish(g):
    return g / (1.0 + jnp.exp(-g))


def _stack_heads(t):
    lane_head = lax.broadcasted_iota(jnp.int32, t.shape, 1) // DK
    zero = jnp.zeros_like(t)
    return jnp.concatenate([jnp.where(lane_head == h, t, zero) for h in range(HEADS)], axis=0)


def _prep_w_in_kernel(wt_ref, o_ref):
    lr0 = 2 * HK + HV
    o_ref[:, :lr0] = wt_ref[:lr0, :].T.astype(BF16)
    o_ref[:, lr0:C_LR] = wt_ref[lr0 + GATE_RANK:, :].T.astype(BF16)
    tail = wt_ref[lr0:lr0 + GATE_RANK_PAD, :]
    row = lax.broadcasted_iota(jnp.int32, tail.shape, 0)
    o_ref[:, C_LR:] = jnp.where(row < GATE_RANK, tail, 0.0).T.astype(BF16)


def _mem_kv_kernel(mem_ref, g_ref, wkv_ref, wq_ref, wo_ref, wqk_ref, wvo_ref):
    n_mem = mem_ref.shape[1]
    mn = _rms(mem_ref[0], g_ref[...]).astype(BF16)
    kv = _dot(mn, wkv_ref[...])
    for h in range(XATTN_HEADS):
        cols = slice(h * XATTN_DH, (h + 1) * XATTN_DH)
        mcols = slice(h * n_mem, (h + 1) * n_mem)
        k_h = kv[:, cols].astype(BF16)
        v_h = kv[:, D_MODEL + h * XATTN_DH:D_MODEL + (h + 1) * XATTN_DH].astype(BF16)
        wqk_ref[0, :, mcols] = (_dot_nt(wq_ref[:, cols], k_h) * (XATTN_DH ** -0.5)).astype(BF16)
        wvo_ref[0, mcols, :] = _dot(v_h, wo_ref[cols, :]).astype(BF16)


def _mixer_kernel(xnext_ref, xres_ref, cosb_ref, sinb_ref, coso_ref, sino_ref, g_ref, win_ref, wg_ref,
                  bg_ref, glan_ref,
                  retn_ref, wout_ref, dmat_ref, qdec_ref, kdect_ref, sdec_ref,
                  o_ref, proj_a_ref, proj_b_ref, xn_ref, y_ref, kt_ref, sg_ref, sr_ref, *,
                  n_chunks, n_tiles):
    s = pl.program_id(0)
    prev_tile = jnp.maximum(s - 1, 0)

    @pl.when(s == 0)
    def _():
        proj_b_ref[...] = jnp.zeros_like(proj_b_ref)

    @pl.when(prev_tile % n_tiles == 0)
    def _():
        sg_ref[...] = jnp.zeros_like(sg_ref)
        sr_ref[...] = jnp.zeros_like(sr_ref)

    xn_ref[...] = _rms(xnext_ref[0], g_ref[...]).astype(BF16)

    chunk_pass = functools.partial(
        _chunk_pass, xn_ref=xn_ref, rope_refs=(cosb_ref, sinb_ref, coso_ref, sino_ref),
        win_ref=win_ref, wg_ref=wg_ref,
        bg_ref=bg_ref, glan_ref=glan_ref, retn_ref=retn_ref, dmat_ref=dmat_ref, qdec_ref=qdec_ref,
        kdect_ref=kdect_ref, sdec_ref=sdec_ref, y_ref=y_ref, kt_ref=kt_ref, sg_ref=sg_ref,
        sr_ref=sr_ref,
        xres_ref=xres_ref, wout_ref=wout_ref, o_ref=o_ref, n_chunks=n_chunks)

    @pl.when(s % 2 == 0)
    def _():
        chunk_pass(proj_a_ref, proj_b_ref)

    @pl.when(s % 2 == 1)
    def _():
        chunk_pass(proj_b_ref, proj_a_ref)


def _chunk_pass(cur_ref, prv_ref, *, xn_ref, rope_refs, win_ref, wg_ref, bg_ref, glan_ref,
                retn_ref, dmat_ref, qdec_ref, kdect_ref, sdec_ref, y_ref, kt_ref, sg_ref, sr_ref,
                xres_ref, wout_ref, o_ref, n_chunks):
    ts = n_chunks * CHUNK
    chunk_rows = [slice(c * CHUNK, (c + 1) * CHUNK) for c in range(n_chunks)]
    head_rows = [slice(h * CHUNK, (h + 1) * CHUNK) for h in range(HEADS)]
    head_cols = [slice(h * DV, (h + 1) * DV) for h in range(HEADS)]
    pieces = [(i * PROJ_PIECE, min((i + 1) * PROJ_PIECE, D_PROJ))
              for i in range(-(-D_PROJ // PROJ_PIECE))]

    def emit_proj(n):
        for _ in range(min(n, len(pieces))):
            lo, hi = pieces.pop(0)
            cur_ref[:, lo:hi] = _dot(xn_ref[...], win_ref[:, lo:hi])

    bi = lax.broadcasted_iota(jnp.int32, (CUM_BLOCK, CUM_BLOCK), 0)
    bj = lax.broadcasted_iota(jnp.int32, (CUM_BLOCK, CUM_BLOCK), 1)
    tril_blocks = ((bi // CHUNK == bj // CHUNK) & (bj <= bi)).astype(BF16)
    si = lax.broadcasted_iota(jnp.int32, (HEADS * CHUNK, CHUNK), 0) % CHUNK
    sj = lax.broadcasted_iota(jnp.int32, (HEADS * CHUNK, CHUNK), 1)
    lower = sj <= si
    first_half = (lax.broadcasted_iota(jnp.int32, (ts, HK), 1) % DK) < (DK // 2)
    lane_pad = jnp.zeros((HK, CHUNK), BF16)

    def contributions(kdt, v):
        return jnp.concatenate([_dot(kdt[head_rows[h]], v[:, head_cols[h]]) for h in range(HEADS)],
                               axis=0)

    def scan_states(s_ref, decays, contribs):
        state, starts = s_ref[...], []
        for dec, con in zip(decays, contribs):
            starts.append(state.astype(BF16))
            state = state * dec + con
        s_ref[...] = state
        return starts

    z = _dot(prv_ref[:, C_LR:C_LR + GATE_RANK_PAD].astype(BF16), wg_ref[...]) + bg_ref[...]
    emit_proj(1)
    log_a = (jnp.minimum(z, 0.0) - jnp.log1p(jnp.exp(-jnp.abs(z)))) * (1.0 / GATE_TAU)
    la_hi = log_a.astype(BF16)
    la_lo = (log_a - la_hi.astype(F32)).astype(BF16)
    la_split = jnp.concatenate([la_hi, la_lo], axis=1)
    bcum = []
    for b in range(ts // CUM_BLOCK):
        r = _dot(tril_blocks, la_split[b * CUM_BLOCK:(b + 1) * CUM_BLOCK])
        bcum.append(r[:, :HK] + r[:, HK:])
    bcum = jnp.concatenate(bcum, axis=0)
    emit_proj(1)
    q = prv_ref[:, C_GQ:C_GQ + HK] * (DK ** -0.5)
    qeb = (q * jnp.exp(bcum)).astype(BF16)
    qebi = (q * jnp.exp(-bcum)).astype(BF16)

    g_decay, g_contrib = [], []
    for c, rows in enumerate(chunk_rows):
        kt = prv_ref[rows, C_GK:C_GK + HK].T
        bct = bcum[rows].T
        blt = bct[:, CHUNK - 1:CHUNK]
        kt_ref[0, c] = (kt * jnp.exp(bct)).astype(BF16)
        kt_ref[1, c] = (kt * jnp.exp(-bct)).astype(BF16)
        kdt = (kt * jnp.exp(blt - bct)).astype(BF16)
        g_decay.append(jnp.exp(blt))
        g_contrib.append(contributions(kdt, prv_ref[rows, C_GV:C_GV + HV].astype(BF16)))
        if c % 2 == 1:
            emit_proj(1)

    cosb, sinb, coso, sino = [r[...] for r in rope_refs]
    cosb, sinb = cosb[0], sinb[0]
    cs = cosb * coso - sinb * sino
    sn = sinb * coso + cosb * sino
    sn = jnp.where(first_half, -sn, sn)

    def rope(t):
        partner = jnp.where(first_half, pltpu.roll(t, HK - DK // 2, 1), pltpu.roll(t, DK // 2, 1))
        return t * cs + partner * sn

    qrb = rope(prv_ref[:, C_RQ:C_RQ + HK]).astype(BF16)
    kr = rope(prv_ref[:, C_RK:C_RK + HK]) * (DK ** -0.5)
    r_contrib = []
    for c, rows in enumerate(chunk_rows):
        kt = kr[rows].T
        kt_ref[2, c] = kt.astype(BF16)
        kdt = (kt * kdect_ref[...]).astype(BF16)
        r_contrib.append(contributions(kdt, prv_ref[rows, C_RV:C_RV + HV].astype(BF16)))
        if c % 2 == 1:
            emit_proj(1)

    g_start = scan_states(sg_ref, g_decay, g_contrib)
    r_start = scan_states(sr_ref, [sdec_ref[...]] * n_chunks, r_contrib)

    def gla_scores(c):
        rows = chunk_rows[c]
        r = _dot(_stack_heads(qeb[rows]), jnp.concatenate([g_start[c], kt_ref[1, c], lane_pad], axis=1))
        a_up = _dot(_stack_heads(qebi[rows]), kt_ref[0, c])
        return r, a_up

    def ret_scores(c):
        return _dot(_stack_heads(qrb[chunk_rows[c]]),
                    jnp.concatenate([r_start[c], kt_ref[2, c], lane_pad], axis=1))

    def gla_values(c, r, a_up):
        rows = chunk_rows[c]
        attn = jnp.where(lower, r[:, DV:DV + CHUNK], a_up).astype(BF16)
        v = prv_ref[rows, C_GV:C_GV + HV].astype(BF16)
        for h in range(HEADS):
            o = _dot(attn[head_rows[h]], v[:, head_cols[h]]) + r[head_rows[h], :DV]
            gate = prv_ref[rows, C_GG + h * DV:C_GG + (h + 1) * DV]
            yn = o * lax.rsqrt(jnp.mean(o * o, axis=-1, keepdims=True) + EPS) * glan_ref[h:h + 1, :]
            y_ref[rows, h * DV:(h + 1) * DV] = (yn * _swish(gate)).astype(BF16)

    def ret_values(c, r):
        rows = chunk_rows[c]
        inter = r[:, :DV] * qdec_ref[...]
        scores = (r[:, DV:DV + CHUNK] * dmat_ref[...]).astype(BF16)
        v = prv_ref[rows, C_RV:C_RV + HV].astype(BF16)
        for h in range(HEADS):
            o = _dot(scores[head_rows[h]], v[:, head_cols[h]]) + inter[head_rows[h]]
            gate = prv_ref[rows, C_RG + h * DV:C_RG + (h + 1) * DV]
            d = o - jnp.mean(o, axis=-1, keepdims=True)
            yn = d * lax.rsqrt(jnp.mean(d * d, axis=-1, keepdims=True) + EPS) * retn_ref[h:h + 1, :]
            y_ref[rows, HV + h * DV:HV + (h + 1) * DV] = (yn * _swish(gate)).astype(BF16)

    def out_proj(done):
        o_ref[0, done, :] = xres_ref[0, done, :] + _dot(y_ref[done, :], wout_ref[...])

    finished = None
    ahead = (gla_scores(0), ret_scores(0))
    for c in range(n_chunks):
        (g_r, g_up), r_r = ahead
        if c + 1 < n_chunks:
            ahead = (gla_scores(c + 1), ret_scores(c + 1))
        if finished is not None:
            out_proj(finished)
            finished = None
        else:
            emit_proj(1)
        gla_values(c, g_r, g_up)
        ret_values(c, r_r)
        if (c + 1) % (OUT_ROWS // CHUNK) == 0:
            finished = slice((c + 1) * CHUNK - OUT_ROWS, (c + 1) * CHUNK)
    emit_proj(len(pieces))
    out_proj(finished)


def _xattn_mlp_kernel(h_ref, wqk_ref, wvo_ref, gx_ref, gf_ref, w1_ref, w2_ref, gfin_ref, o_ref):
    blocks = [slice(r, r + XATTN_ROWS) for r in range(0, h_ref.shape[1], XATTN_ROWS)]
    n_mem = wqk_ref.shape[2] // XATTN_HEADS
    h = [h_ref[0, rows, :] for rows in blocks]
    scores = [_dot(_rms(hb, gx_ref[...]).astype(BF16), wqk_ref[0]) for hb in h]
    probs = []
    for sb in scores:
        heads = []
        for a in range(XATTN_HEADS):
            s = sb[:, a * n_mem:(a + 1) * n_mem]
            e = jnp.exp(s - jnp.max(s, axis=-1, keepdims=True))
            heads.append((e * (1.0 / jnp.sum(e, axis=-1, keepdims=True))).astype(BF16))
        probs.append(jnp.concatenate(heads, axis=1))
    h = [hb + _dot(pb, wvo_ref[0]) for hb, pb in zip(h, probs)]

    hn = [_rms(hb, gf_ref[...]).astype(BF16) for hb in h]
    for c in range(D_FF // FF_BLOCK):
        cols = slice(c * FF_BLOCK, (c + 1) * FF_BLOCK)
        u = [jnp.maximum(_dot(hb, w1_ref[:, cols]), 0.0) for hb in hn]
        h = [hb + _dot((ub * ub).astype(BF16), w2_ref[cols, :]) for hb, ub in zip(h, u)]
    for rows, hb in zip(blocks, h):
        o_ref[0, rows, :] = _rms(hb, gfin_ref[...])


def _const_spec(shape):
    n = len(shape)
    return pl.BlockSpec(shape, lambda *_: (0,) * n, pipeline_mode=pl.Buffered(1))


def _retention_tables():
    gamma = 1.0 - np.exp2(-np.linspace(RET_DECAY_EXP_LO, RET_DECAY_EXP_HI, HEADS, dtype=np.float32))
    log_g = np.log(gamma.astype(np.float32)).astype(np.float32)
    pos = np.arange(CHUNK, dtype=np.float32)
    dmat = np.exp(np.abs(pos[:, None] - pos[None, :])[None] * log_g[:, None, None])
    k_dec = np.exp((CHUNK - 1.0 - pos)[None, :] * log_g[:, None])
    q_dec = np.exp((pos + 1.0)[None, :] * log_g[:, None])
    s_dec = np.exp(CHUNK * log_g)
    dmat_s = dmat.reshape(HEADS * CHUNK, CHUNK)
    qdec_s = np.broadcast_to(q_dec.reshape(HEADS * CHUNK, 1), (HEADS * CHUNK, DV))
    kdect_s = np.repeat(k_dec, DK, axis=0)
    sdec_s = np.broadcast_to(np.repeat(s_dec, DK)[:, None], (HEADS * DK, DV))
    as32 = lambda a: jnp.asarray(np.ascontiguousarray(a), dtype=F32)
    return as32(dmat_s), as32(qdec_s), as32(kdect_s), as32(sdec_s)


def kernel(x, mem, norm_mix_g, w_in, gla_w_gate, gla_b_gate, gla_norm_g, ret_norm_g, w_out, norm_x_g,
           norm_mem_g, xa_w_q, xa_w_kv, xa_w_o, norm_ffn_g, ffn_w1, ffn_w2, final_norm_g):
    B, S, D = x.shape
    n_mem = mem.shape[1]
    assert D == D_MODEL and S % SEQ_TILE == 0 and S % XATTN_TILE == 0 and w_in.shape[0] == 1
    ts = SEQ_TILE
    n_tiles = S // ts

    inv_freq = ROPE_BASE ** (-jnp.arange(0, DK, 2, dtype=F32) / DK)
    inv_lane = jnp.tile(inv_freq, 2 * HEADS)
    ang_off = jnp.arange(ts, dtype=F32)[:, None] * inv_lane[None, :]
    ang_base = (jnp.arange(n_tiles, dtype=F32) * ts)[:, None, None] * inv_lane[None, None, :]
    cos_off, sin_off = jnp.cos(ang_off), jnp.sin(ang_off)
    cos_base, sin_base = jnp.cos(ang_base), jnp.sin(ang_base)

    w_in_r = pl.pallas_call(
        _prep_w_in_kernel,
        grid=(D // PREP_ROWS,),
        in_specs=[pl.BlockSpec((w_in.shape[2], PREP_ROWS), lambda i: (0, i))],
        out_specs=pl.BlockSpec((PREP_ROWS, D_PROJ), lambda i: (i, 0)),
        out_shape=jax.ShapeDtypeStruct((D, D_PROJ), BF16),
        compiler_params=pltpu.CompilerParams(dimension_semantics=("arbitrary",),
                                             vmem_limit_bytes=VMEM_LIMIT_BYTES),
        name="prep_w_in",
    )(jnp.swapaxes(w_in[0], 0, 1))
    w_gate_p = jnp.concatenate(
        [gla_w_gate[0], jnp.zeros((GATE_RANK_PAD - GATE_RANK, HK), F32)], axis=0).astype(BF16)
    dmat_s, qdec_s, kdect_s, sdec_s = _retention_tables()

    params = pltpu.CompilerParams(dimension_semantics=("arbitrary", "arbitrary"),
                                  vmem_limit_bytes=VMEM_LIMIT_BYTES)

    hm = XATTN_HEADS * n_mem
    wqk, wvo = pl.pallas_call(
        _mem_kv_kernel,
        grid=(B,),
        in_specs=[pl.BlockSpec((1, n_mem, D), lambda b: (b, 0, 0)),
                  _const_spec((1, D)), _const_spec((D, 2 * D)), _const_spec((D, D)),
                  _const_spec((D, D))],
        out_specs=[pl.BlockSpec((1, D, hm), lambda b: (b, 0, 0)),
                   pl.BlockSpec((1, hm, D), lambda b: (b, 0, 0))],
        out_shape=[jax.ShapeDtypeStruct((B, D, hm), BF16), jax.ShapeDtypeStruct((B, hm, D), BF16)],
        compiler_params=pltpu.CompilerParams(dimension_semantics=("arbitrary",),
                                             vmem_limit_bytes=VMEM_LIMIT_BYTES),
        name="mem_kv",
    )(mem, norm_mem_g[0][None, :], xa_w_kv[0].astype(BF16), xa_w_q[0].astype(BF16),
      xa_w_o[0].astype(BF16))

    n_steps = B * n_tiles + 1

    def next_tile(s):
        p = jnp.minimum(s, n_steps - 2)
        return (p // n_tiles, p % n_tiles, 0)

    def prev_tile(s):
        q = jnp.maximum(s - 1, 0)
        return (q // n_tiles, q % n_tiles, 0)

    def prev_rope(s):
        return (jnp.maximum(s - 1, 0) % n_tiles, 0, 0)

    h1 = pl.pallas_call(
        functools.partial(_mixer_kernel, n_chunks=ts // CHUNK, n_tiles=n_tiles),
        grid=(n_steps,),
        in_specs=[pl.BlockSpec((1, ts, D), next_tile),
                  pl.BlockSpec((1, ts, D), prev_tile),
                  pl.BlockSpec((1, 1, HK), prev_rope), pl.BlockSpec((1, 1, HK), prev_rope),
                  _const_spec((ts, HK)), _const_spec((ts, HK)),
                  _const_spec((1, D)), _const_spec((D, D_PROJ)),
                  _const_spec((GATE_RANK_PAD, HK)), _const_spec((1, HK)),
                  _const_spec((HEADS, DV)), _const_spec((HEADS, DV)),
                  _const_spec((D, D)),
                  _const_spec((HEADS * CHUNK, CHUNK)), _const_spec((HEADS * CHUNK, DV)),
                  _const_spec((HEADS * DK, CHUNK)), _const_spec((HEADS * DK, DV))],
        out_specs=pl.BlockSpec((1, ts, D), prev_tile),
        out_shape=jax.ShapeDtypeStruct((B, S, D), F32),
        scratch_shapes=[pltpu.VMEM((ts, D_PROJ), F32), pltpu.VMEM((ts, D_PROJ), F32),
                        pltpu.VMEM((ts, D), BF16), pltpu.VMEM((ts, D), BF16),
                        pltpu.VMEM((3, ts // CHUNK, HK, CHUNK), BF16),
                        pltpu.VMEM((HEADS * DK, DV), F32), pltpu.VMEM((HEADS * DK, DV), F32)],
        compiler_params=pltpu.CompilerParams(dimension_semantics=("arbitrary",),
                                             vmem_limit_bytes=VMEM_LIMIT_BYTES),
        name="mixer",
    )(x, x, cos_base, sin_base, cos_off, sin_off, norm_mix_g[0][None, :], w_in_r, w_gate_p, gla_b_gate[0][None, :],
      gla_norm_g[0], ret_norm_g[0], w_out[0].astype(BF16), dmat_s, qdec_s, kdect_s, sdec_s)

    out = pl.pallas_call(
        _xattn_mlp_kernel,
        grid=(B, S // XATTN_TILE),
        in_specs=[pl.BlockSpec((1, XATTN_TILE, D), lambda b, t: (b, t, 0)),
                  pl.BlockSpec((1, D, hm), lambda b, t: (b, 0, 0)),
                  pl.BlockSpec((1, hm, D), lambda b, t: (b, 0, 0)),
                  _const_spec((1, D)),
                  _const_spec((1, D)), _const_spec((D, D_FF)), _const_spec((D_FF, D)),
                  _const_spec((1, D))],
        out_specs=pl.BlockSpec((1, XATTN_TILE, D), lambda b, t: (b, t, 0)),
        out_shape=jax.ShapeDtypeStruct((B, S, D), F32),
        compiler_params=params,
        name="xattn_mlp",
    )(h1, wqk, wvo, norm_x_g[0][None, :], norm_ffn_g[0][None, :], ffn_w1[0].astype(BF16),
      ffn_w2[0].astype(BF16), final_norm_g[None, :])
    return out
```

```python
import functools

import numpy as np
import jax
import jax.numpy as jnp
from jax import lax
from jax.experimental import pallas as pl
from jax.experimental.pallas import tpu as pltpu

D_MODEL = 1024
CHUNK = 64
HEADS = 4
DK = 64
DV = 128
GATE_RANK = 16
GATE_RANK_PAD = 128
GATE_TAU = 16.0
RET_DECAY_EXP_LO = 5.0
RET_DECAY_EXP_HI = 12.0
ROPE_BASE = 10000.0
XATTN_HEADS = 4
XATTN_DH = D_MODEL // XATTN_HEADS
D_FF = 4 * D_MODEL
FF_BLOCK = 1024
XATTN_TILE = 512
XATTN_ROWS = 256
PREP_ROWS = 256
EPS = 1e-6

HK = HEADS * DK
HV = HEADS * DV
C_GQ, C_GK, C_GV, C_GG = 0, HK, 2 * HK, 2 * HK + HV
C_RQ = C_GG + HV
C_RK = C_RQ + HK
C_RV = C_RK + HK
C_RG = C_RV + HV
C_LR = C_RG + HV
D_PROJ = C_LR + GATE_RANK_PAD

SEQ_TILE = 512
PROJ_PIECE = 256
CUM_BLOCK = 256
OUT_ROWS = 256
VMEM_LIMIT_BYTES = 56 * 1024 * 1024

F32 = jnp.float32
BF16 = jnp.bfloat16


def _dot(a, b):
    return jnp.dot(a, b, preferred_element_type=F32)


def _dot_nt(a, b):
    return lax.dot_general(a, b, (((1,), (1,)), ((), ())), preferred_element_type=F32)


def _rms(x, g):
    return x * lax.rsqrt(jnp.mean(x * x, axis=-1, keepdims=True) + EPS) * g


def _swish(g):
    return g / (1.0 + jnp.exp(-g))


def _stack_heads(t):
    lane_head = lax.broadcasted_iota(jnp.int32, t.shape, 1) // DK
    zero = jnp.zeros_like(t)
    return jnp.concatenate([jnp.where(lane_head == h, t, zero) for h in range(HEADS)], axis=0)


def _prep_w_in_kernel(wt_ref, o_ref):
    lr0 = 2 * HK + HV
    o_ref[:, :lr0] = wt_ref[:lr0, :].T.astype(BF16)
    o_ref[:, lr0:C_LR] = wt_ref[lr0 + GATE_RANK:, :].T.astype(BF16)
    tail = wt_ref[lr0:lr0 + GATE_RANK_PAD, :]
    row = lax.broadcasted_iota(jnp.int32, tail.shape, 0)
    o_ref[:, C_LR:] = jnp.where(row < GATE_RANK, tail, 0.0).T.astype(BF16)


def _mem_kv_kernel(mem_ref, g_ref, wkv_ref, wq_ref, wo_ref, wqk_ref, wvo_ref):
    n_mem = mem_ref.shape[1]
    mn = _rms(mem_ref[0], g_ref[...])
    kv = _dot(mn, wkv_ref[...])
    for h in range(XATTN_HEADS):
        cols = slice(h * XATTN_DH, (h + 1) * XATTN_DH)
        mcols = slice(h * n_mem, (h + 1) * n_mem)
        k_h = kv[:, cols]
        v_h = kv[:, D_MODEL + h * XATTN_DH:D_MODEL + (h + 1) * XATTN_DH]
        wqk_ref[0, :, mcols] = (_dot_nt(wq_ref[:, cols], k_h) * (XATTN_DH ** -0.5)).astype(BF16)
        wvo_ref[0, mcols, :] = _dot(v_h, wo_ref[cols, :]).astype(BF16)


def _mixer_kernel(xnext_ref, xres_ref, cosb_ref, sinb_ref, coso_ref, sino_ref, g_ref, win_ref, wg_ref,
                  bg_ref, glan_ref, retn_ref, wout_ref, dmat_ref, qdec_ref, kdect_ref, sdec_ref,
                  o_ref, proj_a_ref, proj_b_ref, xn_ref, y_ref, kt_ref, sg_ref, sr_ref, *,
                  n_chunks, n_tiles):
    s = pl.program_id(0)
    prev_tile = jnp.maximum(s - 1, 0)

    @pl.when(s == 0)
    def _():
        proj_b_ref[...] = jnp.zeros_like(proj_b_ref)

    @pl.when(prev_tile % n_tiles == 0)
    def _():
        sg_ref[...] = jnp.zeros_like(sg_ref)
        sr_ref[...] = jnp.zeros_like(sr_ref)

    xn_ref[...] = _rms(xnext_ref[0], g_ref[...]).astype(BF16)

    chunk_pass = functools.partial(
        _chunk_pass, xn_ref=xn_ref, rope_refs=(cosb_ref, sinb_ref, coso_ref, sino_ref),
        win_ref=win_ref, wg_ref=wg_ref, bg_ref=bg_ref, glan_ref=glan_ref, retn_ref=retn_ref,
        dmat_ref=dmat_ref, qdec_ref=qdec_ref, kdect_ref=kdect_ref, sdec_ref=sdec_ref, y_ref=y_ref,
        kt_ref=kt_ref, sg_ref=sg_ref, sr_ref=sr_ref, xres_ref=xres_ref, wout_ref=wout_ref,
        o_ref=o_ref, n_chunks=n_chunks)

    @pl.when(s % 2 == 0)
    def _():
        chunk_pass(proj_a_ref, proj_b_ref)

    @pl.when(s % 2 == 1)
    def _():
        chunk_pass(proj_b_ref, proj_a_ref)


def _chunk_pass(cur_ref, prv_ref, *, xn_ref, rope_refs, win_ref, wg_ref, bg_ref, glan_ref,
                retn_ref, dmat_ref, qdec_ref, kdect_ref, sdec_ref, y_ref, kt_ref, sg_ref, sr_ref,
                xres_ref, wout_ref, o_ref, n_chunks):
    ts = n_chunks * CHUNK
    chunk_rows = [slice(c * CHUNK, (c + 1) * CHUNK) for c in range(n_chunks)]
    head_rows = [slice(h * CHUNK, (h + 1) * CHUNK) for h in range(HEADS)]
    head_cols = [slice(h * DV, (h + 1) * DV) for h in range(HEADS)]
    pieces = [(i * PROJ_PIECE, min((i + 1) * PROJ_PIECE, D_PROJ))
              for i in range(-(-D_PROJ // PROJ_PIECE))]

    def emit_proj(n):
        for _ in range(min(n, len(pieces))):
            lo, hi = pieces.pop(0)
            cur_ref[:, lo:hi] = _dot(xn_ref[...], win_ref[:, lo:hi])

    bi = lax.broadcasted_iota(jnp.int32, (CUM_BLOCK, CUM_BLOCK), 0)
    bj = lax.broadcasted_iota(jnp.int32, (CUM_BLOCK, CUM_BLOCK), 1)
    tril_blocks = ((bi // CHUNK == bj // CHUNK) & (bj <= bi)).astype(BF16)
    si = lax.broadcasted_iota(jnp.int32, (HEADS * CHUNK, CHUNK), 0) % CHUNK
    sj = lax.broadcasted_iota(jnp.int32, (HEADS * CHUNK, CHUNK), 1)
    lower = sj <= si
    first_half = (lax.broadcasted_iota(jnp.int32, (ts, HK), 1) % DK) < (DK // 2)
    lane_pad = jnp.zeros((HK, CHUNK), BF16)

    def contributions(kdt, v):
        return jnp.concatenate([_dot(kdt[head_rows[h]], v[:, head_cols[h]]) for h in range(HEADS)],
                               axis=0)

    def scan_states(s_ref, decays, contribs):
        state, starts = s_ref[...], []
        for dec, con in zip(decays, contribs):
            starts.append(state.astype(BF16))
            state = state * dec + con
        s_ref[...] = state
        return starts

    z = _dot(prv_ref[:, C_LR:C_LR + GATE_RANK_PAD].astype(BF16), wg_ref[...]) + bg_ref[...]
    emit_proj(1)
    log_a = (jnp.minimum(z, 0.0) - jnp.log1p(jnp.exp(-jnp.abs(z)))) * (1.0 / GATE_TAU)
    la_hi = log_a.astype(BF16)
    la_lo = (log_a - la_hi.astype(F32)).astype(BF16)
    la_split = jnp.concatenate([la_hi, la_lo], axis=1)
    bcum = []
    for b in range(ts // CUM_BLOCK):
        r = _dot(tril_blocks, la_split[b * CUM_BLOCK:(b + 1) * CUM_BLOCK])
        bcum.append(r[:, :HK] + r[:, HK:])
    bcum = jnp.concatenate(bcum, axis=0)
    emit_proj(1)
    q = prv_ref[:, C_GQ:C_GQ + HK] * (DK ** -0.5)
    qeb = (q * jnp.exp(bcum)).astype(BF16)
    qebi = (q * jnp.exp(-bcum)).astype(BF16)

    g_decay, g_contrib = [], []
    for c, rows in enumerate(chunk_rows):
        kt = prv_ref[rows, C_GK:C_GK + HK].T
        bct = bcum[rows].T
        blt = bct[:, CHUNK - 1:CHUNK]
        kt_ref[0, c] = (kt * jnp.exp(bct)).astype(BF16)
        kt_ref[1, c] = (kt * jnp.exp(-bct)).astype(BF16)
        kdt = (kt * jnp.exp(blt - bct)).astype(BF16)
        g_decay.append(jnp.exp(blt))
        g_contrib.append(contributions(kdt, prv_ref[rows, C_GV:C_GV + HV].astype(BF16)))
        if c % 2 == 1:
            emit_proj(1)

    cosb, sinb, coso, sino = [r[...] for r in rope_refs]
    cosb, sinb = cosb[0], sinb[0]
    cs = cosb * coso - sinb * sino
    sn = sinb * coso + cosb * sino
    sn = jnp.where(first_half, -sn, sn)

    def rope(t):
        partner = jnp.where(first_half, pltpu.roll(t, HK - DK // 2, 1), pltpu.roll(t, DK // 2, 1))
        return t * cs + partner * sn

    qrb = rope(prv_ref[:, C_RQ:C_RQ + HK]).astype(BF16)
    kr = rope(prv_ref[:, C_RK:C_RK + HK]) * (DK ** -0.5)
    r_contrib = []
    for c, rows in enumerate(chunk_rows):
        kt = kr[rows].T
        kt_ref[2, c] = kt.astype(BF16)
        kdt = (kt * kdect_ref[...]).astype(BF16)
        r_contrib.append(contributions(kdt, prv_ref[rows, C_RV:C_RV + HV].astype(BF16)))
        if c % 2 == 1:
            emit_proj(1)

    g_start = scan_states(sg_ref, g_decay, g_contrib)
    r_start = scan_states(sr_ref, [sdec_ref[...]] * n_chunks, r_contrib)

    def gla_scores(c):
        rows = chunk_rows[c]
        r = _dot(_stack_heads(qeb[rows]), jnp.concatenate([g_start[c], kt_ref[1, c], lane_pad], axis=1))
        a_up = _dot(_stack_heads(qebi[rows]), kt_ref[0, c])
        return r, a_up

    def ret_scores(c):
        return _dot(_stack_heads(qrb[chunk_rows[c]]),
                    jnp.concatenate([r_start[c], kt_ref[2, c], lane_pad], axis=1))

    def gla_values(c, r, a_up):
        rows = chunk_rows[c]
        attn = jnp.where(lower, r[:, DV:DV + CHUNK], a_up).astype(BF16)
        v = prv_ref[rows, C_GV:C_GV + HV].astype(BF16)
        for h in range(HEADS):
            o = _dot(attn[head_rows[h]], v[:, head_cols[h]]) + r[head_rows[h], :DV]
            gate = prv_ref[rows, C_GG + h * DV:C_GG + (h + 1) * DV]
            yn = o * lax.rsqrt(jnp.mean(o * o, axis=-1, keepdims=True) + EPS) * glan_ref[h:h + 1, :]
            y_ref[rows, h * DV:(h + 1) * DV] = (yn * _swish(gate)).astype(BF16)

    def ret_values(c, r):
        rows = chunk_rows[c]
        inter = r[:, :DV] * qdec_ref[...]
        scores = (r[:, DV:DV + CHUNK] * dmat_ref[...]).astype(BF16)
        v = prv_ref[rows, C_RV:C_RV + HV].astype(BF16)
        for h in range(HEADS):
            o = _dot(scores[head_rows[h]], v[:, head_cols[h]]) + inter[head_rows[h]]
            gate = prv_ref[rows, C_RG + h * DV:C_RG + (h + 1) * DV]
            d = o - jnp.mean(o, axis=-1, keepdims=True)
            yn = d * lax.rsqrt(jnp.mean(d * d, axis=-1, keepdims=True) + EPS) * retn_ref[h:h + 1, :]
            y_ref[rows, HV + h * DV:HV + (h + 1) * DV] = (yn * _swish(gate)).astype(BF16)

    def out_proj(done):
        o_ref[0, done, :] = xres_ref[0, done, :] + _dot(y_ref[done, :], wout_ref[...])

    finished = None
    ahead = (gla_scores(0), ret_scores(0))
    for c in range(n_chunks):
        (g_r, g_up), r_r = ahead
        if c + 1 < n_chunks:
            ahead = (gla_scores(c + 1), ret_scores(c + 1))
        if finished is not None:
            out_proj(finished)
            finished = None
        else:
            emit_proj(1)
        gla_values(c, g_r, g_up)
        ret_values(c, r_r)
        if (c + 1) % (OUT_ROWS // CHUNK) == 0:
            finished = slice((c + 1) * CHUNK - OUT_ROWS, (c + 1) * CHUNK)
    emit_proj(len(pieces))
    out_proj(finished)


def _xattn_mlp_kernel(h_ref, wqk_ref, wvo_ref, gx_ref, gf_ref, w1_ref, w2_ref, gfin_ref, o_ref):
    blocks = [slice(r, r + XATTN_ROWS) for r in range(0, h_ref.shape[1], XATTN_ROWS)]
    n_mem = wqk_ref.shape[2] // XATTN_HEADS
    h = [h_ref[0, rows, :] for rows in blocks]
    scores = [_dot(_rms(hb, gx_ref[...]).astype(BF16), wqk_ref[0]) for hb in h]
    probs = []
    for sb in scores:
        heads = []
        for a in range(XATTN_HEADS):
            s = sb[:, a * n_mem:(a + 1) * n_mem]
            e = jnp.exp(s - jnp.max(s, axis=-1, keepdims=True))
            heads.append((e * (1.0 / jnp.sum(e, axis=-1, keepdims=True))).astype(BF16))
        probs.append(jnp.concatenate(heads, axis=1))
    h = [hb + _dot(pb, wvo_ref[0]) for hb, pb in zip(h, probs)]

    hn = [_rms(hb, gf_ref[...]) for hb in h]
    for c in range(D_FF // FF_BLOCK):
        cols = slice(c * FF_BLOCK, (c + 1) * FF_BLOCK)
        u = [jnp.maximum(_dot(hb, w1_ref[:, cols]), 0.0) for hb in hn]
        h = [hb + _dot(ub * ub, w2_ref[cols, :]) for hb, ub in zip(h, u)]
    for rows, hb in zip(blocks, h):
        o_ref[0, rows, :] = _rms(hb, gfin_ref[...])


def _const_spec(shape):
    n = len(shape)
    return pl.BlockSpec(shape, lambda *_: (0,) * n, pipeline_mode=pl.Buffered(1))


def _retention_tables():
    gamma = 1.0 - np.exp2(-np.linspace(RET_DECAY_EXP_LO, RET_DECAY_EXP_HI, HEADS, dtype=np.float32))
    log_g = np.log(gamma.astype(np.float32)).astype(np.float32)
    pos = np.arange(CHUNK, dtype=np.float32)
    dmat = np.exp(np.abs(pos[:, None] - pos[None, :])[None] * log_g[:, None, None])
    k_dec = np.exp((CHUNK - 1.0 - pos)[None, :] * log_g[:, None])
    q_dec = np.exp((pos + 1.0)[None, :] * log_g[:, None])
    s_dec = np.exp(CHUNK * log_g)
    dmat_s = dmat.reshape(HEADS * CHUNK, CHUNK)
    qdec_s = np.broadcast_to(q_dec.reshape(HEADS * CHUNK, 1), (HEADS * CHUNK, DV))
    kdect_s = np.repeat(k_dec, DK, axis=0)
    sdec_s = np.broadcast_to(np.repeat(s_dec, DK)[:, None], (HEADS * DK, DV))
    as32 = lambda a: jnp.asarray(np.ascontiguousarray(a), dtype=F32)
    return as32(dmat_s), as32(qdec_s), as32(kdect_s), as32(sdec_s)


def kernel(x, mem, norm_mix_g, w_in, gla_w_gate, gla_b_gate, gla_norm_g, ret_norm_g, w_out, norm_x_g,
           norm_mem_g, xa_w_q, xa_w_kv, xa_w_o, norm_ffn_g, ffn_w1, ffn_w2, final_norm_g):
    B, S, D = x.shape
    n_mem = mem.shape[1]
    assert D == D_MODEL and S % SEQ_TILE == 0 and S % XATTN_TILE == 0 and w_in.shape[0] == 1
    ts = SEQ_TILE
    n_tiles = S // ts

    inv_freq = ROPE_BASE ** (-jnp.arange(0, DK, 2, dtype=F32) / DK)
    inv_lane = jnp.tile(inv_freq, 2 * HEADS)
    ang_off = jnp.arange(ts, dtype=F32)[:, None] * inv_lane[None, :]
    ang_base = (jnp.arange(n_tiles, dtype=F32) * ts)[:, None, None] * inv_lane[None, None, :]
    cos_off, sin_off = jnp.cos(ang_off), jnp.sin(ang_off)
    cos_base, sin_base = jnp.cos(ang_base), jnp.sin(ang_base)

    w_in_r = pl.pallas_call(
        _prep_w_in_kernel,
        grid=(D // PREP_ROWS,),
        in_specs=[pl.BlockSpec((w_in.shape[2], PREP_ROWS), lambda i: (0, i))],
        out_specs=pl.BlockSpec((PREP_ROWS, D_PROJ), lambda i: (i, 0)),
        out_shape=jax.ShapeDtypeStruct((D, D_PROJ), BF16),
        compiler_params=pltpu.CompilerParams(dimension_semantics=("arbitrary",),
                                             vmem_limit_bytes=VMEM_LIMIT_BYTES),
        name="prep_w_in",
    )(jnp.swapaxes(w_in[0], 0, 1))
    w_gate_p = jnp.concatenate(
        [gla_w_gate[0], jnp.zeros((GATE_RANK_PAD - GATE_RANK, HK), F32)], axis=0).astype(BF16)
    dmat_s, qdec_s, kdect_s, sdec_s = _retention_tables()

    params = pltpu.CompilerParams(dimension_semantics=("arbitrary", "arbitrary"),
                                  vmem_limit_bytes=VMEM_LIMIT_BYTES)

    hm = XATTN_HEADS * n_mem
    wqk, wvo = pl.pallas_call(
        _mem_kv_kernel,
        grid=(B,),
        in_specs=[pl.BlockSpec((1, n_mem, D), lambda b: (b, 0, 0)),
                  _const_spec((1, D)), _const_spec((D, 2 * D)), _const_spec((D, D)),
                  _const_spec((D, D))],
        out_specs=[pl.BlockSpec((1, D, hm), lambda b: (b, 0, 0)),
                   pl.BlockSpec((1, hm, D), lambda b: (b, 0, 0))],
        out_shape=[jax.ShapeDtypeStruct((B, D, hm), BF16), jax.ShapeDtypeStruct((B, hm, D), BF16)],
        compiler_params=pltpu.CompilerParams(dimension_semantics=("arbitrary",),
                                             vmem_limit_bytes=VMEM_LIMIT_BYTES),
        name="mem_kv",
    )(mem, norm_mem_g[0][None, :], xa_w_kv[0], xa_w_q[0], xa_w_o[0])

    n_steps = B * n_tiles + 1

    def next_tile(s):
        p = jnp.minimum(s, n_steps - 2)
        return (p // n_tiles, p % n_tiles, 0)

    def prev_tile(s):
        q = jnp.maximum(s - 1, 0)
        return (q // n_tiles, q % n_tiles, 0)

    def prev_rope(s):
        return (jnp.maximum(s - 1, 0) % n_tiles, 0, 0)

    h1 = pl.pallas_call(
        functools.partial(_mixer_kernel, n_chunks=ts // CHUNK, n_tiles=n_tiles),
        grid=(n_steps,),
        in_specs=[pl.BlockSpec((1, ts, D), next_tile),
                  pl.BlockSpec((1, ts, D), prev_tile),
                  pl.BlockSpec((1, 1, HK), prev_rope), pl.BlockSpec((1, 1, HK), prev_rope),
                  _const_spec((ts, HK)), _const_spec((ts, HK)),
                  _const_spec((1, D)), _const_spec((D, D_PROJ)),
                  _const_spec((GATE_RANK_PAD, HK)), _const_spec((1, HK)),
                  _const_spec((HEADS, DV)), _const_spec((HEADS, DV)),
                  _const_spec((D, D)),
                  _const_spec((HEADS * CHUNK, CHUNK)), _const_spec((HEADS * CHUNK, DV)),
                  _const_spec((HEADS * DK, CHUNK)), _const_spec((HEADS * DK, DV))],
        out_specs=pl.BlockSpec((1, ts, D), prev_tile),
        out_shape=jax.ShapeDtypeStruct((B, S, D), F32),
        scratch_shapes=[pltpu.VMEM((ts, D_PROJ), F32), pltpu.VMEM((ts, D_PROJ), F32),
                        pltpu.VMEM((ts, D), BF16), pltpu.VMEM((ts, D), BF16),
                        pltpu.VMEM((3, ts // CHUNK, HK, CHUNK), BF16),
                        pltpu.VMEM((HEADS * DK, DV), F32), pltpu.VMEM((HEADS * DK, DV), F32)],
        compiler_params=pltpu.CompilerParams(dimension_semantics=("arbitrary",),
                                             vmem_limit_bytes=VMEM_LIMIT_BYTES),
        name="mixer",
    )(x, x, cos_base, sin_base, cos_off, sin_off, norm_mix_g[0][None, :], w_in_r, w_gate_p,
      gla_b_gate[0][None, :], gla_norm_g[0], ret_norm_g[0], w_out[0].astype(BF16),
      dmat_s, qdec_s, kdect_s, sdec_s)

    out = pl.pallas_call(
        _xattn_mlp_kernel,
        grid=(B, S // XATTN_TILE),
        in_specs=[pl.BlockSpec((1, XATTN_TILE, D), lambda b, t: (b, t, 0)),
                  pl.BlockSpec((1, D, hm), lambda b, t: (b, 0, 0), pipeline_mode=pl.Buffered(1)),
                  pl.BlockSpec((1, hm, D), lambda b, t: (b, 0, 0), pipeline_mode=pl.Buffered(1)),
                  _const_spec((1, D)),
                  _const_spec((1, D)), _const_spec((D, D_FF)), _const_spec((D_FF, D)),
                  _const_spec((1, D))],
        out_specs=pl.BlockSpec((1, XATTN_TILE, D), lambda b, t: (b, t, 0)),
        out_shape=jax.ShapeDtypeStruct((B, S, D), F32),
        compiler_params=params,
        name="xattn_mlp",
    )(h1, wqk, wvo, norm_x_g[0][None, :], norm_ffn_g[0][None, :], ffn_w1[0], ffn_w2[0],
      final_norm_g[None, :])
    return out
```

```python
import functools

import numpy as np
import jax
import jax.numpy as jnp
from jax import lax
from jax.experimental import pallas as pl
from jax.experimental.pallas import tpu as pltpu

D_MODEL = 1024
CHUNK = 64
HEADS = 4
DK = 64
DV = 128
GATE_RANK = 16
GATE_RANK_PAD = 128
GATE_TAU = 16.0
RET_DECAY_EXP_LO = 5.0
RET_DECAY_EXP_HI = 12.0
ROPE_BASE = 10000.0
XATTN_HEADS = 4
XATTN_DH = D_MODEL // XATTN_HEADS
D_FF = 4 * D_MODEL
FF_BLOCK = 1024
XATTN_TILE = 512
XATTN_ROWS = 256
PREP_ROWS = 256
EPS = 1e-6

HK = HEADS * DK
HV = HEADS * DV
C_GQ, C_GK, C_GV, C_GG = 0, HK, 2 * HK, 2 * HK + HV
C_RQ = C_GG + HV
C_RK = C_RQ + HK
C_RV = C_RK + HK
C_RG = C_RV + HV
C_LR = C_RG + HV
D_PROJ = C_LR + GATE_RANK_PAD

SEQ_TILE = 512
PROJ_PIECE = 256
CUM_BLOCK = 256
OUT_ROWS = 256
VMEM_LIMIT_BYTES = 56 * 1024 * 1024

F32 = jnp.float32
BF16 = jnp.bfloat16


def _dot(a, b):
    return jnp.dot(a, b, preferred_element_type=F32)


def _dot_nt(a, b):
    return lax.dot_general(a, b, (((1,), (1,)), ((), ())), preferred_element_type=F32)


def _rms(x, g):
    return x * lax.rsqrt(jnp.mean(x * x, axis=-1, keepdims=True) + EPS) * g


def _swish(g):
    return g / (1.0 + jnp.exp(-g))


def _stack_heads(t):
    lane_head = lax.broadcasted_iota(jnp.int32, t.shape, 1) // DK
    zero = jnp.zeros_like(t)
    return jnp.concatenate([jnp.where(lane_head == h, t, zero) for h in range(HEADS)], axis=0)


def _prep_w_in_kernel(wt_ref, o_ref):
    lr0 = 2 * HK + HV
    o_ref[:, :lr0] = wt_ref[:lr0, :].T.astype(BF16)
    o_ref[:, lr0:C_LR] = wt_ref[lr0 + GATE_RANK:, :].T.astype(BF16)
    tail = wt_ref[lr0:lr0 + GATE_RANK_PAD, :]
    row = lax.broadcasted_iota(jnp.int32, tail.shape, 0)
    o_ref[:, C_LR:] = jnp.where(row < GATE_RANK, tail, 0.0).T.astype(BF16)


def _mem_kv_kernel(mem_ref, g_ref, wkv_ref, wq_ref, wo_ref, wqk_ref, wvo_ref):
    n_mem = mem_ref.shape[1]
    mn = _rms(mem_ref[0], g_ref[...])
    kv = _dot(mn, wkv_ref[...])
    for h in range(XATTN_HEADS):
        cols = slice(h * XATTN_DH, (h + 1) * XATTN_DH)
        mcols = slice(h * n_mem, (h + 1) * n_mem)
        k_h = kv[:, cols]
        v_h = kv[:, D_MODEL + h * XATTN_DH:D_MODEL + (h + 1) * XATTN_DH]
        wqk_ref[0, :, mcols] = (_dot_nt(wq_ref[:, cols], k_h) * (XATTN_DH ** -0.5)).astype(BF16)
        wvo_ref[0, mcols, :] = _dot(v_h, wo_ref[cols, :]).astype(BF16)


def _mixer_kernel(xnext_ref, xres_ref, cosb_ref, sinb_ref, coso_ref, sino_ref, g_ref, win_ref, wg_ref,
                  bg_ref, glan_ref, retn_ref, wout_ref, dmat_ref, qdec_ref, kdect_ref, sdec_ref,
                  o_ref, proj_a_ref, proj_b_ref, xn_ref, y_ref, kt_ref, sg_ref, sr_ref, *,
                  n_chunks, n_tiles):
    s = pl.program_id(0)
    prev_tile = jnp.maximum(s - 1, 0)

    @pl.when(s == 0)
    def _():
        proj_b_ref[...] = jnp.zeros_like(proj_b_ref)

    @pl.when(prev_tile % n_tiles == 0)
    def _():
        sg_ref[...] = jnp.zeros_like(sg_ref)
        sr_ref[...] = jnp.zeros_like(sr_ref)

    chunk_pass = functools.partial(
        _chunk_pass, xnext_ref=xnext_ref, g_ref=g_ref, xn_ref=xn_ref,
        rope_refs=(cosb_ref, sinb_ref, coso_ref, sino_ref),
        win_ref=win_ref, wg_ref=wg_ref, bg_ref=bg_ref, glan_ref=glan_ref, retn_ref=retn_ref,
        dmat_ref=dmat_ref, qdec_ref=qdec_ref, kdect_ref=kdect_ref, sdec_ref=sdec_ref, y_ref=y_ref,
        kt_ref=kt_ref, sg_ref=sg_ref, sr_ref=sr_ref, xres_ref=xres_ref, wout_ref=wout_ref,
        o_ref=o_ref, n_chunks=n_chunks)

    @pl.when(s % 2 == 0)
    def _():
        chunk_pass(proj_a_ref, proj_b_ref)

    @pl.when(s % 2 == 1)
    def _():
        chunk_pass(proj_b_ref, proj_a_ref)


def _chunk_pass(cur_ref, prv_ref, *, xnext_ref, g_ref, xn_ref, rope_refs, win_ref, wg_ref, bg_ref, glan_ref,
                retn_ref, dmat_ref, qdec_ref, kdect_ref, sdec_ref, y_ref, kt_ref, sg_ref, sr_ref,
                xres_ref, wout_ref, o_ref, n_chunks):
    ts = n_chunks * CHUNK
    chunk_rows = [slice(c * CHUNK, (c + 1) * CHUNK) for c in range(n_chunks)]
    head_rows = [slice(h * CHUNK, (h + 1) * CHUNK) for h in range(HEADS)]
    head_cols = [slice(h * DV, (h + 1) * DV) for h in range(HEADS)]
    pieces = [(i * PROJ_PIECE, min((i + 1) * PROJ_PIECE, D_PROJ))
              for i in range(-(-D_PROJ // PROJ_PIECE))]

    def emit_proj(n):
        for _ in range(min(n, len(pieces))):
            lo, hi = pieces.pop(0)
            cur_ref[:, lo:hi] = _dot(xn_ref[...], win_ref[:, lo:hi])

    bi = lax.broadcasted_iota(jnp.int32, (CUM_BLOCK, CUM_BLOCK), 0)
    bj = lax.broadcasted_iota(jnp.int32, (CUM_BLOCK, CUM_BLOCK), 1)
    tril_blocks = ((bi // CHUNK == bj // CHUNK) & (bj <= bi)).astype(BF16)
    si = lax.broadcasted_iota(jnp.int32, (HEADS * CHUNK, CHUNK), 0) % CHUNK
    sj = lax.broadcasted_iota(jnp.int32, (HEADS * CHUNK, CHUNK), 1)
    lower = sj <= si
    first_half = (lax.broadcasted_iota(jnp.int32, (ts, HK), 1) % DK) < (DK // 2)
    lane_pad = jnp.zeros((HK, CHUNK), BF16)

    def contributions(kdt, v):
        return jnp.concatenate([_dot(kdt[head_rows[h]], v[:, head_cols[h]]) for h in range(HEADS)],
                               axis=0)

    def scan_states(s_ref, decays, contribs):
        state, starts = s_ref[...], []
        for dec, con in zip(decays, contribs):
            starts.append(state.astype(BF16))
            state = state * dec + con
        s_ref[...] = state
        return starts

    xn_ref[...] = _rms(xnext_ref[0], g_ref[...]).astype(BF16)
    z = _dot(prv_ref[:, C_LR:C_LR + GATE_RANK_PAD].astype(BF16), wg_ref[...]) + bg_ref[...]
    emit_proj(1)
    log_a = (jnp.minimum(z, 0.0) - jnp.log1p(jnp.exp(-jnp.abs(z)))) * (1.0 / GATE_TAU)
    la_hi = log_a.astype(BF16)
    la_lo = (log_a - la_hi.astype(F32)).astype(BF16)
    la_split = jnp.concatenate([la_hi, la_lo], axis=1)
    bcum = []
    for b in range(ts // CUM_BLOCK):
        r = _dot(tril_blocks, la_split[b * CUM_BLOCK:(b + 1) * CUM_BLOCK])
        bcum.append(r[:, :HK] + r[:, HK:])
    bcum = jnp.concatenate(bcum, axis=0)
    emit_proj(1)
    q = prv_ref[:, C_GQ:C_GQ + HK] * (DK ** -0.5)
    qeb = (q * jnp.exp(bcum)).astype(BF16)
    qebi = (q * jnp.exp(-bcum)).astype(BF16)

    g_decay, g_contrib = [], []
    for c, rows in enumerate(chunk_rows):
        kt = prv_ref[rows, C_GK:C_GK + HK].T
        bct = bcum[rows].T
        blt = bct[:, CHUNK - 1:CHUNK]
        kt_ref[0, c] = (kt * jnp.exp(bct)).astype(BF16)
        kt_ref[1, c] = (kt * jnp.exp(-bct)).astype(BF16)
        kdt = (kt * jnp.exp(blt - bct)).astype(BF16)
        g_decay.append(jnp.exp(blt))
        g_contrib.append(contributions(kdt, prv_ref[rows, C_GV:C_GV + HV].astype(BF16)))
        if c % 2 == 1:
            emit_proj(1)

    cosb, sinb, coso, sino = [r[...] for r in rope_refs]
    cosb, sinb = cosb[0], sinb[0]
    cs = cosb * coso - sinb * sino
    sn = sinb * coso + cosb * sino
    sn = jnp.where(first_half, -sn, sn)

    def rope(t):
        partner = jnp.where(first_half, pltpu.roll(t, HK - DK // 2, 1), pltpu.roll(t, DK // 2, 1))
        return t * cs + partner * sn

    qrb = rope(prv_ref[:, C_RQ:C_RQ + HK]).astype(BF16)
    kr = rope(prv_ref[:, C_RK:C_RK + HK]) * (DK ** -0.5)
    r_contrib = []
    for c, rows in enumerate(chunk_rows):
        kt = kr[rows].T
        kt_ref[2, c] = kt.astype(BF16)
        kdt = (kt * kdect_ref[...]).astype(BF16)
        r_contrib.append(contributions(kdt, prv_ref[rows, C_RV:C_RV + HV].astype(BF16)))
        if c % 2 == 1:
            emit_proj(1)

    g_start = scan_states(sg_ref, g_decay, g_contrib)
    r_start = scan_states(sr_ref, [sdec_ref[...]] * n_chunks, r_contrib)

    def gla_scores(c):
        rows = chunk_rows[c]
        r = _dot(_stack_heads(qeb[rows]), jnp.concatenate([g_start[c], kt_ref[1, c], lane_pad], axis=1))
        a_up = _dot(_stack_heads(qebi[rows]), kt_ref[0, c])
        return r, a_up

    def ret_scores(c):
        return _dot(_stack_heads(qrb[chunk_rows[c]]),
                    jnp.concatenate([r_start[c], kt_ref[2, c], lane_pad], axis=1))

    def gla_values(c, r, a_up):
        rows = chunk_rows[c]
        attn = jnp.where(lower, r[:, DV:DV + CHUNK], a_up).astype(BF16)
        v = prv_ref[rows, C_GV:C_GV + HV].astype(BF16)
        for h in range(HEADS):
            o = _dot(attn[head_rows[h]], v[:, head_cols[h]]) + r[head_rows[h], :DV]
            gate = prv_ref[rows, C_GG + h * DV:C_GG + (h + 1) * DV]
            yn = o * lax.rsqrt(jnp.mean(o * o, axis=-1, keepdims=True) + EPS) * glan_ref[h:h + 1, :]
            y_ref[rows, h * DV:(h + 1) * DV] = (yn * _swish(gate)).astype(BF16)

    def ret_values(c, r):
        rows = chunk_rows[c]
        inter = r[:, :DV] * qdec_ref[...]
        scores = (r[:, DV:DV + CHUNK] * dmat_ref[...]).astype(BF16)
        v = prv_ref[rows, C_RV:C_RV + HV].astype(BF16)
        for h in range(HEADS):
            o = _dot(scores[head_rows[h]], v[:, head_cols[h]]) + inter[head_rows[h]]
            gate = prv_ref[rows, C_RG + h * DV:C_RG + (h + 1) * DV]
            d = o - jnp.mean(o, axis=-1, keepdims=True)
            yn = d * lax.rsqrt(jnp.mean(d * d, axis=-1, keepdims=True) + EPS) * retn_ref[h:h + 1, :]
            y_ref[rows, HV + h * DV:HV + (h + 1) * DV] = (yn * _swish(gate)).astype(BF16)

    def out_proj(done):
        o_ref[0, done, :] = xres_ref[0, done, :] + _dot(y_ref[done, :], wout_ref[...])

    finished = None
    ahead = (gla_scores(0), ret_scores(0))
    for c in range(n_chunks):
        (g_r, g_up), r_r = ahead
        if c + 1 < n_chunks:
            ahead = (gla_scores(c + 1), ret_scores(c + 1))
        if finished is not None:
            out_proj(finished)
            finished = None
        else:
            emit_proj(1)
        gla_values(c, g_r, g_up)
        ret_values(c, r_r)
        if (c + 1) % (OUT_ROWS // CHUNK) == 0:
            finished = slice((c + 1) * CHUNK - OUT_ROWS, (c + 1) * CHUNK)
    emit_proj(len(pieces))
    out_proj(finished)


def _xattn_mlp_kernel(h_ref, wqk_ref, wvo_ref, gx_ref, gf_ref, w1_ref, w2_ref, gfin_ref, o_ref):
    blocks = [slice(r, r + XATTN_ROWS) for r in range(0, h_ref.shape[1], XATTN_ROWS)]
    n_mem = wqk_ref.shape[2] // XATTN_HEADS
    h = [h_ref[0, rows, :] for rows in blocks]
    scores = [_dot(_rms(hb, gx_ref[...]).astype(BF16), wqk_ref[0]) for hb in h]
    probs = []
    for sb in scores:
        heads = []
        for a in range(XATTN_HEADS):
            s = sb[:, a * n_mem:(a + 1) * n_mem]
            e = jnp.exp(s - jnp.max(s, axis=-1, keepdims=True))
            heads.append((e * (1.0 / jnp.sum(e, axis=-1, keepdims=True))).astype(BF16))
        probs.append(jnp.concatenate(heads, axis=1))
    h = [hb + _dot(pb, wvo_ref[0]) for hb, pb in zip(h, probs)]

    hn = [_rms(hb, gf_ref[...]).astype(BF16) for hb in h]
    for c in range(D_FF // FF_BLOCK):
        cols = slice(c * FF_BLOCK, (c + 1) * FF_BLOCK)
        u = [jnp.maximum(_dot(hb, w1_ref[:, cols]), 0.0) for hb in hn]
        h = [hb + _dot((ub * ub).astype(BF16), w2_ref[cols, :]) for hb, ub in zip(h, u)]
    for rows, hb in zip(blocks, h):
        o_ref[0, rows, :] = _rms(hb, gfin_ref[...])


def _const_spec(shape):
    n = len(shape)
    return pl.BlockSpec(shape, lambda *_: (0,) * n, pipeline_mode=pl.Buffered(1))


def _retention_tables():
    gamma = 1.0 - np.exp2(-np.linspace(RET_DECAY_EXP_LO, RET_DECAY_EXP_HI, HEADS, dtype=np.float32))
    log_g = np.log(gamma.astype(np.float32)).astype(np.float32)
    pos = np.arange(CHUNK, dtype=np.float32)
    dmat = np.exp(np.abs(pos[:, None] - pos[None, :])[None] * log_g[:, None, None])
    k_dec = np.exp((CHUNK - 1.0 - pos)[None, :] * log_g[:, None])
    q_dec = np.exp((pos + 1.0)[None, :] * log_g[:, None])
    s_dec = np.exp(CHUNK * log_g)
    dmat_s = dmat.reshape(HEADS * CHUNK, CHUNK)
    qdec_s = np.broadcast_to(q_dec.reshape(HEADS * CHUNK, 1), (HEADS * CHUNK, DV))
    kdect_s = np.repeat(k_dec, DK, axis=0)
    sdec_s = np.broadcast_to(np.repeat(s_dec, DK)[:, None], (HEADS * DK, DV))
    as32 = lambda a: jnp.asarray(np.ascontiguousarray(a), dtype=F32)
    return as32(dmat_s), as32(qdec_s), as32(kdect_s), as32(sdec_s)


def kernel(x, mem, norm_mix_g, w_in, gla_w_gate, gla_b_gate, gla_norm_g, ret_norm_g, w_out, norm_x_g,
           norm_mem_g, xa_w_q, xa_w_kv, xa_w_o, norm_ffn_g, ffn_w1, ffn_w2, final_norm_g):
    B, S, D = x.shape
    n_mem = mem.shape[1]
    assert D == D_MODEL and S % SEQ_TILE == 0 and S % XATTN_TILE == 0 and w_in.shape[0] == 1
    ts = SEQ_TILE
    n_tiles = S // ts

    inv_freq = ROPE_BASE ** (-jnp.arange(0, DK, 2, dtype=F32) / DK)
    inv_lane = jnp.tile(inv_freq, 2 * HEADS)
    ang_off = jnp.arange(ts, dtype=F32)[:, None] * inv_lane[None, :]
    ang_base = (jnp.arange(n_tiles, dtype=F32) * ts)[:, None, None] * inv_lane[None, None, :]
    cos_off, sin_off = jnp.cos(ang_off), jnp.sin(ang_off)
    cos_base, sin_base = jnp.cos(ang_base), jnp.sin(ang_base)

    w_in_r = pl.pallas_call(
        _prep_w_in_kernel,
        grid=(D // PREP_ROWS,),
        in_specs=[pl.BlockSpec((w_in.shape[2], PREP_ROWS), lambda i: (0, i))],
        out_specs=pl.BlockSpec((PREP_ROWS, D_PROJ), lambda i: (i, 0)),
        out_shape=jax.ShapeDtypeStruct((D, D_PROJ), BF16),
        compiler_params=pltpu.CompilerParams(dimension_semantics=("arbitrary",),
                                             vmem_limit_bytes=VMEM_LIMIT_BYTES),
        name="prep_w_in",
    )(jnp.swapaxes(w_in[0], 0, 1))
    w_gate_p = jnp.concatenate(
        [gla_w_gate[0], jnp.zeros((GATE_RANK_PAD - GATE_RANK, HK), F32)], axis=0).astype(BF16)
    dmat_s, qdec_s, kdect_s, sdec_s = _retention_tables()

    params = pltpu.CompilerParams(dimension_semantics=("arbitrary", "arbitrary"),
                                  vmem_limit_bytes=VMEM_LIMIT_BYTES)

    hm = XATTN_HEADS * n_mem
    wqk, wvo = pl.pallas_call(
        _mem_kv_kernel,
        grid=(B,),
        in_specs=[pl.BlockSpec((1, n_mem, D), lambda b: (b, 0, 0)),
                  _const_spec((1, D)), _const_spec((D, 2 * D)), _const_spec((D, D)),
                  _const_spec((D, D))],
        out_specs=[pl.BlockSpec((1, D, hm), lambda b: (b, 0, 0)),
                   pl.BlockSpec((1, hm, D), lambda b: (b, 0, 0))],
        out_shape=[jax.ShapeDtypeStruct((B, D, hm), BF16), jax.ShapeDtypeStruct((B, hm, D), BF16)],
        compiler_params=pltpu.CompilerParams(dimension_semantics=("arbitrary",),
                                             vmem_limit_bytes=VMEM_LIMIT_BYTES),
        name="mem_kv",
    )(mem, norm_mem_g[0][None, :], xa_w_kv[0], xa_w_q[0], xa_w_o[0])

    n_steps = B * n_tiles + 1

    def next_tile(s):
        p = jnp.minimum(s, n_steps - 2)
        return (p // n_tiles, p % n_tiles, 0)

    def prev_tile(s):
        q = jnp.maximum(s - 1, 0)
        return (q // n_tiles, q % n_tiles, 0)

    def prev_rope(s):
        return (jnp.maximum(s - 1, 0) % n_tiles, 0, 0)

    h1 = pl.pallas_call(
        functools.partial(_mixer_kernel, n_chunks=ts // CHUNK, n_tiles=n_tiles),
        grid=(n_steps,),
        in_specs=[pl.BlockSpec((1, ts, D), next_tile),
                  pl.BlockSpec((1, ts, D), prev_tile),
                  pl.BlockSpec((1, 1, HK), prev_rope), pl.BlockSpec((1, 1, HK), prev_rope),
                  _const_spec((ts, HK)), _const_spec((ts, HK)),
                  _const_spec((1, D)), _const_spec((D, D_PROJ)),
                  _const_spec((GATE_RANK_PAD, HK)), _const_spec((1, HK)),
                  _const_spec((HEADS, DV)), _const_spec((HEADS, DV)),
                  _const_spec((D, D)),
                  _const_spec((HEADS * CHUNK, CHUNK)), _const_spec((HEADS * CHUNK, DV)),
                  _const_spec((HEADS * DK, CHUNK)), _const_spec((HEADS * DK, DV))],
        out_specs=pl.BlockSpec((1, ts, D), prev_tile),
        out_shape=jax.ShapeDtypeStruct((B, S, D), F32),
        scratch_shapes=[pltpu.VMEM((ts, D_PROJ), F32), pltpu.VMEM((ts, D_PROJ), F32),
                        pltpu.VMEM((ts, D), BF16), pltpu.VMEM((ts, D), BF16),
                        pltpu.VMEM((3, ts // CHUNK, HK, CHUNK), BF16),
                        pltpu.VMEM((HEADS * DK, DV), F32), pltpu.VMEM((HEADS * DK, DV), F32)],
        compiler_params=pltpu.CompilerParams(dimension_semantics=("arbitrary",),
                                             vmem_limit_bytes=VMEM_LIMIT_BYTES),
        name="mixer",
    )(x, x, cos_base, sin_base, cos_off, sin_off, norm_mix_g[0][None, :], w_in_r, w_gate_p,
      gla_b_gate[0][None, :], gla_norm_g[0], ret_norm_g[0], w_out[0].astype(BF16),
      dmat_s, qdec_s, kdect_s, sdec_s)

    out = pl.pallas_call(
        _xattn_mlp_kernel,
        grid=(B, S // XATTN_TILE),
        in_specs=[pl.BlockSpec((1, XATTN_TILE, D), lambda b, t: (b, t, 0)),
                  pl.BlockSpec((1, D, hm), lambda b, t: (b, 0, 0)),
                  pl.BlockSpec((1, hm, D), lambda b, t: (b, 0, 0)),
                  _const_spec((1, D)),
                  _const_spec((1, D)), _const_spec((D, D_FF)), _const_spec((D_FF, D)),
                  _const_spec((1, D))],
        out_specs=pl.BlockSpec((1, XATTN_TILE, D), lambda b, t: (b, t, 0)),
        out_shape=jax.ShapeDtypeStruct((B, S, D), F32),
        compiler_params=params,
        name="xattn_mlp",
    )(h1, wqk, wvo, norm_x_g[0][None, :], norm_ffn_g[0][None, :], ffn_w1[0].astype(BF16),
      ffn_w2[0].astype(BF16), final_norm_g[None, :])
    return out
```

```python
import functools

import numpy as np
import jax
import jax.numpy as jnp
from jax import lax
from jax.experimental import pallas as pl
from jax.experimental.pallas import tpu as pltpu

D_MODEL = 1024
CHUNK = 64
HEADS = 4
DK = 64
DV = 128
GATE_RANK = 16
GATE_RANK_PAD = 128
GATE_TAU = 16.0
RET_DECAY_EXP_LO = 5.0
RET_DECAY_EXP_HI = 12.0
ROPE_BASE = 10000.0
XATTN_HEADS = 4
XATTN_DH = D_MODEL // XATTN_HEADS
D_FF = 4 * D_MODEL
FF_BLOCK = 1024
XATTN_TILE = 512
XATTN_ROWS = 256
PREP_ROWS = 256
EPS = 1e-6

HK = HEADS * DK
HV = HEADS * DV
C_GQ, C_GK, C_GV, C_GG = 0, HK, 2 * HK, 2 * HK + HV
C_RQ = C_GG + HV
C_RK = C_RQ + HK
C_RV = C_RK + HK
C_RG = C_RV + HV
C_LR = C_RG + HV
D_PROJ = C_LR + GATE_RANK_PAD

SEQ_TILE = 512
PROJ_PIECE = 256
STAGE_A_PIECES = 7
CUM_BLOCK = 256
OUT_ROWS = 256
VMEM_LIMIT_BYTES = 56 * 1024 * 1024

F32 = jnp.float32
BF16 = jnp.bfloat16


def _dot(a, b):
    return jnp.dot(a, b, preferred_element_type=F32)


def _dot_nt(a, b):
    return lax.dot_general(a, b, (((1,), (1,)), ((), ())), preferred_element_type=F32)


def _rms(x, g):
    return x * lax.rsqrt(jnp.mean(x * x, axis=-1, keepdims=True) + EPS) * g


def _swish(g):
    return g / (1.0 + jnp.exp(-g))


def _stack_heads(t):
    lane_head = lax.broadcasted_iota(jnp.int32, t.shape, 1) // DK
    zero = jnp.zeros_like(t)
    return jnp.concatenate([jnp.where(lane_head == h, t, zero) for h in range(HEADS)], axis=0)


def _prep_w_in_kernel(wt_ref, o_ref):
    lr0 = 2 * HK + HV
    o_ref[:, :lr0] = wt_ref[:lr0, :].T.astype(BF16)
    o_ref[:, lr0:C_LR] = wt_ref[lr0 + GATE_RANK:, :].T.astype(BF16)
    tail = wt_ref[lr0:lr0 + GATE_RANK_PAD, :]
    row = lax.broadcasted_iota(jnp.int32, tail.shape, 0)
    o_ref[:, C_LR:] = jnp.where(row < GATE_RANK, tail, 0.0).T.astype(BF16)


def _mem_kv_kernel(mem_ref, g_ref, wkv_ref, wq_ref, wo_ref, wqk_ref, wvo_ref):
    n_mem = mem_ref.shape[1]
    mn = _rms(mem_ref[0], g_ref[...])
    kv = _dot(mn, wkv_ref[...])
    for h in range(XATTN_HEADS):
        cols = slice(h * XATTN_DH, (h + 1) * XATTN_DH)
        mcols = slice(h * n_mem, (h + 1) * n_mem)
        k_h = kv[:, cols]
        v_h = kv[:, D_MODEL + h * XATTN_DH:D_MODEL + (h + 1) * XATTN_DH]
        wqk_ref[0, :, mcols] = (_dot_nt(wq_ref[:, cols], k_h) * (XATTN_DH ** -0.5)).astype(BF16)
        wvo_ref[0, mcols, :] = _dot(v_h, wo_ref[cols, :]).astype(BF16)


def _mixer_kernel(xnext_ref, xres_ref, cosb_ref, sinb_ref, coso_ref, sino_ref, g_ref, win_ref, wg_ref,
                  bg_ref, glan_ref, retn_ref, wout_ref, dmat_ref, qdec_ref, kdect_ref, sdec_ref,
                  o_ref, proj_a_ref, proj_b_ref, xn_ref, y_ref, kt_ref, sg_ref, sr_ref, *,
                  n_chunks, n_tiles):
    s = pl.program_id(0)
    prev_tile = jnp.maximum(s - 1, 0)

    @pl.when(s == 0)
    def _():
        proj_b_ref[...] = jnp.zeros_like(proj_b_ref)

    @pl.when(prev_tile % n_tiles == 0)
    def _():
        sg_ref[...] = jnp.zeros_like(sg_ref)
        sr_ref[...] = jnp.zeros_like(sr_ref)

    chunk_pass = functools.partial(
        _chunk_pass, xnext_ref=xnext_ref, g_ref=g_ref, xn_ref=xn_ref,
        rope_refs=(cosb_ref, sinb_ref, coso_ref, sino_ref),
        win_ref=win_ref, wg_ref=wg_ref, bg_ref=bg_ref, glan_ref=glan_ref, retn_ref=retn_ref,
        dmat_ref=dmat_ref, qdec_ref=qdec_ref, kdect_ref=kdect_ref, sdec_ref=sdec_ref, y_ref=y_ref,
        kt_ref=kt_ref, sg_ref=sg_ref, sr_ref=sr_ref, xres_ref=xres_ref, wout_ref=wout_ref,
        o_ref=o_ref, n_chunks=n_chunks)

    @pl.when(s % 2 == 0)
    def _():
        chunk_pass(proj_a_ref, proj_b_ref)

    @pl.when(s % 2 == 1)
    def _():
        chunk_pass(proj_b_ref, proj_a_ref)


def _chunk_pass(cur_ref, prv_ref, *, xnext_ref, g_ref, xn_ref, rope_refs, win_ref, wg_ref, bg_ref, glan_ref,
                retn_ref, dmat_ref, qdec_ref, kdect_ref, sdec_ref, y_ref, kt_ref, sg_ref, sr_ref,
                xres_ref, wout_ref, o_ref, n_chunks):
    ts = n_chunks * CHUNK
    chunk_rows = [slice(c * CHUNK, (c + 1) * CHUNK) for c in range(n_chunks)]
    head_rows = [slice(h * CHUNK, (h + 1) * CHUNK) for h in range(HEADS)]
    head_cols = [slice(h * DV, (h + 1) * DV) for h in range(HEADS)]
    pieces = [(i * PROJ_PIECE, min((i + 1) * PROJ_PIECE, D_PROJ))
              for i in range(-(-D_PROJ // PROJ_PIECE))]

    def emit_proj(n):
        for _ in range(min(n, len(pieces))):
            lo, hi = pieces.pop(0)
            cur_ref[:, lo:hi] = _dot(xn_ref[...], win_ref[:, lo:hi])

    bi = lax.broadcasted_iota(jnp.int32, (CUM_BLOCK, CUM_BLOCK), 0)
    bj = lax.broadcasted_iota(jnp.int32, (CUM_BLOCK, CUM_BLOCK), 1)
    tril_blocks = ((bi // CHUNK == bj // CHUNK) & (bj <= bi)).astype(BF16)
    si = lax.broadcasted_iota(jnp.int32, (HEADS * CHUNK, CHUNK), 0) % CHUNK
    sj = lax.broadcasted_iota(jnp.int32, (HEADS * CHUNK, CHUNK), 1)
    lower = sj <= si
    first_half = (lax.broadcasted_iota(jnp.int32, (ts, HK), 1) % DK) < (DK // 2)
    lane_pad = jnp.zeros((HK, CHUNK), BF16)

    def contributions(kdt, v):
        return jnp.concatenate([_dot(kdt[head_rows[h]], v[:, head_cols[h]]) for h in range(HEADS)],
                               axis=0)

    def scan_states(s_ref, decays, contribs):
        state, starts = s_ref[...], []
        for dec, con in zip(decays, contribs):
            starts.append(state.astype(BF16))
            state = state * dec + con
        s_ref[...] = state
        return starts

    xn_ref[...] = _rms(xnext_ref[0], g_ref[...]).astype(BF16)
    z = _dot(prv_ref[:, C_LR:C_LR + GATE_RANK_PAD].astype(BF16), wg_ref[...]) + bg_ref[...]
    emit_proj(STAGE_A_PIECES)
    log_a = (jnp.minimum(z, 0.0) - jnp.log1p(jnp.exp(-jnp.abs(z)))) * (1.0 / GATE_TAU)
    la_hi = log_a.astype(BF16)
    la_lo = (log_a - la_hi.astype(F32)).astype(BF16)
    la_split = jnp.concatenate([la_hi, la_lo], axis=1)
    bcum = []
    for b in range(ts // CUM_BLOCK):
        r = _dot(tril_blocks, la_split[b * CUM_BLOCK:(b + 1) * CUM_BLOCK])
        bcum.append(r[:, :HK] + r[:, HK:])
    bcum = jnp.concatenate(bcum, axis=0)
    emit_proj(1)
    q = prv_ref[:, C_GQ:C_GQ + HK] * (DK ** -0.5)
    qeb = (q * jnp.exp(bcum)).astype(BF16)
    qebi = (q * jnp.exp(-bcum)).astype(BF16)

    g_decay, g_contrib = [], []
    for c, rows in enumerate(chunk_rows):
        kt = prv_ref[rows, C_GK:C_GK + HK].T
        bct = bcum[rows].T
        blt = bct[:, CHUNK - 1:CHUNK]
        kt_ref[0, c] = (kt * jnp.exp(bct)).astype(BF16)
        kt_ref[1, c] = (kt * jnp.exp(-bct)).astype(BF16)
        kdt = (kt * jnp.exp(blt - bct)).astype(BF16)
        g_decay.append(jnp.exp(blt))
        g_contrib.append(contributions(kdt, prv_ref[rows, C_GV:C_GV + HV].astype(BF16)))
        if c % 2 == 1:
            emit_proj(1)

    cosb, sinb, coso, sino = [r[...] for r in rope_refs]
    cosb, sinb = cosb[0], sinb[0]
    cs = cosb * coso - sinb * sino
    sn = sinb * coso + cosb * sino
    sn = jnp.where(first_half, -sn, sn)

    def rope(t):
        partner = jnp.where(first_half, pltpu.roll(t, HK - DK // 2, 1), pltpu.roll(t, DK // 2, 1))
        return t * cs + partner * sn

    qrb = rope(prv_ref[:, C_RQ:C_RQ + HK]).astype(BF16)
    kr = rope(prv_ref[:, C_RK:C_RK + HK]) * (DK ** -0.5)
    r_contrib = []
    for c, rows in enumerate(chunk_rows):
        kt = kr[rows].T
        kt_ref[2, c] = kt.astype(BF16)
        kdt = (kt * kdect_ref[...]).astype(BF16)
        r_contrib.append(contributions(kdt, prv_ref[rows, C_RV:C_RV + HV].astype(BF16)))
        if c % 2 == 1:
            emit_proj(1)

    g_start = scan_states(sg_ref, g_decay, g_contrib)
    r_start = scan_states(sr_ref, [sdec_ref[...]] * n_chunks, r_contrib)

    def gla_scores(c):
        rows = chunk_rows[c]
        r = _dot(_stack_heads(qeb[rows]), jnp.concatenate([g_start[c], kt_ref[1, c], lane_pad], axis=1))
        a_up = _dot(_stack_heads(qebi[rows]), kt_ref[0, c])
        return r, a_up

    def ret_scores(c):
        return _dot(_stack_heads(qrb[chunk_rows[c]]),
                    jnp.concatenate([r_start[c], kt_ref[2, c], lane_pad], axis=1))

    def gla_values(c, r, a_up):
        rows = chunk_rows[c]
        attn = jnp.where(lower, r[:, DV:DV + CHUNK], a_up).astype(BF16)
        v = prv_ref[rows, C_GV:C_GV + HV].astype(BF16)
        for h in range(HEADS):
            o = _dot(attn[head_rows[h]], v[:, head_cols[h]]) + r[head_rows[h], :DV]
            gate = prv_ref[rows, C_GG + h * DV:C_GG + (h + 1) * DV]
            yn = o * lax.rsqrt(jnp.mean(o * o, axis=-1, keepdims=True) + EPS) * glan_ref[h:h + 1, :]
            y_ref[rows, h * DV:(h + 1) * DV] = (yn * _swish(gate)).astype(BF16)

    def ret_values(c, r):
        rows = chunk_rows[c]
        inter = r[:, :DV] * qdec_ref[...]
        scores = (r[:, DV:DV + CHUNK] * dmat_ref[...]).astype(BF16)
        v = prv_ref[rows, C_RV:C_RV + HV].astype(BF16)
        for h in range(HEADS):
            o = _dot(scores[head_rows[h]], v[:, head_cols[h]]) + inter[head_rows[h]]
            gate = prv_ref[rows, C_RG + h * DV:C_RG + (h + 1) * DV]
            d = o - jnp.mean(o, axis=-1, keepdims=True)
            yn = d * lax.rsqrt(jnp.mean(d * d, axis=-1, keepdims=True) + EPS) * retn_ref[h:h + 1, :]
            y_ref[rows, HV + h * DV:HV + (h + 1) * DV] = (yn * _swish(gate)).astype(BF16)

    def out_proj(done):
        o_ref[0, done, :] = xres_ref[0, done, :] + _dot(y_ref[done, :], wout_ref[...])

    finished = None
    ahead = (gla_scores(0), ret_scores(0))
    for c in range(n_chunks):
        (g_r, g_up), r_r = ahead
        if c + 1 < n_chunks:
            ahead = (gla_scores(c + 1), ret_scores(c + 1))
        if finished is not None:
            out_proj(finished)
            finished = None
        else:
            emit_proj(1)
        gla_values(c, g_r, g_up)
        ret_values(c, r_r)
        if (c + 1) % (OUT_ROWS // CHUNK) == 0:
            finished = slice((c + 1) * CHUNK - OUT_ROWS, (c + 1) * CHUNK)
    emit_proj(len(pieces))
    out_proj(finished)


def _xattn_mlp_kernel(h_ref, wqk_ref, wvo_ref, gx_ref, gf_ref, w1_ref, w2_ref, gfin_ref, o_ref):
    blocks = [slice(r, r + XATTN_ROWS) for r in range(0, h_ref.shape[1], XATTN_ROWS)]
    n_mem = wqk_ref.shape[2] // XATTN_HEADS
    h = [h_ref[0, rows, :] for rows in blocks]
    scores = [_dot(_rms(hb, gx_ref[...]).astype(BF16), wqk_ref[0]) for hb in h]
    probs = []
    for sb in scores:
        heads = []
        for a in range(XATTN_HEADS):
            s = sb[:, a * n_mem:(a + 1) * n_mem]
            e = jnp.exp(s - jnp.max(s, axis=-1, keepdims=True))
            heads.append((e * (1.0 / jnp.sum(e, axis=-1, keepdims=True))).astype(BF16))
        probs.append(jnp.concatenate(heads, axis=1))
    h = [hb + _dot(pb, wvo_ref[0]) for hb, pb in zip(h, probs)]

    hn = [_rms(hb, gf_ref[...]).astype(BF16) for hb in h]
    for c in range(D_FF // FF_BLOCK):
        cols = slice(c * FF_BLOCK, (c + 1) * FF_BLOCK)
        u = [jnp.maximum(_dot(hb, w1_ref[:, cols]), 0.0) for hb in hn]
        h = [hb + _dot((ub * ub).astype(BF16), w2_ref[cols, :]) for hb, ub in zip(h, u)]
    for rows, hb in zip(blocks, h):
        o_ref[0, rows, :] = _rms(hb, gfin_ref[...])


def _const_spec(shape):
    n = len(shape)
    return pl.BlockSpec(shape, lambda *_: (0,) * n, pipeline_mode=pl.Buffered(1))


def _retention_tables():
    gamma = 1.0 - np.exp2(-np.linspace(RET_DECAY_EXP_LO, RET_DECAY_EXP_HI, HEADS, dtype=np.float32))
    log_g = np.log(gamma.astype(np.float32)).astype(np.float32)
    pos = np.arange(CHUNK, dtype=np.float32)
    dmat = np.exp(np.abs(pos[:, None] - pos[None, :])[None] * log_g[:, None, None])
    k_dec = np.exp((CHUNK - 1.0 - pos)[None, :] * log_g[:, None])
    q_dec = np.exp((pos + 1.0)[None, :] * log_g[:, None])
    s_dec = np.exp(CHUNK * log_g)
    dmat_s = dmat.reshape(HEADS * CHUNK, CHUNK)
    qdec_s = np.broadcast_to(q_dec.reshape(HEADS * CHUNK, 1), (HEADS * CHUNK, DV))
    kdect_s = np.repeat(k_dec, DK, axis=0)
    sdec_s = np.broadcast_to(np.repeat(s_dec, DK)[:, None], (HEADS * DK, DV))
    as32 = lambda a: jnp.asarray(np.ascontiguousarray(a), dtype=F32)
    return as32(dmat_s), as32(qdec_s), as32(kdect_s), as32(sdec_s)


def kernel(x, mem, norm_mix_g, w_in, gla_w_gate, gla_b_gate, gla_norm_g, ret_norm_g, w_out, norm_x_g,
           norm_mem_g, xa_w_q, xa_w_kv, xa_w_o, norm_ffn_g, ffn_w1, ffn_w2, final_norm_g):
    B, S, D = x.shape
    n_mem = mem.shape[1]
    assert D == D_MODEL and S % SEQ_TILE == 0 and S % XATTN_TILE == 0 and w_in.shape[0] == 1
    ts = SEQ_TILE
    n_tiles = S // ts

    inv_freq = ROPE_BASE ** (-jnp.arange(0, DK, 2, dtype=F32) / DK)
    inv_lane = jnp.tile(inv_freq, 2 * HEADS)
    ang_off = jnp.arange(ts, dtype=F32)[:, None] * inv_lane[None, :]
    ang_base = (jnp.arange(n_tiles, dtype=F32) * ts)[:, None, None] * inv_lane[None, None, :]
    cos_off, sin_off = jnp.cos(ang_off), jnp.sin(ang_off)
    cos_base, sin_base = jnp.cos(ang_base), jnp.sin(ang_base)

    w_in_r = pl.pallas_call(
        _prep_w_in_kernel,
        grid=(D // PREP_ROWS,),
        in_specs=[pl.BlockSpec((w_in.shape[2], PREP_ROWS), lambda i: (0, i))],
        out_specs=pl.BlockSpec((PREP_ROWS, D_PROJ), lambda i: (i, 0)),
        out_shape=jax.ShapeDtypeStruct((D, D_PROJ), BF16),
        compiler_params=pltpu.CompilerParams(dimension_semantics=("arbitrary",),
                                             vmem_limit_bytes=VMEM_LIMIT_BYTES),
        name="prep_w_in",
    )(jnp.swapaxes(w_in[0], 0, 1))
    w_gate_p = jnp.concatenate(
        [gla_w_gate[0], jnp.zeros((GATE_RANK_PAD - GATE_RANK, HK), F32)], axis=0).astype(BF16)
    dmat_s, qdec_s, kdect_s, sdec_s = _retention_tables()

    params = pltpu.CompilerParams(dimension_semantics=("arbitrary", "arbitrary"),
                                  vmem_limit_bytes=VMEM_LIMIT_BYTES)

    hm = XATTN_HEADS * n_mem
    wqk, wvo = pl.pallas_call(
        _mem_kv_kernel,
        grid=(B,),
        in_specs=[pl.BlockSpec((1, n_mem, D), lambda b: (b, 0, 0)),
                  _const_spec((1, D)), _const_spec((D, 2 * D)), _const_spec((D, D)),
                  _const_spec((D, D))],
        out_specs=[pl.BlockSpec((1, D, hm), lambda b: (b, 0, 0)),
                   pl.BlockSpec((1, hm, D), lambda b: (b, 0, 0))],
        out_shape=[jax.ShapeDtypeStruct((B, D, hm), BF16), jax.ShapeDtypeStruct((B, hm, D), BF16)],
        compiler_params=pltpu.CompilerParams(dimension_semantics=("arbitrary",),
                                             vmem_limit_bytes=VMEM_LIMIT_BYTES),
        name="mem_kv",
    )(mem, norm_mem_g[0][None, :], xa_w_kv[0], xa_w_q[0], xa_w_o[0])

    n_steps = B * n_tiles + 1

    def next_tile(s):
        p = jnp.minimum(s, n_steps - 2)
        return (p // n_tiles, p % n_tiles, 0)

    def prev_tile(s):
        q = jnp.maximum(s - 1, 0)
        return (q // n_tiles, q % n_tiles, 0)

    def prev_rope(s):
        return (jnp.maximum(s - 1, 0) % n_tiles, 0, 0)

    h1 = pl.pallas_call(
        functools.partial(_mixer_kernel, n_chunks=ts // CHUNK, n_tiles=n_tiles),
        grid=(n_steps,),
        in_specs=[pl.BlockSpec((1, ts, D), next_tile),
                  pl.BlockSpec((1, ts, D), prev_tile),
                  pl.BlockSpec((1, 1, HK), prev_rope), pl.BlockSpec((1, 1, HK), prev_rope),
                  _const_spec((ts, HK)), _const_spec((ts, HK)),
                  _const_spec((1, D)), _const_spec((D, D_PROJ)),
                  _const_spec((GATE_RANK_PAD, HK)), _const_spec((1, HK)),
                  _const_spec((HEADS, DV)), _const_spec((HEADS, DV)),
                  _const_spec((D, D)),
                  _const_spec((HEADS * CHUNK, CHUNK)), _const_spec((HEADS * CHUNK, DV)),
                  _const_spec((HEADS * DK, CHUNK)), _const_spec((HEADS * DK, DV))],
        out_specs=pl.BlockSpec((1, ts, D), prev_tile),
        out_shape=jax.ShapeDtypeStruct((B, S, D), F32),
        scratch_shapes=[pltpu.VMEM((ts, D_PROJ), F32), pltpu.VMEM((ts, D_PROJ), F32),
                        pltpu.VMEM((ts, D), BF16), pltpu.VMEM((ts, D), BF16),
                        pltpu.VMEM((3, ts // CHUNK, HK, CHUNK), BF16),
                        pltpu.VMEM((HEADS * DK, DV), F32), pltpu.VMEM((HEADS * DK, DV), F32)],
        compiler_params=pltpu.CompilerParams(dimension_semantics=("arbitrary",),
                                             vmem_limit_bytes=VMEM_LIMIT_BYTES),
        name="mixer",
    )(x, x, cos_base, sin_base, cos_off, sin_off, norm_mix_g[0][None, :], w_in_r, w_gate_p,
      gla_b_gate[0][None, :], gla_norm_g[0], ret_norm_g[0], w_out[0].astype(BF16),
      dmat_s, qdec_s, kdect_s, sdec_s)

    out = pl.pallas_call(
        _xattn_mlp_kernel,
        grid=(B, S // XATTN_TILE),
        in_specs=[pl.BlockSpec((1, XATTN_TILE, D), lambda b, t: (b, t, 0)),
                  pl.BlockSpec((1, D, hm), lambda b, t: (b, 0, 0)),
                  pl.BlockSpec((1, hm, D), lambda b, t: (b, 0, 0)),
                  _const_spec((1, D)),
                  _const_spec((1, D)), _const_spec((D, D_FF)), _const_spec((D_FF, D)),
                  _const_spec((1, D))],
        out_specs=pl.BlockSpec((1, XATTN_TILE, D), lambda b, t: (b, t, 0)),
        out_shape=jax.ShapeDtypeStruct((B, S, D), F32),
        compiler_params=params,
        name="xattn_mlp",
    )(h1, wqk, wvo, norm_x_g[0][None, :], norm_ffn_g[0][None, :], ffn_w1[0].astype(BF16),
      ffn_w2[0].astype(BF16), final_norm_g[None, :])
    return out
```

```python
import functools

import numpy as np
import jax
import jax.numpy as jnp
from jax import lax
from jax.experimental import pallas as pl
from jax.experimental.pallas import tpu as pltpu

D_MODEL = 1024
CHUNK = 64
HEADS = 4
DK = 64
DV = 128
GATE_RANK = 16
GATE_RANK_PAD = 128
GATE_TAU = 16.0
RET_DECAY_EXP_LO = 5.0
RET_DECAY_EXP_HI = 12.0
ROPE_BASE = 10000.0
XATTN_HEADS = 4
XATTN_DH = D_MODEL // XATTN_HEADS
D_FF = 4 * D_MODEL
FF_BLOCK = 1024
XATTN_TILE = 512
XATTN_BLOCKS = (256, 256)
PREP_ROWS = 256
EPS = 1e-6

HK = HEADS * DK
HV = HEADS * DV
C_GQ, C_GK, C_GV, C_GG = 0, HK, 2 * HK, 2 * HK + HV
C_RQ = C_GG + HV
C_RK = C_RQ + HK
C_RV = C_RK + HK
C_RG = C_RV + HV
C_LR = C_RG + HV
D_PROJ = C_LR + GATE_RANK_PAD

SEQ_TILE = 512
PROJ_PIECE = 256
STAGE_A_PIECES = 5
CUM_BLOCK = 256
OUT_ROWS = 512
VMEM_LIMIT_BYTES = 56 * 1024 * 1024

F32 = jnp.float32
BF16 = jnp.bfloat16


def _dot(a, b):
    return jnp.dot(a, b, preferred_element_type=F32)


def _dot_nt(a, b):
    return lax.dot_general(a, b, (((1,), (1,)), ((), ())), preferred_element_type=F32)


def _rms(x, g):
    return x * lax.rsqrt(jnp.mean(x * x, axis=-1, keepdims=True) + EPS) * g


def _swish(g):
    return g / (1.0 + jnp.exp(-g))


def _stack_heads(t):
    lane_head = lax.broadcasted_iota(jnp.int32, t.shape, 1) // DK
    zero = jnp.zeros_like(t)
    return jnp.concatenate([jnp.where(lane_head == h, t, zero) for h in range(HEADS)], axis=0)


def _prep_w_in_kernel(wt_ref, o_ref):
    lr0 = 2 * HK + HV
    o_ref[:, :lr0] = wt_ref[:lr0, :].T.astype(BF16)
    o_ref[:, lr0:C_LR] = wt_ref[lr0 + GATE_RANK:, :].T.astype(BF16)
    tail = wt_ref[lr0:lr0 + GATE_RANK_PAD, :]
    row = lax.broadcasted_iota(jnp.int32, tail.shape, 0)
    o_ref[:, C_LR:] = jnp.where(row < GATE_RANK, tail, 0.0).T.astype(BF16)


def _mem_kv_kernel(mem_ref, g_ref, wkv_ref, wq_ref, wo_ref, wqk_ref, wvo_ref):
    n_mem = mem_ref.shape[1]
    mn = _rms(mem_ref[0], g_ref[...])
    kv = _dot(mn, wkv_ref[...])
    for h in range(XATTN_HEADS):
        cols = slice(h * XATTN_DH, (h + 1) * XATTN_DH)
        mcols = slice(h * n_mem, (h + 1) * n_mem)
        k_h = kv[:, cols]
        v_h = kv[:, D_MODEL + h * XATTN_DH:D_MODEL + (h + 1) * XATTN_DH]
        wqk_ref[0, :, mcols] = (_dot_nt(wq_ref[:, cols], k_h) * (XATTN_DH ** -0.5)).astype(BF16)
        wvo_ref[0, mcols, :] = _dot(v_h, wo_ref[cols, :]).astype(BF16)


def _mixer_kernel(xnext_ref, xres_ref, cosb_ref, sinb_ref, coso_ref, sino_ref, g_ref, win_ref, wg_ref,
                  bg_ref, glan_ref, retn_ref, wout_ref, dmat_ref, qdec_ref, kdect_ref, sdec_ref,
                  o_ref, proj_a_ref, proj_b_ref, xn_ref, y_ref, kt_ref, sg_ref, sr_ref, *,
                  n_chunks, n_tiles):
    s = pl.program_id(0)
    prev_tile = jnp.maximum(s - 1, 0)

    @pl.when(s == 0)
    def _():
        proj_b_ref[...] = jnp.zeros_like(proj_b_ref)

    @pl.when(prev_tile % n_tiles == 0)
    def _():
        sg_ref[...] = jnp.zeros_like(sg_ref)
        sr_ref[...] = jnp.zeros_like(sr_ref)

    chunk_pass = functools.partial(
        _chunk_pass, xnext_ref=xnext_ref, g_ref=g_ref, xn_ref=xn_ref,
        rope_refs=(cosb_ref, sinb_ref, coso_ref, sino_ref),
        win_ref=win_ref, wg_ref=wg_ref, bg_ref=bg_ref, glan_ref=glan_ref, retn_ref=retn_ref,
        dmat_ref=dmat_ref, qdec_ref=qdec_ref, kdect_ref=kdect_ref, sdec_ref=sdec_ref, y_ref=y_ref,
        kt_ref=kt_ref, sg_ref=sg_ref, sr_ref=sr_ref, xres_ref=xres_ref, wout_ref=wout_ref,
        o_ref=o_ref, n_chunks=n_chunks)

    @pl.when(s % 2 == 0)
    def _():
        chunk_pass(proj_a_ref, proj_b_ref)

    @pl.when(s % 2 == 1)
    def _():
        chunk_pass(proj_b_ref, proj_a_ref)


def _chunk_pass(cur_ref, prv_ref, *, xnext_ref, g_ref, xn_ref, rope_refs, win_ref, wg_ref, bg_ref, glan_ref,
                retn_ref, dmat_ref, qdec_ref, kdect_ref, sdec_ref, y_ref, kt_ref, sg_ref, sr_ref,
                xres_ref, wout_ref, o_ref, n_chunks):
    ts = n_chunks * CHUNK
    chunk_rows = [slice(c * CHUNK, (c + 1) * CHUNK) for c in range(n_chunks)]
    head_rows = [slice(h * CHUNK, (h + 1) * CHUNK) for h in range(HEADS)]
    head_cols = [slice(h * DV, (h + 1) * DV) for h in range(HEADS)]
    pieces = [(i * PROJ_PIECE, min((i + 1) * PROJ_PIECE, D_PROJ))
              for i in range(-(-D_PROJ // PROJ_PIECE))]

    def emit_proj(n):
        for _ in range(min(n, len(pieces))):
            lo, hi = pieces.pop(0)
            cur_ref[:, lo:hi] = _dot(xn_ref[...], win_ref[:, lo:hi])

    bi = lax.broadcasted_iota(jnp.int32, (CUM_BLOCK, CUM_BLOCK), 0)
    bj = lax.broadcasted_iota(jnp.int32, (CUM_BLOCK, CUM_BLOCK), 1)
    tril_blocks = ((bi // CHUNK == bj // CHUNK) & (bj <= bi)).astype(BF16)
    si = lax.broadcasted_iota(jnp.int32, (HEADS * CHUNK, CHUNK), 0) % CHUNK
    sj = lax.broadcasted_iota(jnp.int32, (HEADS * CHUNK, CHUNK), 1)
    lower = sj <= si
    first_half = (lax.broadcasted_iota(jnp.int32, (1, HK), 1) % DK) < (DK // 2)
    lane_pad = jnp.zeros((HK, CHUNK), BF16)

    def contributions(kdt, v):
        return jnp.concatenate([_dot(kdt[head_rows[h]], v[:, head_cols[h]]) for h in range(HEADS)],
                               axis=0)

    def scan_states(s_ref, decays, contribs):
        state, starts = s_ref[...], []
        for dec, con in zip(decays, contribs):
            starts.append(state.astype(BF16))
            state = state * dec + con
        s_ref[...] = state
        return starts

    xn_ref[...] = _rms(xnext_ref[0], g_ref[...]).astype(BF16)
    z = _dot(prv_ref[:, C_LR:C_LR + GATE_RANK_PAD].astype(BF16), wg_ref[...]) + bg_ref[...]
    emit_proj(STAGE_A_PIECES)
    log_a = (jnp.minimum(z, 0.0) - jnp.log1p(jnp.exp(-jnp.abs(z)))) * (1.0 / GATE_TAU)
    la_hi = log_a.astype(BF16)
    la_lo = (log_a - la_hi.astype(F32)).astype(BF16)
    la_split = jnp.concatenate([la_hi, la_lo], axis=1)
    bcum = []
    for b in range(ts // CUM_BLOCK):
        r = _dot(tril_blocks, la_split[b * CUM_BLOCK:(b + 1) * CUM_BLOCK])
        bcum.append(r[:, :HK] + r[:, HK:])
    bcum = jnp.concatenate(bcum, axis=0)
    emit_proj(1)
    q = prv_ref[:, C_GQ:C_GQ + HK] * (DK ** -0.5)
    qeb = (q * jnp.exp(bcum)).astype(BF16)
    qebi = (q * jnp.exp(-bcum)).astype(BF16)

    g_decay, g_contrib = [], []
    for c, rows in enumerate(chunk_rows):
        kt = prv_ref[rows, C_GK:C_GK + HK].T
        bct = bcum[rows].T
        blt = bct[:, CHUNK - 1:CHUNK]
        kt_ref[0, c] = (kt * jnp.exp(bct)).astype(BF16)
        kt_ref[1, c] = (kt * jnp.exp(-bct)).astype(BF16)
        kdt = (kt * jnp.exp(blt - bct)).astype(BF16)
        g_decay.append(jnp.exp(blt))
        g_contrib.append(contributions(kdt, prv_ref[rows, C_GV:C_GV + HV].astype(BF16)))
        if c % 2 == 1:
            emit_proj(1)

    cosb, sinb, coso, sino = [r[...] for r in rope_refs]
    cosb, sinb = cosb[0], sinb[0]
    cs = cosb * coso - sinb * sino
    sn = sinb * coso + cosb * sino
    sn = jnp.where(first_half, -sn, sn)

    def rope(t):
        partner = jnp.where(first_half, pltpu.roll(t, HK - DK // 2, 1), pltpu.roll(t, DK // 2, 1))
        return t * cs + partner * sn

    qrb = rope(prv_ref[:, C_RQ:C_RQ + HK]).astype(BF16)
    kr = rope(prv_ref[:, C_RK:C_RK + HK]) * (DK ** -0.5)
    r_contrib = []
    for c, rows in enumerate(chunk_rows):
        kt = kr[rows].T
        kt_ref[2, c] = kt.astype(BF16)
        kdt = (kt * kdect_ref[...]).astype(BF16)
        r_contrib.append(contributions(kdt, prv_ref[rows, C_RV:C_RV + HV].astype(BF16)))
        if c % 2 == 1:
            emit_proj(1)

    g_start = scan_states(sg_ref, g_decay, g_contrib)
    r_start = scan_states(sr_ref, [sdec_ref[...]] * n_chunks, r_contrib)

    def gla_scores(c):
        rows = chunk_rows[c]
        r = _dot(_stack_heads(qeb[rows]), jnp.concatenate([g_start[c], kt_ref[1, c], lane_pad], axis=1))
        a_up = _dot(_stack_heads(qebi[rows]), kt_ref[0, c])
        return r[:, :DV], jnp.where(lower, r[:, DV:DV + CHUNK], a_up).astype(BF16)

    def ret_scores(c):
        r = _dot(_stack_heads(qrb[chunk_rows[c]]),
                 jnp.concatenate([r_start[c], kt_ref[2, c], lane_pad], axis=1))
        return r[:, :DV] * qdec_ref[...], (r[:, DV:DV + CHUNK] * dmat_ref[...]).astype(BF16)

    def gla_values(c, inter, attn):
        rows = chunk_rows[c]
        v = prv_ref[rows, C_GV:C_GV + HV].astype(BF16)
        for h in range(HEADS):
            o = _dot(attn[head_rows[h]], v[:, head_cols[h]]) + inter[head_rows[h]]
            gate = prv_ref[rows, C_GG + h * DV:C_GG + (h + 1) * DV]
            yn = o * lax.rsqrt(jnp.mean(o * o, axis=-1, keepdims=True) + EPS) * glan_ref[h:h + 1, :]
            y_ref[rows, h * DV:(h + 1) * DV] = (yn * _swish(gate)).astype(BF16)

    def ret_values(c, inter, scores):
        rows = chunk_rows[c]
        v = prv_ref[rows, C_RV:C_RV + HV].astype(BF16)
        for h in range(HEADS):
            o = _dot(scores[head_rows[h]], v[:, head_cols[h]]) + inter[head_rows[h]]
            gate = prv_ref[rows, C_RG + h * DV:C_RG + (h + 1) * DV]
            d = o - jnp.mean(o, axis=-1, keepdims=True)
            yn = d * lax.rsqrt(jnp.mean(d * d, axis=-1, keepdims=True) + EPS) * retn_ref[h:h + 1, :]
            y_ref[rows, HV + h * DV:HV + (h + 1) * DV] = (yn * _swish(gate)).astype(BF16)

    def out_proj(done):
        o_ref[0, done, :] = xres_ref[0, done, :] + _dot(y_ref[done, :], wout_ref[...])

    finished = None
    ahead = (gla_scores(0), ret_scores(0))
    for c in range(n_chunks):
        gla_ahead, ret_ahead = ahead
        if c + 1 < n_chunks:
            ahead = (gla_scores(c + 1), ret_scores(c + 1))
        if finished is not None:
            out_proj(finished)
            finished = None
        else:
            emit_proj(1)
        gla_values(c, *gla_ahead)
        ret_values(c, *ret_ahead)
        if (c + 1) % (OUT_ROWS // CHUNK) == 0:
            finished = slice((c + 1) * CHUNK - OUT_ROWS, (c + 1) * CHUNK)
    emit_proj(len(pieces))
    out_proj(finished)


def _xattn_mlp_kernel(h_ref, wqk_ref, wvo_ref, gx_ref, gf_ref, w1_ref, w2_ref, gfin_ref, o_ref):
    bounds = np.cumsum((0,) + XATTN_BLOCKS)
    blocks = [slice(int(a), int(b)) for a, b in zip(bounds[:-1], bounds[1:])]
    n_mem = wqk_ref.shape[2] // XATTN_HEADS
    h = [h_ref[0, rows, :] for rows in blocks]
    scores = [_dot(_rms(hb, gx_ref[...]).astype(BF16), wqk_ref[0]) for hb in h]
    probs = []
    for sb in scores:
        heads = []
        for a in range(XATTN_HEADS):
            s = sb[:, a * n_mem:(a + 1) * n_mem]
            e = jnp.exp(s - jnp.max(s, axis=-1, keepdims=True))
            heads.append((e * (1.0 / jnp.sum(e, axis=-1, keepdims=True))).astype(BF16))
        probs.append(jnp.concatenate(heads, axis=1))
    h = [hb + _dot(pb, wvo_ref[0]) for hb, pb in zip(h, probs)]

    hn = [_rms(hb, gf_ref[...]).astype(BF16) for hb in h]
    for c in range(D_FF // FF_BLOCK):
        cols = slice(c * FF_BLOCK, (c + 1) * FF_BLOCK)
        u = [jnp.maximum(_dot(hb, w1_ref[:, cols]), 0.0) for hb in hn]
        h = [hb + _dot((ub * ub).astype(BF16), w2_ref[cols, :]) for hb, ub in zip(h, u)]
    for rows, hb in zip(blocks, h):
        o_ref[0, rows, :] = _rms(hb, gfin_ref[...])


def _const_spec(shape):
    n = len(shape)
    return pl.BlockSpec(shape, lambda *_: (0,) * n, pipeline_mode=pl.Buffered(1))


def _retention_tables():
    gamma = 1.0 - np.exp2(-np.linspace(RET_DECAY_EXP_LO, RET_DECAY_EXP_HI, HEADS, dtype=np.float32))
    log_g = np.log(gamma.astype(np.float32)).astype(np.float32)
    pos = np.arange(CHUNK, dtype=np.float32)
    dmat = np.exp(np.abs(pos[:, None] - pos[None, :])[None] * log_g[:, None, None])
    k_dec = np.exp((CHUNK - 1.0 - pos)[None, :] * log_g[:, None])
    q_dec = np.exp((pos + 1.0)[None, :] * log_g[:, None])
    s_dec = np.exp(CHUNK * log_g)
    dmat_s = dmat.reshape(HEADS * CHUNK, CHUNK)
    qdec_s = np.broadcast_to(q_dec.reshape(HEADS * CHUNK, 1), (HEADS * CHUNK, DV))
    kdect_s = np.repeat(k_dec, DK, axis=0)
    sdec_s = np.broadcast_to(np.repeat(s_dec, DK)[:, None], (HEADS * DK, DV))
    as32 = lambda a: jnp.asarray(np.ascontiguousarray(a), dtype=F32)
    return as32(dmat_s), as32(qdec_s), as32(kdect_s), as32(sdec_s)


def kernel(x, mem, norm_mix_g, w_in, gla_w_gate, gla_b_gate, gla_norm_g, ret_norm_g, w_out, norm_x_g,
           norm_mem_g, xa_w_q, xa_w_kv, xa_w_o, norm_ffn_g, ffn_w1, ffn_w2, final_norm_g):
    B, S, D = x.shape
    n_mem = mem.shape[1]
    assert D == D_MODEL and S % SEQ_TILE == 0 and S % XATTN_TILE == 0 and w_in.shape[0] == 1
    ts = SEQ_TILE
    n_tiles = S // ts

    inv_freq = ROPE_BASE ** (-jnp.arange(0, DK, 2, dtype=F32) / DK)
    inv_lane = jnp.tile(inv_freq, 2 * HEADS)
    ang_off = jnp.arange(ts, dtype=F32)[:, None] * inv_lane[None, :]
    ang_base = (jnp.arange(n_tiles, dtype=F32) * ts)[:, None, None] * inv_lane[None, None, :]
    cos_off, sin_off = jnp.cos(ang_off), jnp.sin(ang_off)
    cos_base, sin_base = jnp.cos(ang_base), jnp.sin(ang_base)

    w_in_r = pl.pallas_call(
        _prep_w_in_kernel,
        grid=(D // PREP_ROWS,),
        in_specs=[pl.BlockSpec((w_in.shape[2], PREP_ROWS), lambda i: (0, i))],
        out_specs=pl.BlockSpec((PREP_ROWS, D_PROJ), lambda i: (i, 0)),
        out_shape=jax.ShapeDtypeStruct((D, D_PROJ), BF16),
        compiler_params=pltpu.CompilerParams(dimension_semantics=("arbitrary",),
                                             vmem_limit_bytes=VMEM_LIMIT_BYTES),
        name="prep_w_in",
    )(jnp.swapaxes(w_in[0], 0, 1))
    w_gate_p = jnp.concatenate(
        [gla_w_gate[0], jnp.zeros((GATE_RANK_PAD - GATE_RANK, HK), F32)], axis=0).astype(BF16)
    dmat_s, qdec_s, kdect_s, sdec_s = _retention_tables()

    params = pltpu.CompilerParams(dimension_semantics=("arbitrary", "arbitrary"),
                                  vmem_limit_bytes=VMEM_LIMIT_BYTES)

    hm = XATTN_HEADS * n_mem
    wqk, wvo = pl.pallas_call(
        _mem_kv_kernel,
        grid=(B,),
        in_specs=[pl.BlockSpec((1, n_mem, D), lambda b: (b, 0, 0)),
                  _const_spec((1, D)), _const_spec((D, 2 * D)), _const_spec((D, D)),
                  _const_spec((D, D))],
        out_specs=[pl.BlockSpec((1, D, hm), lambda b: (b, 0, 0)),
                   pl.BlockSpec((1, hm, D), lambda b: (b, 0, 0))],
        out_shape=[jax.ShapeDtypeStruct((B, D, hm), BF16), jax.ShapeDtypeStruct((B, hm, D), BF16)],
        compiler_params=pltpu.CompilerParams(dimension_semantics=("arbitrary",),
                                             vmem_limit_bytes=VMEM_LIMIT_BYTES),
        name="mem_kv",
    )(mem, norm_mem_g[0][None, :], xa_w_kv[0], xa_w_q[0], xa_w_o[0])

    n_steps = B * n_tiles + 1

    def next_tile(s):
        p = jnp.minimum(s, n_steps - 2)
        return (p // n_tiles, p % n_tiles, 0)

    def prev_tile(s):
        q = jnp.maximum(s - 1, 0)
        return (q // n_tiles, q % n_tiles, 0)

    def prev_rope(s):
        return (jnp.maximum(s - 1, 0) % n_tiles, 0, 0)

    h1 = pl.pallas_call(
        functools.partial(_mixer_kernel, n_chunks=ts // CHUNK, n_tiles=n_tiles),
        grid=(n_steps,),
        in_specs=[pl.BlockSpec((1, ts, D), next_tile),
                  pl.BlockSpec((1, ts, D), prev_tile),
                  pl.BlockSpec((1, 1, HK), prev_rope), pl.BlockSpec((1, 1, HK), prev_rope),
                  _const_spec((ts, HK)), _const_spec((ts, HK)),
                  _const_spec((1, D)), _const_spec((D, D_PROJ)),
                  _const_spec((GATE_RANK_PAD, HK)), _const_spec((1, HK)),
                  _const_spec((HEADS, DV)), _const_spec((HEADS, DV)),
                  _const_spec((D, D)),
                  _const_spec((HEADS * CHUNK, CHUNK)), _const_spec((HEADS * CHUNK, DV)),
                  _const_spec((HEADS * DK, CHUNK)), _const_spec((HEADS * DK, DV))],
        out_specs=pl.BlockSpec((1, ts, D), prev_tile),
        out_shape=jax.ShapeDtypeStruct((B, S, D), F32),
        scratch_shapes=[pltpu.VMEM((ts, D_PROJ), F32), pltpu.VMEM((ts, D_PROJ), F32),
                        pltpu.VMEM((ts, D), BF16), pltpu.VMEM((ts, D), BF16),
                        pltpu.VMEM((3, ts // CHUNK, HK, CHUNK), BF16),
                        pltpu.VMEM((HEADS * DK, DV), F32), pltpu.VMEM((HEADS * DK, DV), F32)],
        compiler_params=pltpu.CompilerParams(dimension_semantics=("arbitrary",),
                                             vmem_limit_bytes=VMEM_LIMIT_BYTES),
        name="mixer",
    )(x, x, cos_base, sin_base, cos_off, sin_off, norm_mix_g[0][None, :], w_in_r, w_gate_p,
      gla_b_gate[0][None, :], gla_norm_g[0], ret_norm_g[0], w_out[0].astype(BF16),
      dmat_s, qdec_s, kdect_s, sdec_s)

    out = pl.pallas_call(
        _xattn_mlp_kernel,
        grid=(B, S // XATTN_TILE),
        in_specs=[pl.BlockSpec((1, XATTN_TILE, D), lambda b, t: (b, t, 0)),
                  pl.BlockSpec((1, D, hm), lambda b, t: (b, 0, 0)),
                  pl.BlockSpec((1, hm, D), lambda b, t: (b, 0, 0)),
                  _const_spec((1, D)),
                  _const_spec((1, D)), _const_spec((D, D_FF)), _const_spec((D_FF, D)),
                  _const_spec((1, D))],
        out_specs=pl.BlockSpec((1, XATTN_TILE, D), lambda b, t: (b, t, 0)),
        out_shape=jax.ShapeDtypeStruct((B, S, D), F32),
        compiler_params=params,
        name="xattn_mlp",
    )(h1, wqk, wvo, norm_x_g[0][None, :], norm_ffn_g[0][None, :], ffn_w1[0].astype(BF16),
      ffn_w2[0].astype(BF16), final_norm_g[None, :])
    return out
```

```python
import functools

import numpy as np
import jax
import jax.numpy as jnp
from jax import lax
from jax.experimental import pallas as pl
from jax.experimental.pallas import tpu as pltpu

D_MODEL = 1024
CHUNK = 64
HEADS = 4
DK = 64
DV = 128
GATE_RANK = 16
GATE_RANK_PAD = 128
GATE_TAU = 16.0
RET_DECAY_EXP_LO = 5.0
RET_DECAY_EXP_HI = 12.0
ROPE_BASE = 10000.0
XATTN_HEADS = 4
XATTN_DH = D_MODEL // XATTN_HEADS
D_FF = 4 * D_MODEL
FF_BLOCK = 1024
XATTN_TILE = 512
XATTN_BLOCKS = (256, 256)
PREP_ROWS = 256
EPS = 1e-6

HK = HEADS * DK
HV = HEADS * DV
C_GQ, C_GK, C_GV, C_GG = 0, HK, 2 * HK, 2 * HK + HV
C_RQ = C_GG + HV
C_RK = C_RQ + HK
C_RV = C_RK + HK
C_RG = C_RV + HV
C_LR = C_RG + HV
D_PROJ = C_LR + GATE_RANK_PAD

SEQ_TILE = 512
PROJ_PIECE = 256
STAGE_A_PIECES = 5
CUM_BLOCK = 256
OUT_ROWS = 512
VMEM_LIMIT_BYTES = 56 * 1024 * 1024

F32 = jnp.float32
BF16 = jnp.bfloat16


def _dot(a, b):
    return jnp.dot(a, b, preferred_element_type=F32)


def _dot_nt(a, b):
    return lax.dot_general(a, b, (((1,), (1,)), ((), ())), preferred_element_type=F32)


def _rms(x, g):
    return x * lax.rsqrt(jnp.mean(x * x, axis=-1, keepdims=True) + EPS) * g


def _swish(g):
    return g / (1.0 + jnp.exp(-g))


def _stack_heads(t):
    lane_head = lax.broadcasted_iota(jnp.int32, t.shape, 1) // DK
    zero = jnp.zeros_like(t)
    return jnp.concatenate([jnp.where(lane_head == h, t, zero) for h in range(HEADS)], axis=0)


def _prep_w_in_kernel(wt_ref, o_ref):
    lr0 = 2 * HK + HV
    o_ref[:, :lr0] = wt_ref[:lr0, :].T.astype(BF16)
    o_ref[:, lr0:C_LR] = wt_ref[lr0 + GATE_RANK:, :].T.astype(BF16)
    tail = wt_ref[lr0:lr0 + GATE_RANK_PAD, :]
    row = lax.broadcasted_iota(jnp.int32, tail.shape, 0)
    o_ref[:, C_LR:] = jnp.where(row < GATE_RANK, tail, 0.0).T.astype(BF16)


def _mem_kv_kernel(mem_ref, g_ref, wkv_ref, wq_ref, wo_ref, w1_ref, w2_ref,
                   wqk_ref, wvo_ref, w1b_ref, w2b_ref):
    w1b_ref[...] = w1_ref[...].astype(BF16)
    w2b_ref[...] = w2_ref[...].astype(BF16)
    n_mem = mem_ref.shape[1]
    mn = _rms(mem_ref[0], g_ref[...])
    kv = _dot(mn, wkv_ref[...])
    for h in range(XATTN_HEADS):
        cols = slice(h * XATTN_DH, (h + 1) * XATTN_DH)
        mcols = slice(h * n_mem, (h + 1) * n_mem)
        k_h = kv[:, cols]
        v_h = kv[:, D_MODEL + h * XATTN_DH:D_MODEL + (h + 1) * XATTN_DH]
        wqk_ref[0, :, mcols] = (_dot_nt(wq_ref[:, cols], k_h) * (XATTN_DH ** -0.5)).astype(BF16)
        wvo_ref[0, mcols, :] = _dot(v_h, wo_ref[cols, :]).astype(BF16)


def _mixer_kernel(xnext_ref, xres_ref, cosb_ref, sinb_ref, coso_ref, sino_ref, g_ref, win_ref, wg_ref,
                  bg_ref, glan_ref, retn_ref, wout_ref, dmat_ref, qdec_ref, kdect_ref, sdec_ref,
                  o_ref, proj_a_ref, proj_b_ref, xn_ref, y_ref, kt_ref, sg_ref, sr_ref, *,
                  n_chunks, n_tiles):
    s = pl.program_id(0)
    prev_tile = jnp.maximum(s - 1, 0)

    @pl.when(s == 0)
    def _():
        proj_b_ref[...] = jnp.zeros_like(proj_b_ref)

    @pl.when(prev_tile % n_tiles == 0)
    def _():
        sg_ref[...] = jnp.zeros_like(sg_ref)
        sr_ref[...] = jnp.zeros_like(sr_ref)

    chunk_pass = functools.partial(
        _chunk_pass, xnext_ref=xnext_ref, g_ref=g_ref, xn_ref=xn_ref,
        rope_refs=(cosb_ref, sinb_ref, coso_ref, sino_ref),
        win_ref=win_ref, wg_ref=wg_ref, bg_ref=bg_ref, glan_ref=glan_ref, retn_ref=retn_ref,
        dmat_ref=dmat_ref, qdec_ref=qdec_ref, kdect_ref=kdect_ref, sdec_ref=sdec_ref, y_ref=y_ref,
        kt_ref=kt_ref, sg_ref=sg_ref, sr_ref=sr_ref, xres_ref=xres_ref, wout_ref=wout_ref,
        o_ref=o_ref, n_chunks=n_chunks)

    @pl.when(s % 2 == 0)
    def _():
        chunk_pass(proj_a_ref, proj_b_ref)

    @pl.when(s % 2 == 1)
    def _():
        chunk_pass(proj_b_ref, proj_a_ref)


def _chunk_pass(cur_ref, prv_ref, *, xnext_ref, g_ref, xn_ref, rope_refs, win_ref, wg_ref, bg_ref, glan_ref,
                retn_ref, dmat_ref, qdec_ref, kdect_ref, sdec_ref, y_ref, kt_ref, sg_ref, sr_ref,
                xres_ref, wout_ref, o_ref, n_chunks):
    ts = n_chunks * CHUNK
    chunk_rows = [slice(c * CHUNK, (c + 1) * CHUNK) for c in range(n_chunks)]
    head_rows = [slice(h * CHUNK, (h + 1) * CHUNK) for h in range(HEADS)]
    head_cols = [slice(h * DV, (h + 1) * DV) for h in range(HEADS)]
    pieces = [(i * PROJ_PIECE, min((i + 1) * PROJ_PIECE, D_PROJ))
              for i in range(-(-D_PROJ // PROJ_PIECE))]

    def emit_proj(n):
        for _ in range(min(n, len(pieces))):
            lo, hi = pieces.pop(0)
            cur_ref[:, lo:hi] = _dot(xn_ref[...], win_ref[:, lo:hi])

    bi = lax.broadcasted_iota(jnp.int32, (CUM_BLOCK, CUM_BLOCK), 0)
    bj = lax.broadcasted_iota(jnp.int32, (CUM_BLOCK, CUM_BLOCK), 1)
    tril_blocks = ((bi // CHUNK == bj // CHUNK) & (bj <= bi)).astype(BF16)
    si = lax.broadcasted_iota(jnp.int32, (HEADS * CHUNK, CHUNK), 0) % CHUNK
    sj = lax.broadcasted_iota(jnp.int32, (HEADS * CHUNK, CHUNK), 1)
    lower = sj <= si
    first_half = (lax.broadcasted_iota(jnp.int32, (1, HK), 1) % DK) < (DK // 2)
    lane_pad = jnp.zeros((HK, CHUNK), BF16)

    def contributions(kdt, v):
        return jnp.concatenate([_dot(kdt[head_rows[h]], v[:, head_cols[h]]) for h in range(HEADS)],
                               axis=0)

    def scan_states(s_ref, decays, contribs):
        state, starts = s_ref[...], []
        for dec, con in zip(decays, contribs):
            starts.append(state.astype(BF16))
            state = state * dec + con
        s_ref[...] = state
        return starts

    xn_ref[...] = _rms(xnext_ref[0], g_ref[...]).astype(BF16)
    z = _dot(prv_ref[:, C_LR:C_LR + GATE_RANK_PAD].astype(BF16), wg_ref[...]) + bg_ref[...]
    emit_proj(STAGE_A_PIECES)
    log_a = (jnp.minimum(z, 0.0) - jnp.log1p(jnp.exp(-jnp.abs(z)))) * (1.0 / GATE_TAU)
    la_hi = log_a.astype(BF16)
    la_lo = (log_a - la_hi.astype(F32)).astype(BF16)
    la_split = jnp.concatenate([la_hi, la_lo], axis=1)
    bcum = []
    for b in range(ts // CUM_BLOCK):
        r = _dot(tril_blocks, la_split[b * CUM_BLOCK:(b + 1) * CUM_BLOCK])
        bcum.append(r[:, :HK] + r[:, HK:])
    bcum = jnp.concatenate(bcum, axis=0)
    emit_proj(1)
    q = prv_ref[:, C_GQ:C_GQ + HK] * (DK ** -0.5)
    qeb = (q * jnp.exp(bcum)).astype(BF16)
    qebi = (q * jnp.exp(-bcum)).astype(BF16)

    g_decay, g_contrib = [], []
    for c, rows in enumerate(chunk_rows):
        kt = prv_ref[rows, C_GK:C_GK + HK].T
        bct = bcum[rows].T
        blt = bct[:, CHUNK - 1:CHUNK]
        kt_ref[0, c] = (kt * jnp.exp(bct)).astype(BF16)
        kt_ref[1, c] = (kt * jnp.exp(-bct)).astype(BF16)
        kdt = (kt * jnp.exp(blt - bct)).astype(BF16)
        g_decay.append(jnp.exp(blt))
        g_contrib.append(contributions(kdt, prv_ref[rows, C_GV:C_GV + HV].astype(BF16)))
        if c % 2 == 1:
            emit_proj(1)

    cosb, sinb, coso, sino = [r[...] for r in rope_refs]
    cosb, sinb = cosb[0], sinb[0]
    cs = cosb * coso - sinb * sino
    sn = sinb * coso + cosb * sino
    sn = jnp.where(first_half, -sn, sn)

    def rope(t):
        partner = jnp.where(first_half, pltpu.roll(t, HK - DK // 2, 1), pltpu.roll(t, DK // 2, 1))
        return t * cs + partner * sn

    qrb = rope(prv_ref[:, C_RQ:C_RQ + HK]).astype(BF16)
    kr = rope(prv_ref[:, C_RK:C_RK + HK]) * (DK ** -0.5)
    r_contrib = []
    for c, rows in enumerate(chunk_rows):
        kt = kr[rows].T
        kt_ref[2, c] = kt.astype(BF16)
        kdt = (kt * kdect_ref[...]).astype(BF16)
        r_contrib.append(contributions(kdt, prv_ref[rows, C_RV:C_RV + HV].astype(BF16)))
        if c % 2 == 1:
            emit_proj(1)

    g_start = scan_states(sg_ref, g_decay, g_contrib)
    r_start = scan_states(sr_ref, [sdec_ref[...]] * n_chunks, r_contrib)

    def gla_scores(c):
        rows = chunk_rows[c]
        r = _dot(_stack_heads(qeb[rows]), jnp.concatenate([g_start[c], kt_ref[1, c], lane_pad], axis=1))
        a_up = _dot(_stack_heads(qebi[rows]), kt_ref[0, c])
        return r[:, :DV], jnp.where(lower, r[:, DV:DV + CHUNK], a_up).astype(BF16)

    def ret_scores(c):
        r = _dot(_stack_heads(qrb[chunk_rows[c]]),
                 jnp.concatenate([r_start[c], kt_ref[2, c], lane_pad], axis=1))
        return r[:, :DV] * qdec_ref[...], (r[:, DV:DV + CHUNK] * dmat_ref[...]).astype(BF16)

    def gla_values(c, inter, attn):
        rows = chunk_rows[c]
        v = prv_ref[rows, C_GV:C_GV + HV].astype(BF16)
        for h in range(HEADS):
            o = _dot(attn[head_rows[h]], v[:, head_cols[h]]) + inter[head_rows[h]]
            gate = prv_ref[rows, C_GG + h * DV:C_GG + (h + 1) * DV]
            yn = o * lax.rsqrt(jnp.mean(o * o, axis=-1, keepdims=True) + EPS) * glan_ref[h:h + 1, :]
            y_ref[rows, h * DV:(h + 1) * DV] = (yn * _swish(gate)).astype(BF16)

    def ret_values(c, inter, scores):
        rows = chunk_rows[c]
        v = prv_ref[rows, C_RV:C_RV + HV].astype(BF16)
        for h in range(HEADS):
            o = _dot(scores[head_rows[h]], v[:, head_cols[h]]) + inter[head_rows[h]]
            gate = prv_ref[rows, C_RG + h * DV:C_RG + (h + 1) * DV]
            d = o - jnp.mean(o, axis=-1, keepdims=True)
            yn = d * lax.rsqrt(jnp.mean(d * d, axis=-1, keepdims=True) + EPS) * retn_ref[h:h + 1, :]
            y_ref[rows, HV + h * DV:HV + (h + 1) * DV] = (yn * _swish(gate)).astype(BF16)

    def out_proj(done):
        o_ref[0, done, :] = xres_ref[0, done, :] + _dot(y_ref[done, :], wout_ref[...])

    finished = None
    ahead = (gla_scores(0), ret_scores(0))
    for c in range(n_chunks):
        gla_ahead, ret_ahead = ahead
        if c + 1 < n_chunks:
            ahead = (gla_scores(c + 1), ret_scores(c + 1))
        if finished is not None:
            out_proj(finished)
            finished = None
        else:
            emit_proj(1)
        gla_values(c, *gla_ahead)
        ret_values(c, *ret_ahead)
        if (c + 1) % (OUT_ROWS // CHUNK) == 0:
            finished = slice((c + 1) * CHUNK - OUT_ROWS, (c + 1) * CHUNK)
    emit_proj(len(pieces))
    out_proj(finished)


def _xattn_mlp_kernel(h_ref, wqk_ref, wvo_ref, gx_ref, gf_ref, w1_ref, w2_ref, gfin_ref, o_ref):
    bounds = np.cumsum((0,) + XATTN_BLOCKS)
    blocks = [slice(int(a), int(b)) for a, b in zip(bounds[:-1], bounds[1:])]
    n_mem = wqk_ref.shape[2] // XATTN_HEADS
    h = [h_ref[0, rows, :] for rows in blocks]
    scores = [_dot(_rms(hb, gx_ref[...]).astype(BF16), wqk_ref[0]) for hb in h]
    probs = []
    for sb in scores:
        heads = []
        for a in range(XATTN_HEADS):
            s = sb[:, a * n_mem:(a + 1) * n_mem]
            e = jnp.exp(s - jnp.max(s, axis=-1, keepdims=True))
            heads.append((e * (1.0 / jnp.sum(e, axis=-1, keepdims=True))).astype(BF16))
        probs.append(jnp.concatenate(heads, axis=1))
    h = [hb + _dot(pb, wvo_ref[0]) for hb, pb in zip(h, probs)]

    hn = [_rms(hb, gf_ref[...]).astype(BF16) for hb in h]
    for c in range(D_FF // FF_BLOCK):
        cols = slice(c * FF_BLOCK, (c + 1) * FF_BLOCK)
        u = [jnp.maximum(_dot(hb, w1_ref[:, cols]), 0.0) for hb in hn]
        h = [hb + _dot((ub * ub).astype(BF16), w2_ref[cols, :]) for hb, ub in zip(h, u)]
    for rows, hb in zip(blocks, h):
        o_ref[0, rows, :] = _rms(hb, gfin_ref[...])


def _const_spec(shape):
    n = len(shape)
    return pl.BlockSpec(shape, lambda *_: (0,) * n, pipeline_mode=pl.Buffered(1))


def _retention_tables():
    gamma = 1.0 - np.exp2(-np.linspace(RET_DECAY_EXP_LO, RET_DECAY_EXP_HI, HEADS, dtype=np.float32))
    log_g = np.log(gamma.astype(np.float32)).astype(np.float32)
    pos = np.arange(CHUNK, dtype=np.float32)
    dmat = np.exp(np.abs(pos[:, None] - pos[None, :])[None] * log_g[:, None, None])
    k_dec = np.exp((CHUNK - 1.0 - pos)[None, :] * log_g[:, None])
    q_dec = np.exp((pos + 1.0)[None, :] * log_g[:, None])
    s_dec = np.exp(CHUNK * log_g)
    dmat_s = dmat.reshape(HEADS * CHUNK, CHUNK)
    qdec_s = np.broadcast_to(q_dec.reshape(HEADS * CHUNK, 1), (HEADS * CHUNK, DV))
    kdect_s = np.repeat(k_dec, DK, axis=0)
    sdec_s = np.broadcast_to(np.repeat(s_dec, DK)[:, None], (HEADS * DK, DV))
    as32 = lambda a: jnp.asarray(np.ascontiguousarray(a), dtype=F32)
    return as32(dmat_s), as32(qdec_s), as32(kdect_s), as32(sdec_s)


def kernel(x, mem, norm_mix_g, w_in, gla_w_gate, gla_b_gate, gla_norm_g, ret_norm_g, w_out, norm_x_g,
           norm_mem_g, xa_w_q, xa_w_kv, xa_w_o, norm_ffn_g, ffn_w1, ffn_w2, final_norm_g):
    B, S, D = x.shape
    n_mem = mem.shape[1]
    assert D == D_MODEL and S % SEQ_TILE == 0 and S % XATTN_TILE == 0 and w_in.shape[0] == 1
    ts = SEQ_TILE
    n_tiles = S // ts

    inv_freq = ROPE_BASE ** (-jnp.arange(0, DK, 2, dtype=F32) / DK)
    inv_lane = jnp.tile(inv_freq, 2 * HEADS)
    ang_off = jnp.arange(ts, dtype=F32)[:, None] * inv_lane[None, :]
    ang_base = (jnp.arange(n_tiles, dtype=F32) * ts)[:, None, None] * inv_lane[None, None, :]
    cos_off, sin_off = jnp.cos(ang_off), jnp.sin(ang_off)
    cos_base, sin_base = jnp.cos(ang_base), jnp.sin(ang_base)

    w_in_r = pl.pallas_call(
        _prep_w_in_kernel,
        grid=(D // PREP_ROWS,),
        in_specs=[pl.BlockSpec((w_in.shape[2], PREP_ROWS), lambda i: (0, i))],
        out_specs=pl.BlockSpec((PREP_ROWS, D_PROJ), lambda i: (i, 0)),
        out_shape=jax.ShapeDtypeStruct((D, D_PROJ), BF16),
        compiler_params=pltpu.CompilerParams(dimension_semantics=("arbitrary",),
                                             vmem_limit_bytes=VMEM_LIMIT_BYTES),
        name="prep_w_in",
    )(jnp.swapaxes(w_in[0], 0, 1))
    w_gate_p = jnp.concatenate(
        [gla_w_gate[0], jnp.zeros((GATE_RANK_PAD - GATE_RANK, HK), F32)], axis=0).astype(BF16)
    dmat_s, qdec_s, kdect_s, sdec_s = _retention_tables()

    params = pltpu.CompilerParams(dimension_semantics=("arbitrary", "arbitrary"),
                                  vmem_limit_bytes=VMEM_LIMIT_BYTES)

    hm = XATTN_HEADS * n_mem
    assert D % B == 0 and D_FF % B == 0
    wqk, wvo, w1_b, w2_b = pl.pallas_call(
        _mem_kv_kernel,
        grid=(B,),
        in_specs=[pl.BlockSpec((1, n_mem, D), lambda b: (b, 0, 0)),
                  _const_spec((1, D)), _const_spec((D, 2 * D)), _const_spec((D, D)),
                  _const_spec((D, D)),
                  pl.BlockSpec((D // B, D_FF), lambda b: (b, 0)),
                  pl.BlockSpec((D_FF // B, D), lambda b: (b, 0))],
        out_specs=[pl.BlockSpec((1, D, hm), lambda b: (b, 0, 0)),
                   pl.BlockSpec((1, hm, D), lambda b: (b, 0, 0)),
                   pl.BlockSpec((D // B, D_FF), lambda b: (b, 0)),
                   pl.BlockSpec((D_FF // B, D), lambda b: (b, 0))],
        out_shape=[jax.ShapeDtypeStruct((B, D, hm), BF16), jax.ShapeDtypeStruct((B, hm, D), BF16),
                   jax.ShapeDtypeStruct((D, D_FF), BF16), jax.ShapeDtypeStruct((D_FF, D), BF16)],
        compiler_params=pltpu.CompilerParams(dimension_semantics=("arbitrary",),
                                             vmem_limit_bytes=VMEM_LIMIT_BYTES),
        name="mem_kv",
    )(mem, norm_mem_g[0][None, :], xa_w_kv[0], xa_w_q[0], xa_w_o[0], ffn_w1[0], ffn_w2[0])

    n_steps = B * n_tiles + 1

    def next_tile(s):
        p = jnp.minimum(s, n_steps - 2)
        return (p // n_tiles, p % n_tiles, 0)

    def prev_tile(s):
        q = jnp.maximum(s - 1, 0)
        return (q // n_tiles, q % n_tiles, 0)

    def prev_rope(s):
        return (jnp.maximum(s - 1, 0) % n_tiles, 0, 0)

    h1 = pl.pallas_call(
        functools.partial(_mixer_kernel, n_chunks=ts // CHUNK, n_tiles=n_tiles),
        grid=(n_steps,),
        in_specs=[pl.BlockSpec((1, ts, D), next_tile),
                  pl.BlockSpec((1, ts, D), prev_tile),
                  pl.BlockSpec((1, 1, HK), prev_rope), pl.BlockSpec((1, 1, HK), prev_rope),
                  _const_spec((ts, HK)), _const_spec((ts, HK)),
                  _const_spec((1, D)), _const_spec((D, D_PROJ)),
                  _const_spec((GATE_RANK_PAD, HK)), _const_spec((1, HK)),
                  _const_spec((HEADS, DV)), _const_spec((HEADS, DV)),
                  _const_spec((D, D)),
                  _const_spec((HEADS * CHUNK, CHUNK)), _const_spec((HEADS * CHUNK, DV)),
                  _const_spec((HEADS * DK, CHUNK)), _const_spec((HEADS * DK, DV))],
        out_specs=pl.BlockSpec((1, ts, D), prev_tile),
        out_shape=jax.ShapeDtypeStruct((B, S, D), F32),
        scratch_shapes=[pltpu.VMEM((ts, D_PROJ), F32), pltpu.VMEM((ts, D_PROJ), F32),
                        pltpu.VMEM((ts, D), BF16), pltpu.VMEM((ts, D), BF16),
                        pltpu.VMEM((3, ts // CHUNK, HK, CHUNK), BF16),
                        pltpu.VMEM((HEADS * DK, DV), F32), pltpu.VMEM((HEADS * DK, DV), F32)],
        compiler_params=pltpu.CompilerParams(dimension_semantics=("arbitrary",),
                                             vmem_limit_bytes=VMEM_LIMIT_BYTES),
        name="mixer",
    )(x, x, cos_base, sin_base, cos_off, sin_off, norm_mix_g[0][None, :], w_in_r, w_gate_p,
      gla_b_gate[0][None, :], gla_norm_g[0], ret_norm_g[0], w_out[0].astype(BF16),
      dmat_s, qdec_s, kdect_s, sdec_s)

    out = pl.pallas_call(
        _xattn_mlp_kernel,
        grid=(B, S // XATTN_TILE),
        in_specs=[pl.BlockSpec((1, XATTN_TILE, D), lambda b, t: (b, t, 0)),
                  pl.BlockSpec((1, D, hm), lambda b, t: (b, 0, 0)),
                  pl.BlockSpec((1, hm, D), lambda b, t: (b, 0, 0)),
                  _const_spec((1, D)),
                  _const_spec((1, D)), _const_spec((D, D_FF)), _const_spec((D_FF, D)),
                  _const_spec((1, D))],
        out_specs=pl.BlockSpec((1, XATTN_TILE, D), lambda b, t: (b, t, 0)),
        out_shape=jax.ShapeDtypeStruct((B, S, D), F32),
        compiler_params=params,
        name="xattn_mlp",
    )(h1, wqk, wvo, norm_x_g[0][None, :], norm_ffn_g[0][None, :], w1_b, w2_b,
      final_norm_g[None, :])
    return out
```

```python
import functools

import numpy as np
import jax
import jax.numpy as jnp
from jax import lax
from jax.experimental import pallas as pl
from jax.experimental.pallas import tpu as pltpu

D_MODEL = 1024
CHUNK = 64
HEADS = 4
DK = 64
DV = 128
GATE_RANK = 16
GATE_RANK_PAD = 128
GATE_TAU = 16.0
RET_DECAY_EXP_LO = 5.0
RET_DECAY_EXP_HI = 12.0
ROPE_BASE = 10000.0
XATTN_HEADS = 4
XATTN_DH = D_MODEL // XATTN_HEADS
D_FF = 4 * D_MODEL
FF_BLOCK = 1024
XATTN_TILE = 512
XATTN_BLOCKS = (256, 256)
PREP_ROWS = 256
EPS = 1e-6

HK = HEADS * DK
HV = HEADS * DV
C_GQ, C_GK, C_GV, C_GG = 0, HK, 2 * HK, 2 * HK + HV
C_RQ = C_GG + HV
C_RK = C_RQ + HK
C_RV = C_RK + HK
C_RG = C_RV + HV
C_LR = C_RG + HV
D_PROJ = C_LR + GATE_RANK_PAD

SEQ_TILE = 512
PROJ_PIECE = 256
TAIL_PIECES = 2
STAGE_A_PIECES = 3
CUM_BLOCK = 256
OUT_ROWS = 512
VMEM_LIMIT_BYTES = 56 * 1024 * 1024

F32 = jnp.float32
BF16 = jnp.bfloat16


def _dot(a, b):
    return jnp.dot(a, b, preferred_element_type=F32)


def _dot_nt(a, b):
    return lax.dot_general(a, b, (((1,), (1,)), ((), ())), preferred_element_type=F32)


def _rms(x, g):
    return x * lax.rsqrt(jnp.mean(x * x, axis=-1, keepdims=True) + EPS) * g


def _swish(g):
    return g / (1.0 + jnp.exp(-g))


def _stack_heads(t):
    lane_head = lax.broadcasted_iota(jnp.int32, t.shape, 1) // DK
    zero = jnp.zeros_like(t)
    return jnp.concatenate([jnp.where(lane_head == h, t, zero) for h in range(HEADS)], axis=0)


def _prep_w_in_kernel(wt_ref, o_ref):
    lr0 = 2 * HK + HV
    o_ref[:, :lr0] = wt_ref[:lr0, :].T.astype(BF16)
    o_ref[:, lr0:C_LR] = wt_ref[lr0 + GATE_RANK:, :].T.astype(BF16)
    tail = wt_ref[lr0:lr0 + GATE_RANK_PAD, :]
    row = lax.broadcasted_iota(jnp.int32, tail.shape, 0)
    o_ref[:, C_LR:] = jnp.where(row < GATE_RANK, tail, 0.0).T.astype(BF16)


def _mem_kv_kernel(mem_ref, g_ref, wkv_ref, wq_ref, wo_ref, wqk_ref, wvo_ref):
    n_mem = mem_ref.shape[1]
    mn = _rms(mem_ref[0], g_ref[...])
    kv = _dot(mn, wkv_ref[...])
    for h in range(XATTN_HEADS):
        cols = slice(h * XATTN_DH, (h + 1) * XATTN_DH)
        mcols = slice(h * n_mem, (h + 1) * n_mem)
        k_h = kv[:, cols]
        v_h = kv[:, D_MODEL + h * XATTN_DH:D_MODEL + (h + 1) * XATTN_DH]
        wqk_ref[0, :, mcols] = (_dot_nt(wq_ref[:, cols], k_h) * (XATTN_DH ** -0.5)).astype(BF16)
        wvo_ref[0, mcols, :] = _dot(v_h, wo_ref[cols, :]).astype(BF16)


def _mixer_kernel(xnext_ref, xres_ref, cosb_ref, sinb_ref, coso_ref, sino_ref, g_ref, win_ref, wg_ref,
                  bg_ref, glan_ref, retn_ref, wout_ref, dmat_ref, qdec_ref, kdect_ref, sdec_ref,
                  o_ref, proj_a_ref, proj_b_ref, xn_ref, y_a_ref, y_b_ref, kt_ref, sg_ref, sr_ref, *,
                  n_chunks, n_tiles, n_total):
    s = pl.program_id(0)
    prev_tile = jnp.clip(s - 1, 0, n_total - 1)

    @pl.when(s == 0)
    def _():
        proj_b_ref[...] = jnp.zeros_like(proj_b_ref)
        y_b_ref[...] = jnp.zeros_like(y_b_ref)

    @pl.when(prev_tile % n_tiles == 0)
    def _():
        sg_ref[...] = jnp.zeros_like(sg_ref)
        sr_ref[...] = jnp.zeros_like(sr_ref)

    chunk_pass = functools.partial(
        _chunk_pass, xnext_ref=xnext_ref, g_ref=g_ref, xn_ref=xn_ref,
        rope_refs=(cosb_ref, sinb_ref, coso_ref, sino_ref),
        win_ref=win_ref, wg_ref=wg_ref, bg_ref=bg_ref, glan_ref=glan_ref, retn_ref=retn_ref,
        dmat_ref=dmat_ref, qdec_ref=qdec_ref, kdect_ref=kdect_ref, sdec_ref=sdec_ref,
        kt_ref=kt_ref, sg_ref=sg_ref, sr_ref=sr_ref, xres_ref=xres_ref, wout_ref=wout_ref,
        o_ref=o_ref, n_chunks=n_chunks)

    @pl.when(s % 2 == 0)
    def _():
        chunk_pass(proj_a_ref, proj_b_ref, y_a_ref, y_b_ref)

    @pl.when(s % 2 == 1)
    def _():
        chunk_pass(proj_b_ref, proj_a_ref, y_b_ref, y_a_ref)


def _chunk_pass(cur_ref, prv_ref, y_ref, y_prev_ref, *, xnext_ref, g_ref, xn_ref, rope_refs, win_ref, wg_ref, bg_ref, glan_ref,
                retn_ref, dmat_ref, qdec_ref, kdect_ref, sdec_ref, kt_ref, sg_ref, sr_ref,
                xres_ref, wout_ref, o_ref, n_chunks):
    ts = n_chunks * CHUNK
    chunk_rows = [slice(c * CHUNK, (c + 1) * CHUNK) for c in range(n_chunks)]
    head_rows = [slice(h * CHUNK, (h + 1) * CHUNK) for h in range(HEADS)]
    head_cols = [slice(h * DV, (h + 1) * DV) for h in range(HEADS)]
    pieces = [(i * PROJ_PIECE, min((i + 1) * PROJ_PIECE, D_PROJ))
              for i in range(-(-D_PROJ // PROJ_PIECE))]

    def emit_proj(n, keep=TAIL_PIECES):
        for _ in range(min(n, len(pieces) - keep)):
            lo, hi = pieces.pop(0)
            cur_ref[:, lo:hi] = _dot(xn_ref[...], win_ref[:, lo:hi])

    bi = lax.broadcasted_iota(jnp.int32, (CUM_BLOCK, CUM_BLOCK), 0)
    bj = lax.broadcasted_iota(jnp.int32, (CUM_BLOCK, CUM_BLOCK), 1)
    tril_blocks = ((bi // CHUNK == bj // CHUNK) & (bj <= bi)).astype(BF16)
    si = lax.broadcasted_iota(jnp.int32, (HEADS * CHUNK, CHUNK), 0) % CHUNK
    sj = lax.broadcasted_iota(jnp.int32, (HEADS * CHUNK, CHUNK), 1)
    lower = sj <= si
    first_half = (lax.broadcasted_iota(jnp.int32, (1, HK), 1) % DK) < (DK // 2)
    lane_pad = jnp.zeros((HK, CHUNK), BF16)

    def contributions(kdt, v):
        return jnp.concatenate([_dot(kdt[head_rows[h]], v[:, head_cols[h]]) for h in range(HEADS)],
                               axis=0)

    def scan_states(s_ref, decays, contribs):
        state, starts = s_ref[...], []
        for dec, con in zip(decays, contribs):
            starts.append(state.astype(BF16))
            state = state * dec + con
        s_ref[...] = state
        return starts

    o_ref[0] = xres_ref[0] + _dot(y_prev_ref[...], wout_ref[...])

    xn_ref[...] = _rms(xnext_ref[0], g_ref[...]).astype(BF16)
    z = _dot(prv_ref[:, C_LR:C_LR + GATE_RANK_PAD].astype(BF16), wg_ref[...]) + bg_ref[...]
    emit_proj(STAGE_A_PIECES)
    log_a = (jnp.minimum(z, 0.0) - jnp.log1p(jnp.exp(-jnp.abs(z)))) * (1.0 / GATE_TAU)
    la_hi = log_a.astype(BF16)
    la_lo = (log_a - la_hi.astype(F32)).astype(BF16)
    la_split = jnp.concatenate([la_hi, la_lo], axis=1)
    bcum = []
    for b in range(ts // CUM_BLOCK):
        r = _dot(tril_blocks, la_split[b * CUM_BLOCK:(b + 1) * CUM_BLOCK])
        bcum.append(r[:, :HK] + r[:, HK:])
    bcum = jnp.concatenate(bcum, axis=0)
    emit_proj(1)
    q = prv_ref[:, C_GQ:C_GQ + HK] * (DK ** -0.5)
    qeb = (q * jnp.exp(bcum)).astype(BF16)
    qebi = (q * jnp.exp(-bcum)).astype(BF16)

    g_decay, g_contrib = [], []
    for c, rows in enumerate(chunk_rows):
        kt = prv_ref[rows, C_GK:C_GK + HK].T
        bct = bcum[rows].T
        blt = bct[:, CHUNK - 1:CHUNK]
        kt_ref[0, c] = (kt * jnp.exp(bct)).astype(BF16)
        kt_ref[1, c] = (kt * jnp.exp(-bct)).astype(BF16)
        kdt = (kt * jnp.exp(blt - bct)).astype(BF16)
        g_decay.append(jnp.exp(blt))
        g_contrib.append(contributions(kdt, prv_ref[rows, C_GV:C_GV + HV].astype(BF16)))
        if c % 2 == 1:
            emit_proj(1)

    cosb, sinb, coso, sino = [r[...] for r in rope_refs]
    cosb, sinb = cosb[0], sinb[0]
    cs = cosb * coso - sinb * sino
    sn = sinb * coso + cosb * sino
    sn = jnp.where(first_half, -sn, sn)

    def rope(t):
        partner = jnp.where(first_half, pltpu.roll(t, HK - DK // 2, 1), pltpu.roll(t, DK // 2, 1))
        return t * cs + partner * sn

    qrb = rope(prv_ref[:, C_RQ:C_RQ + HK]).astype(BF16)
    kr = rope(prv_ref[:, C_RK:C_RK + HK]) * (DK ** -0.5)
    r_contrib = []
    for c, rows in enumerate(chunk_rows):
        kt = kr[rows].T
        kt_ref[2, c] = kt.astype(BF16)
        kdt = (kt * kdect_ref[...]).astype(BF16)
        r_contrib.append(contributions(kdt, prv_ref[rows, C_RV:C_RV + HV].astype(BF16)))
        if c % 2 == 1:
            emit_proj(1)

    g_start = scan_states(sg_ref, g_decay, g_contrib)
    r_start = scan_states(sr_ref, [sdec_ref[...]] * n_chunks, r_contrib)

    def gla_scores(c):
        rows = chunk_rows[c]
        r = _dot(_stack_heads(qeb[rows]), jnp.concatenate([g_start[c], kt_ref[1, c], lane_pad], axis=1))
        a_up = _dot(_stack_heads(qebi[rows]), kt_ref[0, c])
        return r[:, :DV], jnp.where(lower, r[:, DV:DV + CHUNK], a_up).astype(BF16)

    def ret_scores(c):
        r = _dot(_stack_heads(qrb[chunk_rows[c]]),
                 jnp.concatenate([r_start[c], kt_ref[2, c], lane_pad], axis=1))
        return r[:, :DV] * qdec_ref[...], (r[:, DV:DV + CHUNK] * dmat_ref[...]).astype(BF16)

    def gla_values(c, inter, attn):
        rows = chunk_rows[c]
        v = prv_ref[rows, C_GV:C_GV + HV].astype(BF16)
        for h in range(HEADS):
            o = _dot(attn[head_rows[h]], v[:, head_cols[h]]) + inter[head_rows[h]]
            gate = prv_ref[rows, C_GG + h * DV:C_GG + (h + 1) * DV]
            yn = o * lax.rsqrt(jnp.mean(o * o, axis=-1, keepdims=True) + EPS) * glan_ref[h:h + 1, :]
            y_ref[rows, h * DV:(h + 1) * DV] = (yn * _swish(gate)).astype(BF16)

    def ret_values(c, inter, scores):
        rows = chunk_rows[c]
        v = prv_ref[rows, C_RV:C_RV + HV].astype(BF16)
        for h in range(HEADS):
            o = _dot(scores[head_rows[h]], v[:, head_cols[h]]) + inter[head_rows[h]]
            gate = prv_ref[rows, C_RG + h * DV:C_RG + (h + 1) * DV]
            d = o - jnp.mean(o, axis=-1, keepdims=True)
            yn = d * lax.rsqrt(jnp.mean(d * d, axis=-1, keepdims=True) + EPS) * retn_ref[h:h + 1, :]
            y_ref[rows, HV + h * DV:HV + (h + 1) * DV] = (yn * _swish(gate)).astype(BF16)

    ahead = (gla_scores(0), ret_scores(0))
    for c in range(n_chunks):
        gla_ahead, ret_ahead = ahead
        if c + 1 < n_chunks:
            ahead = (gla_scores(c + 1), ret_scores(c + 1))
        emit_proj(1)
        gla_values(c, *gla_ahead)
        ret_values(c, *ret_ahead)
    emit_proj(len(pieces), keep=0)


def _xattn_mlp_kernel(h_ref, wqk_ref, wvo_ref, gx_ref, gf_ref, w1_ref, w2_ref, gfin_ref, o_ref):
    bounds = np.cumsum((0,) + XATTN_BLOCKS)
    blocks = [slice(int(a), int(b)) for a, b in zip(bounds[:-1], bounds[1:])]
    n_mem = wqk_ref.shape[2] // XATTN_HEADS
    h = [h_ref[0, rows, :] for rows in blocks]
    scores = [_dot(_rms(hb, gx_ref[...]).astype(BF16), wqk_ref[0]) for hb in h]
    probs = []
    for sb in scores:
        heads = []
        for a in range(XATTN_HEADS):
            s = sb[:, a * n_mem:(a + 1) * n_mem]
            e = jnp.exp(s - jnp.max(s, axis=-1, keepdims=True))
            heads.append((e * (1.0 / jnp.sum(e, axis=-1, keepdims=True))).astype(BF16))
        probs.append(jnp.concatenate(heads, axis=1))
    h = [hb + _dot(pb, wvo_ref[0]) for hb, pb in zip(h, probs)]

    hn = [_rms(hb, gf_ref[...]).astype(BF16) for hb in h]
    for c in range(D_FF // FF_BLOCK):
        cols = slice(c * FF_BLOCK, (c + 1) * FF_BLOCK)
        u = [jnp.maximum(_dot(hb, w1_ref[:, cols]), 0.0) for hb in hn]
        h = [hb + _dot((ub * ub).astype(BF16), w2_ref[cols, :]) for hb, ub in zip(h, u)]
    for rows, hb in zip(blocks, h):
        o_ref[0, rows, :] = _rms(hb, gfin_ref[...])


def _const_spec(shape):
    n = len(shape)
    return pl.BlockSpec(shape, lambda *_: (0,) * n, pipeline_mode=pl.Buffered(1))


def _retention_tables():
    gamma = 1.0 - np.exp2(-np.linspace(RET_DECAY_EXP_LO, RET_DECAY_EXP_HI, HEADS, dtype=np.float32))
    log_g = np.log(gamma.astype(np.float32)).astype(np.float32)
    pos = np.arange(CHUNK, dtype=np.float32)
    dmat = np.exp(np.abs(pos[:, None] - pos[None, :])[None] * log_g[:, None, None])
    k_dec = np.exp((CHUNK - 1.0 - pos)[None, :] * log_g[:, None])
    q_dec = np.exp((pos + 1.0)[None, :] * log_g[:, None])
    s_dec = np.exp(CHUNK * log_g)
    dmat_s = dmat.reshape(HEADS * CHUNK, CHUNK)
    qdec_s = np.broadcast_to(q_dec.reshape(HEADS * CHUNK, 1), (HEADS * CHUNK, DV))
    kdect_s = np.repeat(k_dec, DK, axis=0)
    sdec_s = np.broadcast_to(np.repeat(s_dec, DK)[:, None], (HEADS * DK, DV))
    as32 = lambda a: jnp.asarray(np.ascontiguousarray(a), dtype=F32)
    return as32(dmat_s), as32(qdec_s), as32(kdect_s), as32(sdec_s)


def kernel(x, mem, norm_mix_g, w_in, gla_w_gate, gla_b_gate, gla_norm_g, ret_norm_g, w_out, norm_x_g,
           norm_mem_g, xa_w_q, xa_w_kv, xa_w_o, norm_ffn_g, ffn_w1, ffn_w2, final_norm_g):
    B, S, D = x.shape
    n_mem = mem.shape[1]
    assert D == D_MODEL and S % SEQ_TILE == 0 and S % XATTN_TILE == 0 and w_in.shape[0] == 1
    ts = SEQ_TILE
    n_tiles = S // ts

    inv_freq = ROPE_BASE ** (-jnp.arange(0, DK, 2, dtype=F32) / DK)
    inv_lane = jnp.tile(inv_freq, 2 * HEADS)
    ang_off = jnp.arange(ts, dtype=F32)[:, None] * inv_lane[None, :]
    ang_base = (jnp.arange(n_tiles, dtype=F32) * ts)[:, None, None] * inv_lane[None, None, :]
    cos_off, sin_off = jnp.cos(ang_off), jnp.sin(ang_off)
    cos_base, sin_base = jnp.cos(ang_base), jnp.sin(ang_base)

    w_in_r = pl.pallas_call(
        _prep_w_in_kernel,
        grid=(D // PREP_ROWS,),
        in_specs=[pl.BlockSpec((w_in.shape[2], PREP_ROWS), lambda i: (0, i))],
        out_specs=pl.BlockSpec((PREP_ROWS, D_PROJ), lambda i: (i, 0)),
        out_shape=jax.ShapeDtypeStruct((D, D_PROJ), BF16),
        compiler_params=pltpu.CompilerParams(dimension_semantics=("arbitrary",),
                                             vmem_limit_bytes=VMEM_LIMIT_BYTES),
        name="prep_w_in",
    )(jnp.swapaxes(w_in[0], 0, 1))
    w_gate_p = jnp.concatenate(
        [gla_w_gate[0], jnp.zeros((GATE_RANK_PAD - GATE_RANK, HK), F32)], axis=0).astype(BF16)
    dmat_s, qdec_s, kdect_s, sdec_s = _retention_tables()

    params = pltpu.CompilerParams(dimension_semantics=("arbitrary", "arbitrary"),
                                  vmem_limit_bytes=VMEM_LIMIT_BYTES)

    hm = XATTN_HEADS * n_mem
    wqk, wvo = pl.pallas_call(
        _mem_kv_kernel,
        grid=(B,),
        in_specs=[pl.BlockSpec((1, n_mem, D), lambda b: (b, 0, 0)),
                  _const_spec((1, D)), _const_spec((D, 2 * D)), _const_spec((D, D)),
                  _const_spec((D, D))],
        out_specs=[pl.BlockSpec((1, D, hm), lambda b: (b, 0, 0)),
                   pl.BlockSpec((1, hm, D), lambda b: (b, 0, 0))],
        out_shape=[jax.ShapeDtypeStruct((B, D, hm), BF16), jax.ShapeDtypeStruct((B, hm, D), BF16)],
        compiler_params=pltpu.CompilerParams(dimension_semantics=("arbitrary",),
                                             vmem_limit_bytes=VMEM_LIMIT_BYTES),
        name="mem_kv",
    )(mem, norm_mem_g[0][None, :], xa_w_kv[0], xa_w_q[0], xa_w_o[0])

    n_total = B * n_tiles
    n_steps = n_total + 2

    def next_tile(s):
        p = jnp.minimum(s, n_total - 1)
        return (p // n_tiles, p % n_tiles, 0)

    def out_tile(s):
        q = jnp.clip(s - 2, 0, n_total - 1)
        return (q // n_tiles, q % n_tiles, 0)

    def prev_rope(s):
        return (jnp.clip(s - 1, 0, n_total - 1) % n_tiles, 0, 0)

    h1 = pl.pallas_call(
        functools.partial(_mixer_kernel, n_chunks=ts // CHUNK, n_tiles=n_tiles, n_total=n_total),
        grid=(n_steps,),
        in_specs=[pl.BlockSpec((1, ts, D), next_tile),
                  pl.BlockSpec((1, ts, D), out_tile),
                  pl.BlockSpec((1, 1, HK), prev_rope), pl.BlockSpec((1, 1, HK), prev_rope),
                  _const_spec((ts, HK)), _const_spec((ts, HK)),
                  _const_spec((1, D)), _const_spec((D, D_PROJ)),
                  _const_spec((GATE_RANK_PAD, HK)), _const_spec((1, HK)),
                  _const_spec((HEADS, DV)), _const_spec((HEADS, DV)),
                  _const_spec((D, D)),
                  _const_spec((HEADS * CHUNK, CHUNK)), _const_spec((HEADS * CHUNK, DV)),
                  _const_spec((HEADS * DK, CHUNK)), _const_spec((HEADS * DK, DV))],
        out_specs=pl.BlockSpec((1, ts, D), out_tile),
        out_shape=jax.ShapeDtypeStruct((B, S, D), F32),
        scratch_shapes=[pltpu.VMEM((ts, D_PROJ), F32), pltpu.VMEM((ts, D_PROJ), F32),
                        pltpu.VMEM((ts, D), BF16), pltpu.VMEM((ts, D), BF16), pltpu.VMEM((ts, D), BF16),
                        pltpu.VMEM((3, ts // CHUNK, HK, CHUNK), BF16),
                        pltpu.VMEM((HEADS * DK, DV), F32), pltpu.VMEM((HEADS * DK, DV), F32)],
        compiler_params=pltpu.CompilerParams(dimension_semantics=("arbitrary",),
                                             vmem_limit_bytes=VMEM_LIMIT_BYTES),
        name="mixer",
    )(x, x, cos_base, sin_base, cos_off, sin_off, norm_mix_g[0][None, :], w_in_r, w_gate_p,
      gla_b_gate[0][None, :], gla_norm_g[0], ret_norm_g[0], w_out[0].astype(BF16),
      dmat_s, qdec_s, kdect_s, sdec_s)

    out = pl.pallas_call(
        _xattn_mlp_kernel,
        grid=(B, S // XATTN_TILE),
        in_specs=[pl.BlockSpec((1, XATTN_TILE, D), lambda b, t: (b, t, 0)),
                  pl.BlockSpec((1, D, hm), lambda b, t: (b, 0, 0)),
                  pl.BlockSpec((1, hm, D), lambda b, t: (b, 0, 0)),
                  _const_spec((1, D)),
                  _const_spec((1, D)), _const_spec((D, D_FF)), _const_spec((D_FF, D)),
                  _const_spec((1, D))],
        out_specs=pl.BlockSpec((1, XATTN_TILE, D), lambda b, t: (b, t, 0)),
        out_shape=jax.ShapeDtypeStruct((B, S, D), F32),
        compiler_params=params,
        name="xattn_mlp",
    )(h1, wqk, wvo, norm_x_g[0][None, :], norm_ffn_g[0][None, :], ffn_w1[0].astype(BF16),
      ffn_w2[0].astype(BF16), final_norm_g[None, :])
    return out
```

```python
import functools

import numpy as np
import jax
import jax.numpy as jnp
from jax import lax
from jax.experimental import pallas as pl
from jax.experimental.pallas import tpu as pltpu

D_MODEL = 1024
CHUNK = 64
HEADS = 4
DK = 64
DV = 128
GATE_RANK = 16
GATE_RANK_PAD = 128
GATE_TAU = 16.0
RET_DECAY_EXP_LO = 5.0
RET_DECAY_EXP_HI = 12.0
ROPE_BASE = 10000.0
XATTN_HEADS = 4
XATTN_DH = D_MODEL // XATTN_HEADS
D_FF = 4 * D_MODEL
FF_BLOCK = 1024
XATTN_TILE = 512
XATTN_BLOCKS = (256, 256)
PREP_ROWS = 256
EPS = 1e-6

HK = HEADS * DK
HV = HEADS * DV
C_GQ, C_GK, C_GV, C_GG = 0, HK, 2 * HK, 2 * HK + HV
C_RQ = C_GG + HV
C_RK = C_RQ + HK
C_RV = C_RK + HK
C_RG = C_RV + HV
C_LR = C_RG + HV
D_PROJ = C_LR + GATE_RANK_PAD

SEQ_TILE = 512
PROJ_PIECE = 256
STAGE_A_PIECES = 5
CUM_BLOCK = 256
OUT_ROWS = 512
VMEM_LIMIT_BYTES = 56 * 1024 * 1024

F32 = jnp.float32
BF16 = jnp.bfloat16


def _dot(a, b):
    return jnp.dot(a, b, preferred_element_type=F32)


def _dot_nt(a, b):
    return lax.dot_general(a, b, (((1,), (1,)), ((), ())), preferred_element_type=F32)


def _rms(x, g):
    return x * lax.rsqrt(jnp.mean(x * x, axis=-1, keepdims=True) + EPS) * g


def _swish(g):
    return g / (1.0 + jnp.exp(-g))


def _stack_heads(t):
    lane_head = lax.broadcasted_iota(jnp.int32, t.shape, 1) // DK
    return jnp.concatenate([t * (lane_head == h).astype(t.dtype) for h in range(HEADS)], axis=0)


def _prep_w_in_kernel(wt_ref, o_ref):
    lr0 = 2 * HK + HV
    o_ref[:, :lr0] = wt_ref[:lr0, :].T.astype(BF16)
    o_ref[:, lr0:C_LR] = wt_ref[lr0 + GATE_RANK:, :].T.astype(BF16)
    tail = wt_ref[lr0:lr0 + GATE_RANK_PAD, :]
    row = lax.broadcasted_iota(jnp.int32, tail.shape, 0)
    o_ref[:, C_LR:] = jnp.where(row < GATE_RANK, tail, 0.0).T.astype(BF16)


def _mem_kv_kernel(mem_ref, g_ref, wkv_ref, wq_ref, wo_ref, wqk_ref, wvo_ref):
    n_mem = mem_ref.shape[1]
    mn = _rms(mem_ref[0], g_ref[...])
    kv = _dot(mn, wkv_ref[...])
    for h in range(XATTN_HEADS):
        cols = slice(h * XATTN_DH, (h + 1) * XATTN_DH)
        mcols = slice(h * n_mem, (h + 1) * n_mem)
        k_h = kv[:, cols]
        v_h = kv[:, D_MODEL + h * XATTN_DH:D_MODEL + (h + 1) * XATTN_DH]
        wqk_ref[0, :, mcols] = (_dot_nt(wq_ref[:, cols], k_h) * (XATTN_DH ** -0.5)).astype(BF16)
        wvo_ref[0, mcols, :] = _dot(v_h, wo_ref[cols, :]).astype(BF16)


def _mixer_kernel(xnext_ref, xres_ref, cosb_ref, sinb_ref, coso_ref, sino_ref, g_ref, win_ref, wg_ref,
                  bg_ref, glan_ref, retn_ref, wout_ref, dmat_ref, qdec_ref, kdect_ref, sdec_ref,
                  o_ref, proj_a_ref, proj_b_ref, xn_ref, y_ref, kt_ref, sg_ref, sr_ref, *,
                  n_chunks, n_tiles):
    s = pl.program_id(0)
    prev_tile = jnp.maximum(s - 1, 0)

    @pl.when(s == 0)
    def _():
        proj_b_ref[...] = jnp.zeros_like(proj_b_ref)

    @pl.when(prev_tile % n_tiles == 0)
    def _():
        sg_ref[...] = jnp.zeros_like(sg_ref)
        sr_ref[...] = jnp.zeros_like(sr_ref)

    chunk_pass = functools.partial(
        _chunk_pass, xnext_ref=xnext_ref, g_ref=g_ref, xn_ref=xn_ref,
        rope_refs=(cosb_ref, sinb_ref, coso_ref, sino_ref),
        win_ref=win_ref, wg_ref=wg_ref, bg_ref=bg_ref, glan_ref=glan_ref, retn_ref=retn_ref,
        dmat_ref=dmat_ref, qdec_ref=qdec_ref, kdect_ref=kdect_ref, sdec_ref=sdec_ref, y_ref=y_ref,
        kt_ref=kt_ref, sg_ref=sg_ref, sr_ref=sr_ref, xres_ref=xres_ref, wout_ref=wout_ref,
        o_ref=o_ref, n_chunks=n_chunks)

    @pl.when(s % 2 == 0)
    def _():
        chunk_pass(proj_a_ref, proj_b_ref)

    @pl.when(s % 2 == 1)
    def _():
        chunk_pass(proj_b_ref, proj_a_ref)


def _chunk_pass(cur_ref, prv_ref, *, xnext_ref, g_ref, xn_ref, rope_refs, win_ref, wg_ref, bg_ref, glan_ref,
                retn_ref, dmat_ref, qdec_ref, kdect_ref, sdec_ref, y_ref, kt_ref, sg_ref, sr_ref,
                xres_ref, wout_ref, o_ref, n_chunks):
    ts = n_chunks * CHUNK
    chunk_rows = [slice(c * CHUNK, (c + 1) * CHUNK) for c in range(n_chunks)]
    head_rows = [slice(h * CHUNK, (h + 1) * CHUNK) for h in range(HEADS)]
    head_cols = [slice(h * DV, (h + 1) * DV) for h in range(HEADS)]
    pieces = [(i * PROJ_PIECE, min((i + 1) * PROJ_PIECE, D_PROJ))
              for i in range(-(-D_PROJ // PROJ_PIECE))]

    def emit_proj(n):
        for _ in range(min(n, len(pieces))):
            lo, hi = pieces.pop(0)
            cur_ref[:, lo:hi] = _dot(xn_ref[...], win_ref[:, lo:hi])

    bi = lax.broadcasted_iota(jnp.int32, (CUM_BLOCK, CUM_BLOCK), 0)
    bj = lax.broadcasted_iota(jnp.int32, (CUM_BLOCK, CUM_BLOCK), 1)
    tril_blocks = ((bi // CHUNK == bj // CHUNK) & (bj <= bi)).astype(BF16)
    si = lax.broadcasted_iota(jnp.int32, (HEADS * CHUNK, CHUNK), 0) % CHUNK
    sj = lax.broadcasted_iota(jnp.int32, (HEADS * CHUNK, CHUNK), 1)
    lower = sj <= si
    first_half = (lax.broadcasted_iota(jnp.int32, (1, HK), 1) % DK) < (DK // 2)
    lane_pad = jnp.zeros((HK, CHUNK), BF16)

    def contributions(kdt, v):
        return jnp.concatenate([_dot(kdt[head_rows[h]], v[:, head_cols[h]]) for h in range(HEADS)],
                               axis=0)

    def scan_states(s_ref, decays, contribs):
        state, starts = s_ref[...], []
        for dec, con in zip(decays, contribs):
            starts.append(state.astype(BF16))
            state = state * dec + con
        s_ref[...] = state
        return starts

    xn_ref[...] = _rms(xnext_ref[0], g_ref[...]).astype(BF16)
    z = _dot(prv_ref[:, C_LR:C_LR + GATE_RANK_PAD].astype(BF16), wg_ref[...]) + bg_ref[...]
    emit_proj(STAGE_A_PIECES)
    log_a = (jnp.minimum(z, 0.0) - jnp.log1p(jnp.exp(-jnp.abs(z)))) * (1.0 / GATE_TAU)
    la_hi = log_a.astype(BF16)
    la_lo = (log_a - la_hi.astype(F32)).astype(BF16)
    la_split = jnp.concatenate([la_hi, la_lo], axis=1)
    bcum = []
    for b in range(ts // CUM_BLOCK):
        r = _dot(tril_blocks, la_split[b * CUM_BLOCK:(b + 1) * CUM_BLOCK])
        bcum.append(r[:, :HK] + r[:, HK:])
    bcum = jnp.concatenate(bcum, axis=0)
    emit_proj(1)
    q = prv_ref[:, C_GQ:C_GQ + HK] * (DK ** -0.5)
    qeb = (q * jnp.exp(bcum)).astype(BF16)
    qebi = (q * jnp.exp(-bcum)).astype(BF16)

    g_decay, g_contrib = [], []
    for c, rows in enumerate(chunk_rows):
        kt = prv_ref[rows, C_GK:C_GK + HK].T
        bct = bcum[rows].T
        blt = bct[:, CHUNK - 1:CHUNK]
        kt_ref[0, c] = (kt * jnp.exp(bct)).astype(BF16)
        kt_ref[1, c] = (kt * jnp.exp(-bct)).astype(BF16)
        kdt = (kt * jnp.exp(blt - bct)).astype(BF16)
        g_decay.append(jnp.exp(blt))
        g_contrib.append(contributions(kdt, prv_ref[rows, C_GV:C_GV + HV].astype(BF16)))
        if c % 2 == 1:
            emit_proj(1)

    cosb, sinb, coso, sino = [r[...] for r in rope_refs]
    cosb, sinb = cosb[0], sinb[0]
    cs = cosb * coso - sinb * sino
    sn = sinb * coso + cosb * sino
    sn = jnp.where(first_half, -sn, sn)

    def rope(t):
        partner = jnp.where(first_half, pltpu.roll(t, HK - DK // 2, 1), pltpu.roll(t, DK // 2, 1))
        return t * cs + partner * sn

    qrb = rope(prv_ref[:, C_RQ:C_RQ + HK]).astype(BF16)
    kr = rope(prv_ref[:, C_RK:C_RK + HK]) * (DK ** -0.5)
    r_contrib = []
    for c, rows in enumerate(chunk_rows):
        kt = kr[rows].T
        kt_ref[2, c] = kt.astype(BF16)
        kdt = (kt * kdect_ref[...]).astype(BF16)
        r_contrib.append(contributions(kdt, prv_ref[rows, C_RV:C_RV + HV].astype(BF16)))
        if c % 2 == 1:
            emit_proj(1)

    g_start = scan_states(sg_ref, g_decay, g_contrib)
    r_start = scan_states(sr_ref, [sdec_ref[...]] * n_chunks, r_contrib)

    def gla_scores(c):
        rows = chunk_rows[c]
        r = _dot(_stack_heads(qeb[rows]), jnp.concatenate([g_start[c], kt_ref[1, c], lane_pad], axis=1))
        a_up = _dot(_stack_heads(qebi[rows]), kt_ref[0, c])
        return r[:, :DV], jnp.where(lower, r[:, DV:DV + CHUNK], a_up).astype(BF16)

    def ret_scores(c):
        r = _dot(_stack_heads(qrb[chunk_rows[c]]),
                 jnp.concatenate([r_start[c], kt_ref[2, c], lane_pad], axis=1))
        return r[:, :DV] * qdec_ref[...], (r[:, DV:DV + CHUNK] * dmat_ref[...]).astype(BF16)

    def gla_values(c, inter, attn):
        rows = chunk_rows[c]
        v = prv_ref[rows, C_GV:C_GV + HV].astype(BF16)
        for h in range(HEADS):
            o = _dot(attn[head_rows[h]], v[:, head_cols[h]]) + inter[head_rows[h]]
            gate = prv_ref[rows, C_GG + h * DV:C_GG + (h + 1) * DV]
            yn = o * lax.rsqrt(jnp.mean(o * o, axis=-1, keepdims=True) + EPS) * glan_ref[h:h + 1, :]
            y_ref[rows, h * DV:(h + 1) * DV] = (yn * _swish(gate)).astype(BF16)

    def ret_values(c, inter, scores):
        rows = chunk_rows[c]
        v = prv_ref[rows, C_RV:C_RV + HV].astype(BF16)
        for h in range(HEADS):
            o = _dot(scores[head_rows[h]], v[:, head_cols[h]]) + inter[head_rows[h]]
            gate = prv_ref[rows, C_RG + h * DV:C_RG + (h + 1) * DV]
            d = o - jnp.mean(o, axis=-1, keepdims=True)
            yn = d * lax.rsqrt(jnp.mean(d * d, axis=-1, keepdims=True) + EPS) * retn_ref[h:h + 1, :]
            y_ref[rows, HV + h * DV:HV + (h + 1) * DV] = (yn * _swish(gate)).astype(BF16)

    def out_proj(done):
        o_ref[0, done, :] = xres_ref[0, done, :] + _dot(y_ref[done, :], wout_ref[...])

    finished = None
    ahead = (gla_scores(0), ret_scores(0))
    for c in range(n_chunks):
        gla_ahead, ret_ahead = ahead
        if c + 1 < n_chunks:
            ahead = (gla_scores(c + 1), ret_scores(c + 1))
        if finished is not None:
            out_proj(finished)
            finished = None
        else:
            emit_proj(1)
        gla_values(c, *gla_ahead)
        ret_values(c, *ret_ahead)
        if (c + 1) % (OUT_ROWS // CHUNK) == 0:
            finished = slice((c + 1) * CHUNK - OUT_ROWS, (c + 1) * CHUNK)
    emit_proj(len(pieces))
    out_proj(finished)


def _xattn_mlp_kernel(h_ref, wqk_ref, wvo_ref, gx_ref, gf_ref, w1_ref, w2_ref, gfin_ref, o_ref):
    bounds = np.cumsum((0,) + XATTN_BLOCKS)
    blocks = [slice(int(a), int(b)) for a, b in zip(bounds[:-1], bounds[1:])]
    n_mem = wqk_ref.shape[2] // XATTN_HEADS
    h = [h_ref[0, rows, :] for rows in blocks]
    scores = [_dot(_rms(hb, gx_ref[...]).astype(BF16), wqk_ref[0]) for hb in h]
    probs = []
    for sb in scores:
        heads = []
        for a in range(XATTN_HEADS):
            s = sb[:, a * n_mem:(a + 1) * n_mem]
            e = jnp.exp(s - jnp.max(s, axis=-1, keepdims=True))
            heads.append((e * (1.0 / jnp.sum(e, axis=-1, keepdims=True))).astype(BF16))
        probs.append(jnp.concatenate(heads, axis=1))
    h = [hb + _dot(pb, wvo_ref[0]) for hb, pb in zip(h, probs)]

    hn = [_rms(hb, gf_ref[...]).astype(BF16) for hb in h]
    for c in range(D_FF // FF_BLOCK):
        cols = slice(c * FF_BLOCK, (c + 1) * FF_BLOCK)
        u = [jnp.maximum(_dot(hb, w1_ref[:, cols]), 0.0) for hb in hn]
        h = [hb + _dot((ub * ub).astype(BF16), w2_ref[cols, :]) for hb, ub in zip(h, u)]
    for rows, hb in zip(blocks, h):
        o_ref[0, rows, :] = _rms(hb, gfin_ref[...])


def _const_spec(shape):
    n = len(shape)
    return pl.BlockSpec(shape, lambda *_: (0,) * n, pipeline_mode=pl.Buffered(1))


def _retention_tables():
    gamma = 1.0 - np.exp2(-np.linspace(RET_DECAY_EXP_LO, RET_DECAY_EXP_HI, HEADS, dtype=np.float32))
    log_g = np.log(gamma.astype(np.float32)).astype(np.float32)
    pos = np.arange(CHUNK, dtype=np.float32)
    dmat = np.exp(np.abs(pos[:, None] - pos[None, :])[None] * log_g[:, None, None])
    k_dec = np.exp((CHUNK - 1.0 - pos)[None, :] * log_g[:, None])
    q_dec = np.exp((pos + 1.0)[None, :] * log_g[:, None])
    s_dec = np.exp(CHUNK * log_g)
    dmat_s = dmat.reshape(HEADS * CHUNK, CHUNK)
    qdec_s = np.broadcast_to(q_dec.reshape(HEADS * CHUNK, 1), (HEADS * CHUNK, DV))
    kdect_s = np.repeat(k_dec, DK, axis=0)
    sdec_s = np.broadcast_to(np.repeat(s_dec, DK)[:, None], (HEADS * DK, DV))
    as32 = lambda a: jnp.asarray(np.ascontiguousarray(a), dtype=F32)
    return as32(dmat_s), as32(qdec_s), as32(kdect_s), as32(sdec_s)


def kernel(x, mem, norm_mix_g, w_in, gla_w_gate, gla_b_gate, gla_norm_g, ret_norm_g, w_out, norm_x_g,
           norm_mem_g, xa_w_q, xa_w_kv, xa_w_o, norm_ffn_g, ffn_w1, ffn_w2, final_norm_g):
    B, S, D = x.shape
    n_mem = mem.shape[1]
    assert D == D_MODEL and S % SEQ_TILE == 0 and S % XATTN_TILE == 0 and w_in.shape[0] == 1
    ts = SEQ_TILE
    n_tiles = S // ts

    inv_freq = ROPE_BASE ** (-jnp.arange(0, DK, 2, dtype=F32) / DK)
    inv_lane = jnp.tile(inv_freq, 2 * HEADS)
    ang_off = jnp.arange(ts, dtype=F32)[:, None] * inv_lane[None, :]
    ang_base = (jnp.arange(n_tiles, dtype=F32) * ts)[:, None, None] * inv_lane[None, None, :]
    cos_off, sin_off = jnp.cos(ang_off), jnp.sin(ang_off)
    cos_base, sin_base = jnp.cos(ang_base), jnp.sin(ang_base)

    w_in_r = pl.pallas_call(
        _prep_w_in_kernel,
        grid=(D // PREP_ROWS,),
        in_specs=[pl.BlockSpec((w_in.shape[2], PREP_ROWS), lambda i: (0, i))],
        out_specs=pl.BlockSpec((PREP_ROWS, D_PROJ), lambda i: (i, 0)),
        out_shape=jax.ShapeDtypeStruct((D, D_PROJ), BF16),
        compiler_params=pltpu.CompilerParams(dimension_semantics=("arbitrary",),
                                             vmem_limit_bytes=VMEM_LIMIT_BYTES),
        name="prep_w_in",
    )(jnp.swapaxes(w_in[0], 0, 1))
    w_gate_p = jnp.concatenate(
        [gla_w_gate[0], jnp.zeros((GATE_RANK_PAD - GATE_RANK, HK), F32)], axis=0).astype(BF16)
    dmat_s, qdec_s, kdect_s, sdec_s = _retention_tables()

    params = pltpu.CompilerParams(dimension_semantics=("arbitrary", "arbitrary"),
                                  vmem_limit_bytes=VMEM_LIMIT_BYTES)

    hm = XATTN_HEADS * n_mem
    wqk, wvo = pl.pallas_call(
        _mem_kv_kernel,
        grid=(B,),
        in_specs=[pl.BlockSpec((1, n_mem, D), lambda b: (b, 0, 0)),
                  _const_spec((1, D)), _const_spec((D, 2 * D)), _const_spec((D, D)),
                  _const_spec((D, D))],
        out_specs=[pl.BlockSpec((1, D, hm), lambda b: (b, 0, 0)),
                   pl.BlockSpec((1, hm, D), lambda b: (b, 0, 0))],
        out_shape=[jax.ShapeDtypeStruct((B, D, hm), BF16), jax.ShapeDtypeStruct((B, hm, D), BF16)],
        compiler_params=pltpu.CompilerParams(dimension_semantics=("arbitrary",),
                                             vmem_limit_bytes=VMEM_LIMIT_BYTES),
        name="mem_kv",
    )(mem, norm_mem_g[0][None, :], xa_w_kv[0], xa_w_q[0], xa_w_o[0])

    n_steps = B * n_tiles + 1

    def next_tile(s):
        p = jnp.minimum(s, n_steps - 2)
        return (p // n_tiles, p % n_tiles, 0)

    def prev_tile(s):
        q = jnp.maximum(s - 1, 0)
        return (q // n_tiles, q % n_tiles, 0)

    def prev_rope(s):
        return (jnp.maximum(s - 1, 0) % n_tiles, 0, 0)

    h1 = pl.pallas_call(
        functools.partial(_mixer_kernel, n_chunks=ts // CHUNK, n_tiles=n_tiles),
        grid=(n_steps,),
        in_specs=[pl.BlockSpec((1, ts, D), next_tile),
                  pl.BlockSpec((1, ts, D), prev_tile),
                  pl.BlockSpec((1, 1, HK), prev_rope), pl.BlockSpec((1, 1, HK), prev_rope),
                  _const_spec((ts, HK)), _const_spec((ts, HK)),
                  _const_spec((1, D)), _const_spec((D, D_PROJ)),
                  _const_spec((GATE_RANK_PAD, HK)), _const_spec((1, HK)),
                  _const_spec((HEADS, DV)), _const_spec((HEADS, DV)),
                  _const_spec((D, D)),
                  _const_spec((HEADS * CHUNK, CHUNK)), _const_spec((HEADS * CHUNK, DV)),
                  _const_spec((HEADS * DK, CHUNK)), _const_spec((HEADS * DK, DV))],
        out_specs=pl.BlockSpec((1, ts, D), prev_tile),
        out_shape=jax.ShapeDtypeStruct((B, S, D), F32),
        scratch_shapes=[pltpu.VMEM((ts, D_PROJ), F32), pltpu.VMEM((ts, D_PROJ), F32),
                        pltpu.VMEM((ts, D), BF16), pltpu.VMEM((ts, D), BF16),
                        pltpu.VMEM((3, ts // CHUNK, HK, CHUNK), BF16),
                        pltpu.VMEM((HEADS * DK, DV), F32), pltpu.VMEM((HEADS * DK, DV), F32)],
        compiler_params=pltpu.CompilerParams(dimension_semantics=("arbitrary",),
                                             vmem_limit_bytes=VMEM_LIMIT_BYTES),
        name="mixer",
    )(x, x, cos_base, sin_base, cos_off, sin_off, norm_mix_g[0][None, :], w_in_r, w_gate_p,
      gla_b_gate[0][None, :], gla_norm_g[0], ret_norm_g[0], w_out[0].astype(BF16),
      dmat_s, qdec_s, kdect_s, sdec_s)

    out = pl.pallas_call(
        _xattn_mlp_kernel,
        grid=(B, S // XATTN_TILE),
        in_specs=[pl.BlockSpec((1, XATTN_TILE, D), lambda b, t: (b, t, 0)),
                  pl.BlockSpec((1, D, hm), lambda b, t: (b, 0, 0)),
                  pl.BlockSpec((1, hm, D), lambda b, t: (b, 0, 0)),
                  _const_spec((1, D)),
                  _const_spec((1, D)), _const_spec((D, D_FF)), _const_spec((D_FF, D)),
                  _const_spec((1, D))],
        out_specs=pl.BlockSpec((1, XATTN_TILE, D), lambda b, t: (b, t, 0)),
        out_shape=jax.ShapeDtypeStruct((B, S, D), F32),
        compiler_params=params,
        name="xattn_mlp",
    )(h1, wqk, wvo, norm_x_g[0][None, :], norm_ffn_g[0][None, :], ffn_w1[0].astype(BF16),
      ffn_w2[0].astype(BF16), final_norm_g[None, :])
    return out
```

```python
import functools

import numpy as np
import jax
import jax.numpy as jnp
from jax import lax
from jax.experimental import pallas as pl
from jax.experimental.pallas import tpu as pltpu

D_MODEL = 1024
CHUNK = 64
HEADS = 4
DK = 64
DV = 128
GATE_RANK = 16
GATE_RANK_PAD = 128
GATE_TAU = 16.0
RET_DECAY_EXP_LO = 5.0
RET_DECAY_EXP_HI = 12.0
ROPE_BASE = 10000.0
XATTN_HEADS = 4
XATTN_DH = D_MODEL // XATTN_HEADS
D_FF = 4 * D_MODEL
FF_BLOCK = 1024
XATTN_TILE = 512
XATTN_BLOCKS = (256, 256)
PREP_ROWS = 256
EPS = 1e-6

HK = HEADS * DK
HV = HEADS * DV
C_GQ, C_GK, C_GV, C_GG = 0, HK, 2 * HK, 2 * HK + HV
C_RQ = C_GG + HV
C_RK = C_RQ + HK
C_RV = C_RK + HK
C_RG = C_RV + HV
C_LR = C_RG + HV
D_PROJ = C_LR + GATE_RANK_PAD

SEQ_TILE = 512
PROJ_PIECE = 256
STAGE_A_PIECES = 5
CUM_BLOCK = 256
OUT_ROWS = 512
VMEM_LIMIT_BYTES = 56 * 1024 * 1024

F32 = jnp.float32
BF16 = jnp.bfloat16


def _dot(a, b):
    return jnp.dot(a, b, preferred_element_type=F32)


def _dot_nt(a, b):
    return lax.dot_general(a, b, (((1,), (1,)), ((), ())), preferred_element_type=F32)


def _rms(x, g):
    return x * lax.rsqrt(jnp.mean(x * x, axis=-1, keepdims=True) + EPS) * g


def _swish(g):
    return g / (1.0 + jnp.exp(-g))


def _stack_heads(t):
    lane_head = lax.broadcasted_iota(jnp.int32, t.shape, 1) // DK
    zero = jnp.zeros_like(t)
    return jnp.concatenate([jnp.where(lane_head == h, t, zero) for h in range(HEADS)], axis=0)


def _prep_w_in_kernel(wt_ref, o_ref):
    lr0 = 2 * HK + HV
    o_ref[:, :lr0] = wt_ref[:lr0, :].T.astype(BF16)
    o_ref[:, lr0:C_LR] = wt_ref[lr0 + GATE_RANK:, :].T.astype(BF16)
    tail = wt_ref[lr0:lr0 + GATE_RANK_PAD, :]
    row = lax.broadcasted_iota(jnp.int32, tail.shape, 0)
    o_ref[:, C_LR:] = jnp.where(row < GATE_RANK, tail, 0.0).T.astype(BF16)


def _mem_kv_kernel(mem_ref, g_ref, wk_ref, wv_ref, wq_ref, wo_ref, wqk_ref, wvo_ref, mn_ref):
    n_batch, n_mem = mem_ref.shape[0], mem_ref.shape[1]

    @pl.when(pl.program_id(0) == 0)
    def _():
        for b in range(n_batch):
            mn_ref[b * n_mem:(b + 1) * n_mem, :] = _rms(mem_ref[b], g_ref[...])

    k = _dot(mn_ref[...], wk_ref[...])
    v = _dot(mn_ref[...], wv_ref[...])
    for b in range(n_batch):
        rows = slice(b * n_mem, (b + 1) * n_mem)
        wqk_ref[b] = (_dot_nt(wq_ref[...], k[rows]) * (XATTN_DH ** -0.5)).astype(BF16)
        wvo_ref[b] = _dot(v[rows], wo_ref[...]).astype(BF16)


def _mixer_kernel(xnext_ref, xres_ref, cosb_ref, sinb_ref, coso_ref, sino_ref, g_ref, win_ref, wg_ref,
                  bg_ref, glan_ref, retn_ref, wout_ref, dmat_ref, qdec_ref, kdect_ref, sdec_ref,
                  o_ref, proj_a_ref, proj_b_ref, xn_ref, y_ref, kt_ref, sg_ref, sr_ref, *,
                  n_chunks, n_tiles):
    s = pl.program_id(0)
    prev_tile = jnp.maximum(s - 1, 0)

    @pl.when(s == 0)
    def _():
        proj_b_ref[...] = jnp.zeros_like(proj_b_ref)

    @pl.when(prev_tile % n_tiles == 0)
    def _():
        sg_ref[...] = jnp.zeros_like(sg_ref)
        sr_ref[...] = jnp.zeros_like(sr_ref)

    chunk_pass = functools.partial(
        _chunk_pass, xnext_ref=xnext_ref, g_ref=g_ref, xn_ref=xn_ref,
        rope_refs=(cosb_ref, sinb_ref, coso_ref, sino_ref),
        win_ref=win_ref, wg_ref=wg_ref, bg_ref=bg_ref, glan_ref=glan_ref, retn_ref=retn_ref,
        dmat_ref=dmat_ref, qdec_ref=qdec_ref, kdect_ref=kdect_ref, sdec_ref=sdec_ref, y_ref=y_ref,
        kt_ref=kt_ref, sg_ref=sg_ref, sr_ref=sr_ref, xres_ref=xres_ref, wout_ref=wout_ref,
        o_ref=o_ref, n_chunks=n_chunks)

    @pl.when(s % 2 == 0)
    def _():
        chunk_pass(proj_a_ref, proj_b_ref)

    @pl.when(s % 2 == 1)
    def _():
        chunk_pass(proj_b_ref, proj_a_ref)


def _chunk_pass(cur_ref, prv_ref, *, xnext_ref, g_ref, xn_ref, rope_refs, win_ref, wg_ref, bg_ref, glan_ref,
                retn_ref, dmat_ref, qdec_ref, kdect_ref, sdec_ref, y_ref, kt_ref, sg_ref, sr_ref,
                xres_ref, wout_ref, o_ref, n_chunks):
    ts = n_chunks * CHUNK
    chunk_rows = [slice(c * CHUNK, (c + 1) * CHUNK) for c in range(n_chunks)]
    head_rows = [slice(h * CHUNK, (h + 1) * CHUNK) for h in range(HEADS)]
    head_cols = [slice(h * DV, (h + 1) * DV) for h in range(HEADS)]
    pieces = [(i * PROJ_PIECE, min((i + 1) * PROJ_PIECE, D_PROJ))
              for i in range(-(-D_PROJ // PROJ_PIECE))]

    def emit_proj(n):
        for _ in range(min(n, len(pieces))):
            lo, hi = pieces.pop(0)
            cur_ref[:, lo:hi] = _dot(xn_ref[...], win_ref[:, lo:hi])

    bi = lax.broadcasted_iota(jnp.int32, (CUM_BLOCK, CUM_BLOCK), 0)
    bj = lax.broadcasted_iota(jnp.int32, (CUM_BLOCK, CUM_BLOCK), 1)
    tril_blocks = ((bi // CHUNK == bj // CHUNK) & (bj <= bi)).astype(BF16)
    si = lax.broadcasted_iota(jnp.int32, (HEADS * CHUNK, CHUNK), 0) % CHUNK
    sj = lax.broadcasted_iota(jnp.int32, (HEADS * CHUNK, CHUNK), 1)
    lower = sj <= si
    first_half = (lax.broadcasted_iota(jnp.int32, (1, HK), 1) % DK) < (DK // 2)
    lane_pad = jnp.zeros((HK, CHUNK), BF16)

    def contributions(kdt, v):
        return jnp.concatenate([_dot(kdt[head_rows[h]], v[:, head_cols[h]]) for h in range(HEADS)],
                               axis=0)

    def scan_states(s_ref, decays, contribs):
        state, starts = s_ref[...], []
        for dec, con in zip(decays, contribs):
            starts.append(state.astype(BF16))
            state = state * dec + con
        s_ref[...] = state
        return starts

    xn_ref[...] = _rms(xnext_ref[0], g_ref[...]).astype(BF16)
    z = _dot(prv_ref[:, C_LR:C_LR + GATE_RANK_PAD].astype(BF16), wg_ref[...]) + bg_ref[...]
    emit_proj(STAGE_A_PIECES)
    log_a = (jnp.minimum(z, 0.0) - jnp.log1p(jnp.exp(-jnp.abs(z)))) * (1.0 / GATE_TAU)
    la_hi = log_a.astype(BF16)
    la_lo = (log_a - la_hi.astype(F32)).astype(BF16)
    la_split = jnp.concatenate([la_hi, la_lo], axis=1)
    bcum = []
    for b in range(ts // CUM_BLOCK):
        r = _dot(tril_blocks, la_split[b * CUM_BLOCK:(b + 1) * CUM_BLOCK])
        bcum.append(r[:, :HK] + r[:, HK:])
    bcum = jnp.concatenate(bcum, axis=0)
    emit_proj(1)
    q = prv_ref[:, C_GQ:C_GQ + HK] * (DK ** -0.5)
    qeb = (q * jnp.exp(bcum)).astype(BF16)
    qebi = (q * jnp.exp(-bcum)).astype(BF16)

    g_decay, g_contrib = [], []
    for c, rows in enumerate(chunk_rows):
        kt = prv_ref[rows, C_GK:C_GK + HK].T
        bct = bcum[rows].T
        blt = bct[:, CHUNK - 1:CHUNK]
        kt_ref[0, c] = (kt * jnp.exp(bct)).astype(BF16)
        kt_ref[1, c] = (kt * jnp.exp(-bct)).astype(BF16)
        kdt = (kt * jnp.exp(blt - bct)).astype(BF16)
        g_decay.append(jnp.exp(blt))
        g_contrib.append(contributions(kdt, prv_ref[rows, C_GV:C_GV + HV].astype(BF16)))
        if c % 2 == 1:
            emit_proj(1)

    cosb, sinb, coso, sino = [r[...] for r in rope_refs]
    cosb, sinb = cosb[0], sinb[0]
    cs = cosb * coso - sinb * sino
    sn = sinb * coso + cosb * sino
    sn = jnp.where(first_half, -sn, sn)

    def rope(t):
        partner = jnp.where(first_half, pltpu.roll(t, HK - DK // 2, 1), pltpu.roll(t, DK // 2, 1))
        return t * cs + partner * sn

    qrb = rope(prv_ref[:, C_RQ:C_RQ + HK]).astype(BF16)
    kr = rope(prv_ref[:, C_RK:C_RK + HK]) * (DK ** -0.5)
    r_contrib = []
    for c, rows in enumerate(chunk_rows):
        kt = kr[rows].T
        kt_ref[2, c] = kt.astype(BF16)
        kdt = (kt * kdect_ref[...]).astype(BF16)
        r_contrib.append(contributions(kdt, prv_ref[rows, C_RV:C_RV + HV].astype(BF16)))
        if c % 2 == 1:
            emit_proj(1)

    g_start = scan_states(sg_ref, g_decay, g_contrib)
    r_start = scan_states(sr_ref, [sdec_ref[...]] * n_chunks, r_contrib)

    def gla_scores(c):
        rows = chunk_rows[c]
        r = _dot(_stack_heads(qeb[rows]), jnp.concatenate([g_start[c], kt_ref[1, c], lane_pad], axis=1))
        a_up = _dot(_stack_heads(qebi[rows]), kt_ref[0, c])
        return r[:, :DV], jnp.where(lower, r[:, DV:DV + CHUNK], a_up).astype(BF16)

    def ret_scores(c):
        r = _dot(_stack_heads(qrb[chunk_rows[c]]),
                 jnp.concatenate([r_start[c], kt_ref[2, c], lane_pad], axis=1))
        return r[:, :DV] * qdec_ref[...], (r[:, DV:DV + CHUNK] * dmat_ref[...]).astype(BF16)

    def gla_values(c, inter, attn):
        rows = chunk_rows[c]
        v = prv_ref[rows, C_GV:C_GV + HV].astype(BF16)
        for h in range(HEADS):
            o = _dot(attn[head_rows[h]], v[:, head_cols[h]]) + inter[head_rows[h]]
            gate = prv_ref[rows, C_GG + h * DV:C_GG + (h + 1) * DV]
            yn = o * lax.rsqrt(jnp.mean(o * o, axis=-1, keepdims=True) + EPS) * glan_ref[h:h + 1, :]
            y_ref[rows, h * DV:(h + 1) * DV] = (yn * _swish(gate)).astype(BF16)

    def ret_values(c, inter, scores):
        rows = chunk_rows[c]
        v = prv_ref[rows, C_RV:C_RV + HV].astype(BF16)
        for h in range(HEADS):
            o = _dot(scores[head_rows[h]], v[:, head_cols[h]]) + inter[head_rows[h]]
            gate = prv_ref[rows, C_RG + h * DV:C_RG + (h + 1) * DV]
            d = o - jnp.mean(o, axis=-1, keepdims=True)
            yn = d * lax.rsqrt(jnp.mean(d * d, axis=-1, keepdims=True) + EPS) * retn_ref[h:h + 1, :]
            y_ref[rows, HV + h * DV:HV + (h + 1) * DV] = (yn * _swish(gate)).astype(BF16)

    def out_proj(done):
        o_ref[0, done, :] = xres_ref[0, done, :] + _dot(y_ref[done, :], wout_ref[...])

    finished = None
    ahead = (gla_scores(0), ret_scores(0))
    for c in range(n_chunks):
        gla_ahead, ret_ahead = ahead
        if c + 1 < n_chunks:
            ahead = (gla_scores(c + 1), ret_scores(c + 1))
        if finished is not None:
            out_proj(finished)
            finished = None
        else:
            emit_proj(1)
        gla_values(c, *gla_ahead)
        ret_values(c, *ret_ahead)
        if (c + 1) % (OUT_ROWS // CHUNK) == 0:
            finished = slice((c + 1) * CHUNK - OUT_ROWS, (c + 1) * CHUNK)
    emit_proj(len(pieces))
    out_proj(finished)


def _xattn_mlp_kernel(h_ref, wqk_ref, wvo_ref, gx_ref, gf_ref, w1_ref, w2_ref, gfin_ref, o_ref):
    bounds = np.cumsum((0,) + XATTN_BLOCKS)
    blocks = [slice(int(a), int(b)) for a, b in zip(bounds[:-1], bounds[1:])]
    n_mem = wqk_ref.shape[2] // XATTN_HEADS
    h = [h_ref[0, rows, :] for rows in blocks]
    scores = [_dot(_rms(hb, gx_ref[...]).astype(BF16), wqk_ref[0]) for hb in h]
    probs = []
    for sb in scores:
        heads = []
        for a in range(XATTN_HEADS):
            s = sb[:, a * n_mem:(a + 1) * n_mem]
            e = jnp.exp(s - jnp.max(s, axis=-1, keepdims=True))
            heads.append((e * (1.0 / jnp.sum(e, axis=-1, keepdims=True))).astype(BF16))
        probs.append(jnp.concatenate(heads, axis=1))
    h = [hb + _dot(pb, wvo_ref[0]) for hb, pb in zip(h, probs)]

    hn = [_rms(hb, gf_ref[...]).astype(BF16) for hb in h]
    for c in range(D_FF // FF_BLOCK):
        cols = slice(c * FF_BLOCK, (c + 1) * FF_BLOCK)
        u = [jnp.maximum(_dot(hb, w1_ref[:, cols]), 0.0) for hb in hn]
        h = [hb + _dot((ub * ub).astype(BF16), w2_ref[cols, :]) for hb, ub in zip(h, u)]
    for rows, hb in zip(blocks, h):
        o_ref[0, rows, :] = _rms(hb, gfin_ref[...])


def _const_spec(shape):
    n = len(shape)
    return pl.BlockSpec(shape, lambda *_: (0,) * n, pipeline_mode=pl.Buffered(1))


def _retention_tables():
    gamma = 1.0 - np.exp2(-np.linspace(RET_DECAY_EXP_LO, RET_DECAY_EXP_HI, HEADS, dtype=np.float32))
    log_g = np.log(gamma.astype(np.float32)).astype(np.float32)
    pos = np.arange(CHUNK, dtype=np.float32)
    dmat = np.exp(np.abs(pos[:, None] - pos[None, :])[None] * log_g[:, None, None])
    k_dec = np.exp((CHUNK - 1.0 - pos)[None, :] * log_g[:, None])
    q_dec = np.exp((pos + 1.0)[None, :] * log_g[:, None])
    s_dec = np.exp(CHUNK * log_g)
    dmat_s = dmat.reshape(HEADS * CHUNK, CHUNK)
    qdec_s = np.broadcast_to(q_dec.reshape(HEADS * CHUNK, 1), (HEADS * CHUNK, DV))
    kdect_s = np.repeat(k_dec, DK, axis=0)
    sdec_s = np.broadcast_to(np.repeat(s_dec, DK)[:, None], (HEADS * DK, DV))
    as32 = lambda a: jnp.asarray(np.ascontiguousarray(a), dtype=F32)
    return as32(dmat_s), as32(qdec_s), as32(kdect_s), as32(sdec_s)


def kernel(x, mem, norm_mix_g, w_in, gla_w_gate, gla_b_gate, gla_norm_g, ret_norm_g, w_out, norm_x_g,
           norm_mem_g, xa_w_q, xa_w_kv, xa_w_o, norm_ffn_g, ffn_w1, ffn_w2, final_norm_g):
    B, S, D = x.shape
    n_mem = mem.shape[1]
    assert D == D_MODEL and S % SEQ_TILE == 0 and S % XATTN_TILE == 0 and w_in.shape[0] == 1
    ts = SEQ_TILE
    n_tiles = S // ts

    inv_freq = ROPE_BASE ** (-jnp.arange(0, DK, 2, dtype=F32) / DK)
    inv_lane = jnp.tile(inv_freq, 2 * HEADS)
    ang_off = jnp.arange(ts, dtype=F32)[:, None] * inv_lane[None, :]
    ang_base = (jnp.arange(n_tiles, dtype=F32) * ts)[:, None, None] * inv_lane[None, None, :]
    cos_off, sin_off = jnp.cos(ang_off), jnp.sin(ang_off)
    cos_base, sin_base = jnp.cos(ang_base), jnp.sin(ang_base)

    w_in_r = pl.pallas_call(
        _prep_w_in_kernel,
        grid=(D // PREP_ROWS,),
        in_specs=[pl.BlockSpec((w_in.shape[2], PREP_ROWS), lambda i: (0, i))],
        out_specs=pl.BlockSpec((PREP_ROWS, D_PROJ), lambda i: (i, 0)),
        out_shape=jax.ShapeDtypeStruct((D, D_PROJ), BF16),
        compiler_params=pltpu.CompilerParams(dimension_semantics=("arbitrary",),
                                             vmem_limit_bytes=VMEM_LIMIT_BYTES),
        name="prep_w_in",
    )(jnp.swapaxes(w_in[0], 0, 1))
    w_gate_p = jnp.concatenate(
        [gla_w_gate[0], jnp.zeros((GATE_RANK_PAD - GATE_RANK, HK), F32)], axis=0).astype(BF16)
    dmat_s, qdec_s, kdect_s, sdec_s = _retention_tables()

    params = pltpu.CompilerParams(dimension_semantics=("arbitrary", "arbitrary"),
                                  vmem_limit_bytes=VMEM_LIMIT_BYTES)

    hm = XATTN_HEADS * n_mem
    wqk, wvo = pl.pallas_call(
        _mem_kv_kernel,
        grid=(XATTN_HEADS,),
        in_specs=[_const_spec((B, n_mem, D)), _const_spec((1, D)),
                  pl.BlockSpec((D, XATTN_DH), lambda h: (0, h)),
                  pl.BlockSpec((D, XATTN_DH), lambda h: (0, XATTN_HEADS + h)),
                  pl.BlockSpec((D, XATTN_DH), lambda h: (0, h)),
                  pl.BlockSpec((XATTN_DH, D), lambda h: (h, 0))],
        out_specs=[pl.BlockSpec((B, D, n_mem), lambda h: (0, 0, h)),
                   pl.BlockSpec((B, n_mem, D), lambda h: (0, h, 0))],
        out_shape=[jax.ShapeDtypeStruct((B, D, hm), BF16), jax.ShapeDtypeStruct((B, hm, D), BF16)],
        scratch_shapes=[pltpu.VMEM((B * n_mem, D), F32)],
        compiler_params=pltpu.CompilerParams(dimension_semantics=("arbitrary",),
                                             vmem_limit_bytes=VMEM_LIMIT_BYTES),
        name="mem_kv",
    )(mem, norm_mem_g[0][None, :], xa_w_kv[0], xa_w_kv[0], xa_w_q[0], xa_w_o[0])

    n_steps = B * n_tiles + 1

    def next_tile(s):
        p = jnp.minimum(s, n_steps - 2)
        return (p // n_tiles, p % n_tiles, 0)

    def prev_tile(s):
        q = jnp.maximum(s - 1, 0)
        return (q // n_tiles, q % n_tiles, 0)

    def prev_rope(s):
        return (jnp.maximum(s - 1, 0) % n_tiles, 0, 0)

    h1 = pl.pallas_call(
        functools.partial(_mixer_kernel, n_chunks=ts // CHUNK, n_tiles=n_tiles),
        grid=(n_steps,),
        in_specs=[pl.BlockSpec((1, ts, D), next_tile),
                  pl.BlockSpec((1, ts, D), prev_tile),
                  pl.BlockSpec((1, 1, HK), prev_rope), pl.BlockSpec((1, 1, HK), prev_rope),
                  _const_spec((ts, HK)), _const_spec((ts, HK)),
                  _const_spec((1, D)), _const_spec((D, D_PROJ)),
                  _const_spec((GATE_RANK_PAD, HK)), _const_spec((1, HK)),
                  _const_spec((HEADS, DV)), _const_spec((HEADS, DV)),
                  _const_spec((D, D)),
                  _const_spec((HEADS * CHUNK, CHUNK)), _const_spec((HEADS * CHUNK, DV)),
                  _const_spec((HEADS * DK, CHUNK)), _const_spec((HEADS * DK, DV))],
        out_specs=pl.BlockSpec((1, ts, D), prev_tile),
        out_shape=jax.ShapeDtypeStruct((B, S, D), F32),
        scratch_shapes=[pltpu.VMEM((ts, D_PROJ), F32), pltpu.VMEM((ts, D_PROJ), F32),
                        pltpu.VMEM((ts, D), BF16), pltpu.VMEM((ts, D), BF16),
                        pltpu.VMEM((3, ts // CHUNK, HK, CHUNK), BF16),
                        pltpu.VMEM((HEADS * DK, DV), F32), pltpu.VMEM((HEADS * DK, DV), F32)],
        compiler_params=pltpu.CompilerParams(dimension_semantics=("arbitrary",),
                                             vmem_limit_bytes=VMEM_LIMIT_BYTES),
        name="mixer",
    )(x, x, cos_base, sin_base, cos_off, sin_off, norm_mix_g[0][None, :], w_in_r, w_gate_p,
      gla_b_gate[0][None, :], gla_norm_g[0], ret_norm_g[0], w_out[0].astype(BF16),
      dmat_s, qdec_s, kdect_s, sdec_s)

    out = pl.pallas_call(
        _xattn_mlp_kernel,
        grid=(B, S // XATTN_TILE),
        in_specs=[pl.BlockSpec((1, XATTN_TILE, D), lambda b, t: (b, t, 0)),
                  pl.BlockSpec((1, D, hm), lambda b, t: (b, 0, 0)),
                  pl.BlockSpec((1, hm, D), lambda b, t: (b, 0, 0)),
                  _const_spec((1, D)),
                  _const_spec((1, D)), _const_spec((D, D_FF)), _const_spec((D_FF, D)),
                  _const_spec((1, D))],
        out_specs=pl.BlockSpec((1, XATTN_TILE, D), lambda b, t: (b, t, 0)),
        out_shape=jax.ShapeDtypeStruct((B, S, D), F32),
        compiler_params=params,
        name="xattn_mlp",
    )(h1, wqk, wvo, norm_x_g[0][None, :], norm_ffn_g[0][None, :], ffn_w1[0].astype(BF16),
      ffn_w2[0].astype(BF16), final_norm_g[None, :])
    return out
```

```python
import functools

import numpy as np
import jax
import jax.numpy as jnp
from jax import lax
from jax.experimental import pallas as pl
from jax.experimental.pallas import tpu as pltpu

D_MODEL = 1024
CHUNK = 64
HEADS = 4
DK = 64
DV = 128
GATE_RANK = 16
GATE_RANK_PAD = 128
GATE_TAU = 16.0
RET_DECAY_EXP_LO = 5.0
RET_DECAY_EXP_HI = 12.0
ROPE_BASE = 10000.0
XATTN_HEADS = 4
XATTN_DH = D_MODEL // XATTN_HEADS
D_FF = 4 * D_MODEL
FF_BLOCK = 1024
XATTN_TILE = 512
XATTN_BLOCKS = (256, 256)
PREP_ROWS = 256
EPS = 1e-6

HK = HEADS * DK
HV = HEADS * DV
C_GQ, C_GK, C_GV, C_GG = 0, HK, 2 * HK, 2 * HK + HV
C_RQ = C_GG + HV
C_RK = C_RQ + HK
C_RV = C_RK + HK
C_RG = C_RV + HV
C_LR = C_RG + HV
D_PROJ = C_LR + GATE_RANK_PAD

SEQ_TILE = 512
PROJ_PIECE = 256
STAGE_A_PIECES = 5
CUM_BLOCK = 256
OUT_ROWS = 512
VMEM_LIMIT_BYTES = 56 * 1024 * 1024

F32 = jnp.float32
BF16 = jnp.bfloat16


def _dot(a, b):
    return jnp.dot(a, b, preferred_element_type=F32)


def _dot_nt(a, b):
    return lax.dot_general(a, b, (((1,), (1,)), ((), ())), preferred_element_type=F32)


def _rms(x, g):
    return x * lax.rsqrt(jnp.mean(x * x, axis=-1, keepdims=True) + EPS) * g


def _swish(g):
    return g / (1.0 + jnp.exp(-g))


def _stack_heads(t):
    lane_head = lax.broadcasted_iota(jnp.int32, t.shape, 1) // DK
    zero = jnp.zeros_like(t)
    return jnp.concatenate([jnp.where(lane_head == h, t, zero) for h in range(HEADS)], axis=0)


def _prep_w_in_kernel(wt_ref, o_ref):
    lr0 = 2 * HK + HV
    o_ref[:, :lr0] = wt_ref[:lr0, :].T.astype(BF16)
    o_ref[:, lr0:C_LR] = wt_ref[lr0 + GATE_RANK:, :].T.astype(BF16)
    tail = wt_ref[lr0:lr0 + GATE_RANK_PAD, :]
    row = lax.broadcasted_iota(jnp.int32, tail.shape, 0)
    o_ref[:, C_LR:] = jnp.where(row < GATE_RANK, tail, 0.0).T.astype(BF16)


def _mem_kv_kernel(mem_ref, g_ref, wkv_ref, wq_ref, wo_ref, wqk_ref, wvo_ref):
    n_mem = mem_ref.shape[1]
    mn = _rms(mem_ref[0], g_ref[...])
    kv = _dot(mn, wkv_ref[...])
    for h in range(XATTN_HEADS):
        cols = slice(h * XATTN_DH, (h + 1) * XATTN_DH)
        mcols = slice(h * n_mem, (h + 1) * n_mem)
        k_h = kv[:, cols]
        v_h = kv[:, D_MODEL + h * XATTN_DH:D_MODEL + (h + 1) * XATTN_DH]
        wqk_ref[0, :, mcols] = (_dot_nt(wq_ref[:, cols], k_h) * (XATTN_DH ** -0.5)).astype(BF16)
        wvo_ref[0, mcols, :] = _dot(v_h, wo_ref[cols, :]).astype(BF16)


def _mixer_kernel(xnext_ref, xres_ref, cosb_ref, sinb_ref, coso_ref, sino_ref, g_ref, win_ref, wg_ref,
                  bg_ref, glan_ref, retn_ref, wout_ref, dmat_ref, qdec_ref, kdect_ref, sdec_ref,
                  o_ref, proj_a_ref, proj_b_ref, xn_ref, y_ref, kt_ref, sg_ref, sr_ref, *,
                  n_chunks, n_tiles):
    s = pl.program_id(0)
    prev_tile = jnp.maximum(s - 1, 0)

    @pl.when(s == 0)
    def _():
        proj_b_ref[...] = jnp.zeros_like(proj_b_ref)

    @pl.when(prev_tile % n_tiles == 0)
    def _():
        sg_ref[...] = jnp.zeros_like(sg_ref)
        sr_ref[...] = jnp.zeros_like(sr_ref)

    chunk_pass = functools.partial(
        _chunk_pass, xnext_ref=xnext_ref, g_ref=g_ref, xn_ref=xn_ref,
        rope_refs=(cosb_ref, sinb_ref, coso_ref, sino_ref),
        win_ref=win_ref, wg_ref=wg_ref, bg_ref=bg_ref, glan_ref=glan_ref, retn_ref=retn_ref,
        dmat_ref=dmat_ref, qdec_ref=qdec_ref, kdect_ref=kdect_ref, sdec_ref=sdec_ref, y_ref=y_ref,
        kt_ref=kt_ref, sg_ref=sg_ref, sr_ref=sr_ref, xres_ref=xres_ref, wout_ref=wout_ref,
        o_ref=o_ref, n_chunks=n_chunks)

    @pl.when(s % 2 == 0)
    def _():
        chunk_pass(proj_a_ref, proj_b_ref)

    @pl.when(s % 2 == 1)
    def _():
        chunk_pass(proj_b_ref, proj_a_ref)


def _chunk_pass(cur_ref, prv_ref, *, xnext_ref, g_ref, xn_ref, rope_refs, win_ref, wg_ref, bg_ref, glan_ref,
                retn_ref, dmat_ref, qdec_ref, kdect_ref, sdec_ref, y_ref, kt_ref, sg_ref, sr_ref,
                xres_ref, wout_ref, o_ref, n_chunks):
    ts = n_chunks * CHUNK
    chunk_rows = [slice(c * CHUNK, (c + 1) * CHUNK) for c in range(n_chunks)]
    head_rows = [slice(h * CHUNK, (h + 1) * CHUNK) for h in range(HEADS)]
    head_cols = [slice(h * DV, (h + 1) * DV) for h in range(HEADS)]
    pieces = [(i * PROJ_PIECE, min((i + 1) * PROJ_PIECE, D_PROJ))
              for i in range(-(-D_PROJ // PROJ_PIECE))]

    def emit_proj(n):
        for _ in range(min(n, len(pieces))):
            lo, hi = pieces.pop(0)
            cur_ref[:, lo:hi] = _dot(xn_ref[...], win_ref[:, lo:hi])

    bi = lax.broadcasted_iota(jnp.int32, (CUM_BLOCK, CUM_BLOCK), 0)
    bj = lax.broadcasted_iota(jnp.int32, (CUM_BLOCK, CUM_BLOCK), 1)
    tril_blocks = ((bi // CHUNK == bj // CHUNK) & (bj <= bi)).astype(BF16)
    si = lax.broadcasted_iota(jnp.int32, (HEADS * CHUNK, CHUNK), 0) % CHUNK
    sj = lax.broadcasted_iota(jnp.int32, (HEADS * CHUNK, CHUNK), 1)
    lower = sj <= si
    first_half = (lax.broadcasted_iota(jnp.int32, (1, HK), 1) % DK) < (DK // 2)
    lane_pad = jnp.zeros((HK, CHUNK), BF16)

    def contributions(kdt, v):
        return jnp.concatenate([_dot(kdt[head_rows[h]], v[:, head_cols[h]]) for h in range(HEADS)],
                               axis=0)

    def scan_states(s_ref, decays, contribs):
        state, starts = s_ref[...], []
        for dec, con in zip(decays, contribs):
            starts.append(state.astype(BF16))
            state = state * dec + con
        s_ref[...] = state
        return starts

    xn_ref[...] = _rms(xnext_ref[0], g_ref[...]).astype(BF16)
    z = _dot(prv_ref[:, C_LR:C_LR + GATE_RANK_PAD].astype(BF16), wg_ref[...]) + bg_ref[...]
    emit_proj(STAGE_A_PIECES)
    log_a = (jnp.minimum(z, 0.0) - jnp.log1p(jnp.exp(-jnp.abs(z)))) * (1.0 / GATE_TAU)
    la_hi = log_a.astype(BF16)
    la_lo = (log_a - la_hi.astype(F32)).astype(BF16)
    la_split = jnp.concatenate([la_hi, la_lo], axis=1)
    bcum = []
    for b in range(ts // CUM_BLOCK):
        r = _dot(tril_blocks, la_split[b * CUM_BLOCK:(b + 1) * CUM_BLOCK])
        bcum.append(r[:, :HK] + r[:, HK:])
    bcum = jnp.concatenate(bcum, axis=0)
    emit_proj(1)
    q = prv_ref[:, C_GQ:C_GQ + HK] * (DK ** -0.5)
    qeb = (q * jnp.exp(bcum)).astype(BF16)
    qebi = (q * jnp.exp(-bcum)).astype(BF16)

    g_decay, g_contrib = [], []
    for c, rows in enumerate(chunk_rows):
        kt = prv_ref[rows, C_GK:C_GK + HK].T
        bct = bcum[rows].T
        blt = bct[:, CHUNK - 1:CHUNK]
        kt_ref[0, c] = (kt * jnp.exp(bct)).astype(BF16)
        kt_ref[1, c] = (kt * jnp.exp(-bct)).astype(BF16)
        kdt = (kt * jnp.exp(blt - bct)).astype(BF16)
        g_decay.append(jnp.exp(blt))
        g_contrib.append(contributions(kdt, prv_ref[rows, C_GV:C_GV + HV].astype(BF16)))
        if c % 2 == 1:
            emit_proj(1)

    cosb, sinb, coso, sino = [r[...] for r in rope_refs]
    cosb, sinb = cosb[0], sinb[0]
    cs = cosb * coso - sinb * sino
    sn = sinb * coso + cosb * sino
    sn = jnp.where(first_half, -sn, sn)

    def rope(t):
        partner = jnp.where(first_half, pltpu.roll(t, HK - DK // 2, 1), pltpu.roll(t, DK // 2, 1))
        return t * cs + partner * sn

    qrb = rope(prv_ref[:, C_RQ:C_RQ + HK]).astype(BF16)
    kr = rope(prv_ref[:, C_RK:C_RK + HK]) * (DK ** -0.5)
    r_contrib = []
    for c, rows in enumerate(chunk_rows):
        kt = kr[rows].T
        kt_ref[2, c] = kt.astype(BF16)
        kdt = (kt * kdect_ref[...]).astype(BF16)
        r_contrib.append(contributions(kdt, prv_ref[rows, C_RV:C_RV + HV].astype(BF16)))
        if c % 4 == 3:
            emit_proj(1)

    g_start = scan_states(sg_ref, g_decay, g_contrib)
    r_start = scan_states(sr_ref, [sdec_ref[...]] * n_chunks, r_contrib)

    def gla_scores(c):
        rows = chunk_rows[c]
        r = _dot(_stack_heads(qeb[rows]), jnp.concatenate([g_start[c], kt_ref[1, c], lane_pad], axis=1))
        a_up = _dot(_stack_heads(qebi[rows]), kt_ref[0, c])
        return r[:, :DV], jnp.where(lower, r[:, DV:DV + CHUNK], a_up).astype(BF16)

    def ret_scores(c):
        r = _dot(_stack_heads(qrb[chunk_rows[c]]),
                 jnp.concatenate([r_start[c], kt_ref[2, c], lane_pad], axis=1))
        return r[:, :DV] * qdec_ref[...], (r[:, DV:DV + CHUNK] * dmat_ref[...]).astype(BF16)

    def gla_values(c, inter, attn):
        rows = chunk_rows[c]
        v = prv_ref[rows, C_GV:C_GV + HV].astype(BF16)
        for h in range(HEADS):
            o = _dot(attn[head_rows[h]], v[:, head_cols[h]]) + inter[head_rows[h]]
            gate = prv_ref[rows, C_GG + h * DV:C_GG + (h + 1) * DV]
            yn = o * lax.rsqrt(jnp.mean(o * o, axis=-1, keepdims=True) + EPS) * glan_ref[h:h + 1, :]
            y_ref[rows, h * DV:(h + 1) * DV] = (yn * _swish(gate)).astype(BF16)

    def ret_values(c, inter, scores):
        rows = chunk_rows[c]
        v = prv_ref[rows, C_RV:C_RV + HV].astype(BF16)
        for h in range(HEADS):
            o = _dot(scores[head_rows[h]], v[:, head_cols[h]]) + inter[head_rows[h]]
            gate = prv_ref[rows, C_RG + h * DV:C_RG + (h + 1) * DV]
            d = o - jnp.mean(o, axis=-1, keepdims=True)
            yn = d * lax.rsqrt(jnp.mean(d * d, axis=-1, keepdims=True) + EPS) * retn_ref[h:h + 1, :]
            y_ref[rows, HV + h * DV:HV + (h + 1) * DV] = (yn * _swish(gate)).astype(BF16)

    def out_proj(done):
        o_ref[0, done, :] = xres_ref[0, done, :] + _dot(y_ref[done, :], wout_ref[...])

    finished = None
    ahead = (gla_scores(0), ret_scores(0))
    for c in range(n_chunks):
        gla_ahead, ret_ahead = ahead
        if c + 1 < n_chunks:
            ahead = (gla_scores(c + 1), ret_scores(c + 1))
        if finished is not None:
            out_proj(finished)
            finished = None
        else:
            emit_proj(1)
        gla_values(c, *gla_ahead)
        ret_values(c, *ret_ahead)
        if (c + 1) % (OUT_ROWS // CHUNK) == 0:
            finished = slice((c + 1) * CHUNK - OUT_ROWS, (c + 1) * CHUNK)
    emit_proj(len(pieces))
    out_proj(finished)


def _xattn_mlp_kernel(h_ref, wqk_ref, wvo_ref, gx_ref, gf_ref, w1_ref, w2_ref, gfin_ref, o_ref):
    bounds = np.cumsum((0,) + XATTN_BLOCKS)
    blocks = [slice(int(a), int(b)) for a, b in zip(bounds[:-1], bounds[1:])]
    n_mem = wqk_ref.shape[2] // XATTN_HEADS
    h = [h_ref[0, rows, :] for rows in blocks]
    scores = [_dot(_rms(hb, gx_ref[...]).astype(BF16), wqk_ref[0]) for hb in h]
    probs = []
    for sb in scores:
        heads = []
        for a in range(XATTN_HEADS):
            s = sb[:, a * n_mem:(a + 1) * n_mem]
            e = jnp.exp(s - jnp.max(s, axis=-1, keepdims=True))
            heads.append((e * (1.0 / jnp.sum(e, axis=-1, keepdims=True))).astype(BF16))
        probs.append(jnp.concatenate(heads, axis=1))
    h = [hb + _dot(pb, wvo_ref[0]) for hb, pb in zip(h, probs)]

    hn = [_rms(hb, gf_ref[...]).astype(BF16) for hb in h]
    for c in range(D_FF // FF_BLOCK):
        cols = slice(c * FF_BLOCK, (c + 1) * FF_BLOCK)
        u = [jnp.maximum(_dot(hb, w1_ref[:, cols]), 0.0) for hb in hn]
        h = [hb + _dot((ub * ub).astype(BF16), w2_ref[cols, :]) for hb, ub in zip(h, u)]
    for rows, hb in zip(blocks, h):
        o_ref[0, rows, :] = _rms(hb, gfin_ref[...])


def _const_spec(shape):
    n = len(shape)
    return pl.BlockSpec(shape, lambda *_: (0,) * n, pipeline_mode=pl.Buffered(1))


def _retention_tables():
    gamma = 1.0 - np.exp2(-np.linspace(RET_DECAY_EXP_LO, RET_DECAY_EXP_HI, HEADS, dtype=np.float32))
    log_g = np.log(gamma.astype(np.float32)).astype(np.float32)
    pos = np.arange(CHUNK, dtype=np.float32)
    dmat = np.exp(np.abs(pos[:, None] - pos[None, :])[None] * log_g[:, None, None])
    k_dec = np.exp((CHUNK - 1.0 - pos)[None, :] * log_g[:, None])
    q_dec = np.exp((pos + 1.0)[None, :] * log_g[:, None])
    s_dec = np.exp(CHUNK * log_g)
    dmat_s = dmat.reshape(HEADS * CHUNK, CHUNK)
    qdec_s = np.broadcast_to(q_dec.reshape(HEADS * CHUNK, 1), (HEADS * CHUNK, DV))
    kdect_s = np.repeat(k_dec, DK, axis=0)
    sdec_s = np.broadcast_to(np.repeat(s_dec, DK)[:, None], (HEADS * DK, DV))
    as32 = lambda a: jnp.asarray(np.ascontiguousarray(a), dtype=F32)
    return as32(dmat_s), as32(qdec_s), as32(kdect_s), as32(sdec_s)


def kernel(x, mem, norm_mix_g, w_in, gla_w_gate, gla_b_gate, gla_norm_g, ret_norm_g, w_out, norm_x_g,
           norm_mem_g, xa_w_q, xa_w_kv, xa_w_o, norm_ffn_g, ffn_w1, ffn_w2, final_norm_g):
    B, S, D = x.shape
    n_mem = mem.shape[1]
    assert D == D_MODEL and S % SEQ_TILE == 0 and S % XATTN_TILE == 0 and w_in.shape[0] == 1
    ts = SEQ_TILE
    n_tiles = S // ts

    inv_freq = ROPE_BASE ** (-jnp.arange(0, DK, 2, dtype=F32) / DK)
    inv_lane = jnp.tile(inv_freq, 2 * HEADS)
    ang_off = jnp.arange(ts, dtype=F32)[:, None] * inv_lane[None, :]
    ang_base = (jnp.arange(n_tiles, dtype=F32) * ts)[:, None, None] * inv_lane[None, None, :]
    cos_off, sin_off = jnp.cos(ang_off), jnp.sin(ang_off)
    cos_base, sin_base = jnp.cos(ang_base), jnp.sin(ang_base)

    w_in_r = pl.pallas_call(
        _prep_w_in_kernel,
        grid=(D // PREP_ROWS,),
        in_specs=[pl.BlockSpec((w_in.shape[2], PREP_ROWS), lambda i: (0, i))],
        out_specs=pl.BlockSpec((PREP_ROWS, D_PROJ), lambda i: (i, 0)),
        out_shape=jax.ShapeDtypeStruct((D, D_PROJ), BF16),
        compiler_params=pltpu.CompilerParams(dimension_semantics=("arbitrary",),
                                             vmem_limit_bytes=VMEM_LIMIT_BYTES),
        name="prep_w_in",
    )(jnp.swapaxes(w_in[0], 0, 1))
    w_gate_p = jnp.concatenate(
        [gla_w_gate[0], jnp.zeros((GATE_RANK_PAD - GATE_RANK, HK), F32)], axis=0).astype(BF16)
    dmat_s, qdec_s, kdect_s, sdec_s = _retention_tables()

    params = pltpu.CompilerParams(dimension_semantics=("arbitrary", "arbitrary"),
                                  vmem_limit_bytes=VMEM_LIMIT_BYTES)

    hm = XATTN_HEADS * n_mem
    wqk, wvo = pl.pallas_call(
        _mem_kv_kernel,
        grid=(B,),
        in_specs=[pl.BlockSpec((1, n_mem, D), lambda b: (b, 0, 0)),
                  _const_spec((1, D)), _const_spec((D, 2 * D)), _const_spec((D, D)),
                  _const_spec((D, D))],
        out_specs=[pl.BlockSpec((1, D, hm), lambda b: (b, 0, 0)),
                   pl.BlockSpec((1, hm, D), lambda b: (b, 0, 0))],
        out_shape=[jax.ShapeDtypeStruct((B, D, hm), BF16), jax.ShapeDtypeStruct((B, hm, D), BF16)],
        compiler_params=pltpu.CompilerParams(dimension_semantics=("arbitrary",),
                                             vmem_limit_bytes=VMEM_LIMIT_BYTES),
        name="mem_kv",
    )(mem, norm_mem_g[0][None, :], xa_w_kv[0], xa_w_q[0], xa_w_o[0])

    n_steps = B * n_tiles + 1

    def next_tile(s):
        p = jnp.minimum(s, n_steps - 2)
        return (p // n_tiles, p % n_tiles, 0)

    def prev_tile(s):
        q = jnp.maximum(s - 1, 0)
        return (q // n_tiles, q % n_tiles, 0)

    def prev_rope(s):
        return (jnp.maximum(s - 1, 0) % n_tiles, 0, 0)

    h1 = pl.pallas_call(
        functools.partial(_mixer_kernel, n_chunks=ts // CHUNK, n_tiles=n_tiles),
        grid=(n_steps,),
        in_specs=[pl.BlockSpec((1, ts, D), next_tile),
                  pl.BlockSpec((1, ts, D), prev_tile),
                  pl.BlockSpec((1, 1, HK), prev_rope), pl.BlockSpec((1, 1, HK), prev_rope),
                  _const_spec((ts, HK)), _const_spec((ts, HK)),
                  _const_spec((1, D)), _const_spec((D, D_PROJ)),
                  _const_spec((GATE_RANK_PAD, HK)), _const_spec((1, HK)),
                  _const_spec((HEADS, DV)), _const_spec((HEADS, DV)),
                  _const_spec((D, D)),
                  _const_spec((HEADS * CHUNK, CHUNK)), _const_spec((HEADS * CHUNK, DV)),
                  _const_spec((HEADS * DK, CHUNK)), _const_spec((HEADS * DK, DV))],
        out_specs=pl.BlockSpec((1, ts, D), prev_tile),
        out_shape=jax.ShapeDtypeStruct((B, S, D), F32),
        scratch_shapes=[pltpu.VMEM((ts, D_PROJ), F32), pltpu.VMEM((ts, D_PROJ), F32),
                        pltpu.VMEM((ts, D), BF16), pltpu.VMEM((ts, D), BF16),
                        pltpu.VMEM((3, ts // CHUNK, HK, CHUNK), BF16),
                        pltpu.VMEM((HEADS * DK, DV), F32), pltpu.VMEM((HEADS * DK, DV), F32)],
        compiler_params=pltpu.CompilerParams(dimension_semantics=("arbitrary",),
                                             vmem_limit_bytes=VMEM_LIMIT_BYTES),
        name="mixer",
    )(x, x, cos_base, sin_base, cos_off, sin_off, norm_mix_g[0][None, :], w_in_r, w_gate_p,
      gla_b_gate[0][None, :], gla_norm_g[0], ret_norm_g[0], w_out[0].astype(BF16),
      dmat_s, qdec_s, kdect_s, sdec_s)

    out = pl.pallas_call(
        _xattn_mlp_kernel,
        grid=(B, S // XATTN_TILE),
        in_specs=[pl.BlockSpec((1, XATTN_TILE, D), lambda b, t: (b, t, 0)),
                  pl.BlockSpec((1, D, hm), lambda b, t: (b, 0, 0)),
                  pl.BlockSpec((1, hm, D), lambda b, t: (b, 0, 0)),
                  _const_spec((1, D)),
                  _const_spec((1, D)), _const_spec((D, D_FF)), _const_spec((D_FF, D)),
                  _const_spec((1, D))],
        out_specs=pl.BlockSpec((1, XATTN_TILE, D), lambda b, t: (b, t, 0)),
        out_shape=jax.ShapeDtypeStruct((B, S, D), F32),
        compiler_params=params,
        name="xattn_mlp",
    )(h1, wqk, wvo, norm_x_g[0][None, :], norm_ffn_g[0][None, :], ffn_w1[0].astype(BF16),
      ffn_w2[0].astype(BF16), final_norm_g[None, :])
    return out
```

```python
import functools

import numpy as np
import jax
import jax.numpy as jnp
from jax import lax
from jax.experimental import pallas as pl
from jax.experimental.pallas import tpu as pltpu

D_MODEL = 1024
CHUNK = 64
HEADS = 4
DK = 64
DV = 128
GATE_RANK = 16
GATE_RANK_PAD = 128
GATE_TAU = 16.0
RET_DECAY_EXP_LO = 5.0
RET_DECAY_EXP_HI = 12.0
ROPE_BASE = 10000.0
XATTN_HEADS = 4
XATTN_DH = D_MODEL // XATTN_HEADS
D_FF = 4 * D_MODEL
FF_BLOCK = 1024
XATTN_TILE = 512
XATTN_BLOCKS = (256, 256)
PREP_ROWS = 256
EPS = 1e-6

HK = HEADS * DK
HV = HEADS * DV
C_GQ, C_GK, C_GV, C_GG = 0, HK, 2 * HK, 2 * HK + HV
C_RQ = C_GG + HV
C_RK = C_RQ + HK
C_RV = C_RK + HK
C_RG = C_RV + HV
C_LR = C_RG + HV
D_PROJ = C_LR + GATE_RANK_PAD

SEQ_TILE = 512
PROJ_PIECE = 256
STAGE_A_PIECES = 7
CUM_BLOCK = 256
OUT_ROWS = 512
VMEM_LIMIT_BYTES = 56 * 1024 * 1024

F32 = jnp.float32
BF16 = jnp.bfloat16


def _dot(a, b):
    return jnp.dot(a, b, preferred_element_type=F32)


def _dot_nt(a, b):
    return lax.dot_general(a, b, (((1,), (1,)), ((), ())), preferred_element_type=F32)


def _rms(x, g):
    return x * lax.rsqrt(jnp.mean(x * x, axis=-1, keepdims=True) + EPS) * g


def _swish(g):
    return g / (1.0 + jnp.exp(-g))


def _stack_heads(t):
    lane_head = lax.broadcasted_iota(jnp.int32, t.shape, 1) // DK
    zero = jnp.zeros_like(t)
    return jnp.concatenate([jnp.where(lane_head == h, t, zero) for h in range(HEADS)], axis=0)


def _prep_w_in_kernel(wt_ref, o_ref):
    lr0 = 2 * HK + HV
    o_ref[:, :lr0] = wt_ref[:lr0, :].T.astype(BF16)
    o_ref[:, lr0:C_LR] = wt_ref[lr0 + GATE_RANK:, :].T.astype(BF16)
    tail = wt_ref[lr0:lr0 + GATE_RANK_PAD, :]
    row = lax.broadcasted_iota(jnp.int32, tail.shape, 0)
    o_ref[:, C_LR:] = jnp.where(row < GATE_RANK, tail, 0.0).T.astype(BF16)


def _mem_kv_kernel(mem_ref, g_ref, wkv_ref, wq_ref, wo_ref, wqk_ref, wvo_ref):
    n_mem = mem_ref.shape[1]
    mn = _rms(mem_ref[0], g_ref[...])
    kv = _dot(mn, wkv_ref[...])
    for h in range(XATTN_HEADS):
        cols = slice(h * XATTN_DH, (h + 1) * XATTN_DH)
        mcols = slice(h * n_mem, (h + 1) * n_mem)
        k_h = kv[:, cols]
        v_h = kv[:, D_MODEL + h * XATTN_DH:D_MODEL + (h + 1) * XATTN_DH]
        wqk_ref[0, :, mcols] = (_dot_nt(wq_ref[:, cols], k_h) * (XATTN_DH ** -0.5)).astype(BF16)
        wvo_ref[0, mcols, :] = _dot(v_h, wo_ref[cols, :]).astype(BF16)


def _mixer_kernel(xnext_ref, xres_ref, cosb_ref, sinb_ref, coso_ref, sino_ref, g_ref, win_ref, wg_ref,
                  bg_ref, glan_ref, retn_ref, wout_ref, dmat_ref, qdec_ref, kdect_ref, sdec_ref,
                  o_ref, proj_a_ref, proj_b_ref, xn_ref, y_ref, kt_ref, sg_ref, sr_ref, *,
                  n_chunks, n_tiles):
    s = pl.program_id(0)
    prev_tile = jnp.maximum(s - 1, 0)

    @pl.when(s == 0)
    def _():
        proj_b_ref[...] = jnp.zeros_like(proj_b_ref)

    @pl.when(prev_tile % n_tiles == 0)
    def _():
        sg_ref[...] = jnp.zeros_like(sg_ref)
        sr_ref[...] = jnp.zeros_like(sr_ref)

    chunk_pass = functools.partial(
        _chunk_pass, xnext_ref=xnext_ref, g_ref=g_ref, xn_ref=xn_ref,
        rope_refs=(cosb_ref, sinb_ref, coso_ref, sino_ref),
        win_ref=win_ref, wg_ref=wg_ref, bg_ref=bg_ref, glan_ref=glan_ref, retn_ref=retn_ref,
        dmat_ref=dmat_ref, qdec_ref=qdec_ref, kdect_ref=kdect_ref, sdec_ref=sdec_ref, y_ref=y_ref,
        kt_ref=kt_ref, sg_ref=sg_ref, sr_ref=sr_ref, xres_ref=xres_ref, wout_ref=wout_ref,
        o_ref=o_ref, n_chunks=n_chunks)

    @pl.when(s % 2 == 0)
    def _():
        chunk_pass(proj_a_ref, proj_b_ref)

    @pl.when(s % 2 == 1)
    def _():
        chunk_pass(proj_b_ref, proj_a_ref)


def _chunk_pass(cur_ref, prv_ref, *, xnext_ref, g_ref, xn_ref, rope_refs, win_ref, wg_ref, bg_ref, glan_ref,
                retn_ref, dmat_ref, qdec_ref, kdect_ref, sdec_ref, y_ref, kt_ref, sg_ref, sr_ref,
                xres_ref, wout_ref, o_ref, n_chunks):
    ts = n_chunks * CHUNK
    chunk_rows = [slice(c * CHUNK, (c + 1) * CHUNK) for c in range(n_chunks)]
    head_rows = [slice(h * CHUNK, (h + 1) * CHUNK) for h in range(HEADS)]
    head_cols = [slice(h * DV, (h + 1) * DV) for h in range(HEADS)]
    pieces = [(i * PROJ_PIECE, min((i + 1) * PROJ_PIECE, D_PROJ))
              for i in range(-(-D_PROJ // PROJ_PIECE))]

    def emit_proj(n):
        for _ in range(min(n, len(pieces))):
            lo, hi = pieces.pop(0)
            cur_ref[:, lo:hi] = _dot(xn_ref[...], win_ref[:, lo:hi])

    bi = lax.broadcasted_iota(jnp.int32, (CUM_BLOCK, CUM_BLOCK), 0)
    bj = lax.broadcasted_iota(jnp.int32, (CUM_BLOCK, CUM_BLOCK), 1)
    tril_blocks = ((bi // CHUNK == bj // CHUNK) & (bj <= bi)).astype(BF16)
    si = lax.broadcasted_iota(jnp.int32, (HEADS * CHUNK, CHUNK), 0) % CHUNK
    sj = lax.broadcasted_iota(jnp.int32, (HEADS * CHUNK, CHUNK), 1)
    lower = sj <= si
    first_half = (lax.broadcasted_iota(jnp.int32, (1, HK), 1) % DK) < (DK // 2)
    lane_pad = jnp.zeros((HK, CHUNK), BF16)

    def contributions(kdt, v):
        return jnp.concatenate([_dot(kdt[head_rows[h]], v[:, head_cols[h]]) for h in range(HEADS)],
                               axis=0)

    def scan_states(s_ref, decays, contribs):
        state, starts = s_ref[...], []
        for dec, con in zip(decays, contribs):
            starts.append(state.astype(BF16))
            state = state * dec + con
        s_ref[...] = state
        return starts

    xn_ref[...] = _rms(xnext_ref[0], g_ref[...]).astype(BF16)
    z = _dot(prv_ref[:, C_LR:C_LR + GATE_RANK_PAD].astype(BF16), wg_ref[...]) + bg_ref[...]
    emit_proj(STAGE_A_PIECES)
    log_a = (jnp.minimum(z, 0.0) - jnp.log1p(jnp.exp(-jnp.abs(z)))) * (1.0 / GATE_TAU)
    la_hi = log_a.astype(BF16)
    la_lo = (log_a - la_hi.astype(F32)).astype(BF16)
    la_split = jnp.concatenate([la_hi, la_lo], axis=1)
    bcum = []
    for b in range(ts // CUM_BLOCK):
        r = _dot(tril_blocks, la_split[b * CUM_BLOCK:(b + 1) * CUM_BLOCK])
        bcum.append(r[:, :HK] + r[:, HK:])
    bcum = jnp.concatenate(bcum, axis=0)
    emit_proj(1)
    q = prv_ref[:, C_GQ:C_GQ + HK] * (DK ** -0.5)
    qeb = (q * jnp.exp(bcum)).astype(BF16)
    qebi = (q * jnp.exp(-bcum)).astype(BF16)

    g_decay, g_contrib = [], []
    for c, rows in enumerate(chunk_rows):
        kt = prv_ref[rows, C_GK:C_GK + HK].T
        bct = bcum[rows].T
        blt = bct[:, CHUNK - 1:CHUNK]
        kt_ref[0, c] = (kt * jnp.exp(bct)).astype(BF16)
        kt_ref[1, c] = (kt * jnp.exp(-bct)).astype(BF16)
        kdt = (kt * jnp.exp(blt - bct)).astype(BF16)
        g_decay.append(jnp.exp(blt))
        g_contrib.append(contributions(kdt, prv_ref[rows, C_GV:C_GV + HV].astype(BF16)))
        if c % 2 == 1:
            emit_proj(1)

    cosb, sinb, coso, sino = [r[...] for r in rope_refs]
    cosb, sinb = cosb[0], sinb[0]
    cs = cosb * coso - sinb * sino
    sn = sinb * coso + cosb * sino
    sn = jnp.where(first_half, -sn, sn)

    def rope(t):
        partner = jnp.where(first_half, pltpu.roll(t, HK - DK // 2, 1), pltpu.roll(t, DK // 2, 1))
        return t * cs + partner * sn

    qrb = rope(prv_ref[:, C_RQ:C_RQ + HK]).astype(BF16)
    kr = rope(prv_ref[:, C_RK:C_RK + HK]) * (DK ** -0.5)
    r_contrib = []
    for c, rows in enumerate(chunk_rows):
        kt = kr[rows].T
        kt_ref[2, c] = kt.astype(BF16)
        kdt = (kt * kdect_ref[...]).astype(BF16)
        r_contrib.append(contributions(kdt, prv_ref[rows, C_RV:C_RV + HV].astype(BF16)))
        if c % 2 == 1:
            emit_proj(1)

    g_start = scan_states(sg_ref, g_decay, g_contrib)
    r_start = scan_states(sr_ref, [sdec_ref[...]] * n_chunks, r_contrib)

    def gla_scores(c):
        rows = chunk_rows[c]
        r = _dot(_stack_heads(qeb[rows]), jnp.concatenate([g_start[c], kt_ref[1, c], lane_pad], axis=1))
        a_up = _dot(_stack_heads(qebi[rows]), kt_ref[0, c])
        return r[:, :DV], jnp.where(lower, r[:, DV:DV + CHUNK], a_up).astype(BF16)

    def ret_scores(c):
        r = _dot(_stack_heads(qrb[chunk_rows[c]]),
                 jnp.concatenate([r_start[c], kt_ref[2, c], lane_pad], axis=1))
        return r[:, :DV] * qdec_ref[...], (r[:, DV:DV + CHUNK] * dmat_ref[...]).astype(BF16)

    def gla_values(c, inter, attn):
        rows = chunk_rows[c]
        v = prv_ref[rows, C_GV:C_GV + HV].astype(BF16)
        for h in range(HEADS):
            o = _dot(attn[head_rows[h]], v[:, head_cols[h]]) + inter[head_rows[h]]
            gate = prv_ref[rows, C_GG + h * DV:C_GG + (h + 1) * DV]
            yn = o * lax.rsqrt(jnp.mean(o * o, axis=-1, keepdims=True) + EPS) * glan_ref[h:h + 1, :]
            y_ref[rows, h * DV:(h + 1) * DV] = (yn * _swish(gate)).astype(BF16)

    def ret_values(c, inter, scores):
        rows = chunk_rows[c]
        v = prv_ref[rows, C_RV:C_RV + HV].astype(BF16)
        for h in range(HEADS):
            o = _dot(scores[head_rows[h]], v[:, head_cols[h]]) + inter[head_rows[h]]
            gate = prv_ref[rows, C_RG + h * DV:C_RG + (h + 1) * DV]
            d = o - jnp.mean(o, axis=-1, keepdims=True)
            yn = d * lax.rsqrt(jnp.mean(d * d, axis=-1, keepdims=True) + EPS) * retn_ref[h:h + 1, :]
            y_ref[rows, HV + h * DV:HV + (h + 1) * DV] = (yn * _swish(gate)).astype(BF16)

    def out_proj(done):
        o_ref[0, done, :] = xres_ref[0, done, :] + _dot(y_ref[done, :], wout_ref[...])

    finished = None
    ahead = (gla_scores(0), ret_scores(0))
    for c in range(n_chunks):
        gla_ahead, ret_ahead = ahead
        if c + 1 < n_chunks:
            ahead = (gla_scores(c + 1), ret_scores(c + 1))
        if finished is not None:
            out_proj(finished)
            finished = None
        else:
            emit_proj(1)
        gla_values(c, *gla_ahead)
        ret_values(c, *ret_ahead)
        if (c + 1) % (OUT_ROWS // CHUNK) == 0:
            finished = slice((c + 1) * CHUNK - OUT_ROWS, (c + 1) * CHUNK)
    emit_proj(len(pieces))
    out_proj(finished)


def _xattn_mlp_kernel(h_ref, wqk_ref, wvo_ref, gx_ref, gf_ref, w1_ref, w2_ref, gfin_ref, o_ref):
    bounds = np.cumsum((0,) + XATTN_BLOCKS)
    blocks = [slice(int(a), int(b)) for a, b in zip(bounds[:-1], bounds[1:])]
    n_mem = wqk_ref.shape[2] // XATTN_HEADS
    h = [h_ref[0, rows, :] for rows in blocks]
    scores = [_dot(_rms(hb, gx_ref[...]).astype(BF16), wqk_ref[0]) for hb in h]
    probs = []
    for sb in scores:
        heads = []
        for a in range(XATTN_HEADS):
            s = sb[:, a * n_mem:(a + 1) * n_mem]
            e = jnp.exp(s - jnp.max(s, axis=-1, keepdims=True))
            heads.append((e * (1.0 / jnp.sum(e, axis=-1, keepdims=True))).astype(BF16))
        probs.append(jnp.concatenate(heads, axis=1))
    h = [hb + _dot(pb, wvo_ref[0]) for hb, pb in zip(h, probs)]

    hn = [_rms(hb, gf_ref[...]).astype(BF16) for hb in h]
    for c in range(D_FF // FF_BLOCK):
        cols = slice(c * FF_BLOCK, (c + 1) * FF_BLOCK)
        u = [jnp.maximum(_dot(hb, w1_ref[:, cols]), 0.0) for hb in hn]
        h = [hb + _dot((ub * ub).astype(BF16), w2_ref[cols, :]) for hb, ub in zip(h, u)]
    for rows, hb in zip(blocks, h):
        o_ref[0, rows, :] = _rms(hb, gfin_ref[...])


def _const_spec(shape):
    n = len(shape)
    return pl.BlockSpec(shape, lambda *_: (0,) * n, pipeline_mode=pl.Buffered(1))


def _retention_tables():
    gamma = 1.0 - np.exp2(-np.linspace(RET_DECAY_EXP_LO, RET_DECAY_EXP_HI, HEADS, dtype=np.float32))
    log_g = np.log(gamma.astype(np.float32)).astype(np.float32)
    pos = np.arange(CHUNK, dtype=np.float32)
    dmat = np.exp(np.abs(pos[:, None] - pos[None, :])[None] * log_g[:, None, None])
    k_dec = np.exp((CHUNK - 1.0 - pos)[None, :] * log_g[:, None])
    q_dec = np.exp((pos + 1.0)[None, :] * log_g[:, None])
    s_dec = np.exp(CHUNK * log_g)
    dmat_s = dmat.reshape(HEADS * CHUNK, CHUNK)
    qdec_s = np.broadcast_to(q_dec.reshape(HEADS * CHUNK, 1), (HEADS * CHUNK, DV))
    kdect_s = np.repeat(k_dec, DK, axis=0)
    sdec_s = np.broadcast_to(np.repeat(s_dec, DK)[:, None], (HEADS * DK, DV))
    as32 = lambda a: jnp.asarray(np.ascontiguousarray(a), dtype=F32)
    return as32(dmat_s), as32(qdec_s), as32(kdect_s), as32(sdec_s)


def kernel(x, mem, norm_mix_g, w_in, gla_w_gate, gla_b_gate, gla_norm_g, ret_norm_g, w_out, norm_x_g,
           norm_mem_g, xa_w_q, xa_w_kv, xa_w_o, norm_ffn_g, ffn_w1, ffn_w2, final_norm_g):
    B, S, D = x.shape
    n_mem = mem.shape[1]
    assert D == D_MODEL and S % SEQ_TILE == 0 and S % XATTN_TILE == 0 and w_in.shape[0] == 1
    ts = SEQ_TILE
    n_tiles = S // ts

    inv_freq = ROPE_BASE ** (-jnp.arange(0, DK, 2, dtype=F32) / DK)
    inv_lane = jnp.tile(inv_freq, 2 * HEADS)
    ang_off = jnp.arange(ts, dtype=F32)[:, None] * inv_lane[None, :]
    ang_base = (jnp.arange(n_tiles, dtype=F32) * ts)[:, None, None] * inv_lane[None, None, :]
    cos_off, sin_off = jnp.cos(ang_off), jnp.sin(ang_off)
    cos_base, sin_base = jnp.cos(ang_base), jnp.sin(ang_base)

    w_in_r = pl.pallas_call(
        _prep_w_in_kernel,
        grid=(D // PREP_ROWS,),
        in_specs=[pl.BlockSpec((w_in.shape[2], PREP_ROWS), lambda i: (0, i))],
        out_specs=pl.BlockSpec((PREP_ROWS, D_PROJ), lambda i: (i, 0)),
        out_shape=jax.ShapeDtypeStruct((D, D_PROJ), BF16),
        compiler_params=pltpu.CompilerParams(dimension_semantics=("arbitrary",),
                                             vmem_limit_bytes=VMEM_LIMIT_BYTES),
        name="prep_w_in",
    )(jnp.swapaxes(w_in[0], 0, 1))
    w_gate_p = jnp.concatenate(
        [gla_w_gate[0], jnp.zeros((GATE_RANK_PAD - GATE_RANK, HK), F32)], axis=0).astype(BF16)
    dmat_s, qdec_s, kdect_s, sdec_s = _retention_tables()

    params = pltpu.CompilerParams(dimension_semantics=("arbitrary", "arbitrary"),
                                  vmem_limit_bytes=VMEM_LIMIT_BYTES)

    hm = XATTN_HEADS * n_mem
    wqk, wvo = pl.pallas_call(
        _mem_kv_kernel,
        grid=(B,),
        in_specs=[pl.BlockSpec((1, n_mem, D), lambda b: (b, 0, 0)),
                  _const_spec((1, D)), _const_spec((D, 2 * D)), _const_spec((D, D)),
                  _const_spec((D, D))],
        out_specs=[pl.BlockSpec((1, D, hm), lambda b: (b, 0, 0)),
                   pl.BlockSpec((1, hm, D), lambda b: (b, 0, 0))],
        out_shape=[jax.ShapeDtypeStruct((B, D, hm), BF16), jax.ShapeDtypeStruct((B, hm, D), BF16)],
        compiler_params=pltpu.CompilerParams(dimension_semantics=("arbitrary",),
                                             vmem_limit_bytes=VMEM_LIMIT_BYTES),
        name="mem_kv",
    )(mem, norm_mem_g[0][None, :], xa_w_kv[0], xa_w_q[0], xa_w_o[0])

    n_steps = B * n_tiles + 1

    def next_tile(s):
        p = jnp.minimum(s, n_steps - 2)
        return (p // n_tiles, p % n_tiles, 0)

    def prev_tile(s):
        q = jnp.maximum(s - 1, 0)
        return (q // n_tiles, q % n_tiles, 0)

    def prev_rope(s):
        return (jnp.maximum(s - 1, 0) % n_tiles, 0, 0)

    h1 = pl.pallas_call(
        functools.partial(_mixer_kernel, n_chunks=ts // CHUNK, n_tiles=n_tiles),
        grid=(n_steps,),
        in_specs=[pl.BlockSpec((1, ts, D), next_tile),
                  pl.BlockSpec((1, ts, D), prev_tile),
                  pl.BlockSpec((1, 1, HK), prev_rope), pl.BlockSpec((1, 1, HK), prev_rope),
                  _const_spec((ts, HK)), _const_spec((ts, HK)),
                  _const_spec((1, D)), _const_spec((D, D_PROJ)),
                  _const_spec((GATE_RANK_PAD, HK)), _const_spec((1, HK)),
                  _const_spec((HEADS, DV)), _const_spec((HEADS, DV)),
                  _const_spec((D, D)),
                  _const_spec((HEADS * CHUNK, CHUNK)), _const_spec((HEADS * CHUNK, DV)),
                  _const_spec((HEADS * DK, CHUNK)), _const_spec((HEADS * DK, DV))],
        out_specs=pl.BlockSpec((1, ts, D), prev_tile),
        out_shape=jax.ShapeDtypeStruct((B, S, D), F32),
        scratch_shapes=[pltpu.VMEM((ts, D_PROJ), F32), pltpu.VMEM((ts, D_PROJ), F32),
                        pltpu.VMEM((ts, D), BF16), pltpu.VMEM((ts, D), BF16),
                        pltpu.VMEM((3, ts // CHUNK, HK, CHUNK), BF16),
                        pltpu.VMEM((HEADS * DK, DV), F32), pltpu.VMEM((HEADS * DK, DV), F32)],
        compiler_params=pltpu.CompilerParams(dimension_semantics=("arbitrary",),
                                             vmem_limit_bytes=VMEM_LIMIT_BYTES),
        name="mixer",
    )(x, x, cos_base, sin_base, cos_off, sin_off, norm_mix_g[0][None, :], w_in_r, w_gate_p,
      gla_b_gate[0][None, :], gla_norm_g[0], ret_norm_g[0], w_out[0].astype(BF16),
      dmat_s, qdec_s, kdect_s, sdec_s)

    out = pl.pallas_call(
        _xattn_mlp_kernel,
        grid=(B, S // XATTN_TILE),
        in_specs=[pl.BlockSpec((1, XATTN_TILE, D), lambda b, t: (b, t, 0)),
                  pl.BlockSpec((1, D, hm), lambda b, t: (b, 0, 0)),
                  pl.BlockSpec((1, hm, D), lambda b, t: (b, 0, 0)),
                  _const_spec((1, D)),
                  _const_spec((1, D)), _const_spec((D, D_FF)), _const_spec((D_FF, D)),
                  _const_spec((1, D))],
        out_specs=pl.BlockSpec((1, XATTN_TILE, D), lambda b, t: (b, t, 0)),
        out_shape=jax.ShapeDtypeStruct((B, S, D), F32),
        compiler_params=params,
        name="xattn_mlp",
    )(h1, wqk, wvo, norm_x_g[0][None, :], norm_ffn_g[0][None, :], ffn_w1[0].astype(BF16),
      ffn_w2[0].astype(BF16), final_norm_g[None, :])
    return out
```

```python
import functools

import numpy as np
import jax
import jax.numpy as jnp
from jax import lax
from jax.experimental import pallas as pl
from jax.experimental.pallas import tpu as pltpu

D_MODEL = 1024
CHUNK = 64
HEADS = 4
DK = 64
DV = 128
GATE_RANK = 16
GATE_RANK_PAD = 128
GATE_TAU = 16.0
RET_DECAY_EXP_LO = 5.0
RET_DECAY_EXP_HI = 12.0
ROPE_BASE = 10000.0
XATTN_HEADS = 4
XATTN_DH = D_MODEL // XATTN_HEADS
D_FF = 4 * D_MODEL
FF_BLOCK = 1024
XATTN_TILE = 512
XATTN_BLOCKS = (256, 256)
PREP_ROWS = 256
EPS = 1e-6

HK = HEADS * DK
HV = HEADS * DV
C_GQ, C_GK, C_GV, C_GG = 0, HK, 2 * HK, 2 * HK + HV
C_RQ = C_GG + HV
C_RK = C_RQ + HK
C_RV = C_RK + HK
C_RG = C_RV + HV
C_LR = C_RG + HV
D_PROJ = C_LR + GATE_RANK_PAD

SEQ_TILE = 512
PROJ_PIECE = 256
STAGE_A_PIECES = 3
CUM_BLOCK = 256
OUT_ROWS = 512
VMEM_LIMIT_BYTES = 56 * 1024 * 1024

F32 = jnp.float32
BF16 = jnp.bfloat16


def _dot(a, b):
    return jnp.dot(a, b, preferred_element_type=F32)


def _dot_nt(a, b):
    return lax.dot_general(a, b, (((1,), (1,)), ((), ())), preferred_element_type=F32)


def _rms(x, g):
    return x * lax.rsqrt(jnp.mean(x * x, axis=-1, keepdims=True) + EPS) * g


def _swish(g):
    return g / (1.0 + jnp.exp(-g))


def _stack_heads(t):
    lane_head = lax.broadcasted_iota(jnp.int32, t.shape, 1) // DK
    zero = jnp.zeros_like(t)
    return jnp.concatenate([jnp.where(lane_head == h, t, zero) for h in range(HEADS)], axis=0)


def _prep_w_in_kernel(wt_ref, o_ref):
    lr0 = 2 * HK + HV
    o_ref[:, :lr0] = wt_ref[:lr0, :].T.astype(BF16)
    o_ref[:, lr0:C_LR] = wt_ref[lr0 + GATE_RANK:, :].T.astype(BF16)
    tail = wt_ref[lr0:lr0 + GATE_RANK_PAD, :]
    row = lax.broadcasted_iota(jnp.int32, tail.shape, 0)
    o_ref[:, C_LR:] = jnp.where(row < GATE_RANK, tail, 0.0).T.astype(BF16)


def _mem_kv_kernel(mem_ref, g_ref, wkv_ref, wq_ref, wo_ref, wqk_ref, wvo_ref):
    n_mem = mem_ref.shape[1]
    mn = _rms(mem_ref[0], g_ref[...])
    kv = _dot(mn, wkv_ref[...])
    for h in range(XATTN_HEADS):
        cols = slice(h * XATTN_DH, (h + 1) * XATTN_DH)
        mcols = slice(h * n_mem, (h + 1) * n_mem)
        k_h = kv[:, cols]
        v_h = kv[:, D_MODEL + h * XATTN_DH:D_MODEL + (h + 1) * XATTN_DH]
        wqk_ref[0, :, mcols] = (_dot_nt(wq_ref[:, cols], k_h) * (XATTN_DH ** -0.5)).astype(BF16)
        wvo_ref[0, mcols, :] = _dot(v_h, wo_ref[cols, :]).astype(BF16)


def _mixer_kernel(xnext_ref, xres_ref, cosb_ref, sinb_ref, coso_ref, sino_ref, g_ref, win_ref, wg_ref,
                  bg_ref, glan_ref, retn_ref, wout_ref, dmat_ref, qdec_ref, kdect_ref, sdec_ref,
                  o_ref, proj_a_ref, proj_b_ref, xn_ref, y_ref, kt_ref, sg_ref, sr_ref, *,
                  n_chunks, n_tiles):
    s = pl.program_id(0)
    prev_tile = jnp.maximum(s - 1, 0)

    @pl.when(s == 0)
    def _():
        proj_b_ref[...] = jnp.zeros_like(proj_b_ref)

    @pl.when(prev_tile % n_tiles == 0)
    def _():
        sg_ref[...] = jnp.zeros_like(sg_ref)
        sr_ref[...] = jnp.zeros_like(sr_ref)

    chunk_pass = functools.partial(
        _chunk_pass, xnext_ref=xnext_ref, g_ref=g_ref, xn_ref=xn_ref,
        rope_refs=(cosb_ref, sinb_ref, coso_ref, sino_ref),
        win_ref=win_ref, wg_ref=wg_ref, bg_ref=bg_ref, glan_ref=glan_ref, retn_ref=retn_ref,
        dmat_ref=dmat_ref, qdec_ref=qdec_ref, kdect_ref=kdect_ref, sdec_ref=sdec_ref, y_ref=y_ref,
        kt_ref=kt_ref, sg_ref=sg_ref, sr_ref=sr_ref, xres_ref=xres_ref, wout_ref=wout_ref,
        o_ref=o_ref, n_chunks=n_chunks)

    @pl.when(s % 2 == 0)
    def _():
        chunk_pass(proj_a_ref, proj_b_ref)

    @pl.when(s % 2 == 1)
    def _():
        chunk_pass(proj_b_ref, proj_a_ref)


def _chunk_pass(cur_ref, prv_ref, *, xnext_ref, g_ref, xn_ref, rope_refs, win_ref, wg_ref, bg_ref, glan_ref,
                retn_ref, dmat_ref, qdec_ref, kdect_ref, sdec_ref, y_ref, kt_ref, sg_ref, sr_ref,
                xres_ref, wout_ref, o_ref, n_chunks):
    ts = n_chunks * CHUNK
    chunk_rows = [slice(c * CHUNK, (c + 1) * CHUNK) for c in range(n_chunks)]
    head_rows = [slice(h * CHUNK, (h + 1) * CHUNK) for h in range(HEADS)]
    head_cols = [slice(h * DV, (h + 1) * DV) for h in range(HEADS)]
    pieces = [(i * PROJ_PIECE, min((i + 1) * PROJ_PIECE, D_PROJ))
              for i in range(-(-D_PROJ // PROJ_PIECE))]

    def emit_proj(n):
        for _ in range(min(n, len(pieces))):
            lo, hi = pieces.pop(0)
            cur_ref[:, lo:hi] = _dot(xn_ref[...], win_ref[:, lo:hi])

    bi = lax.broadcasted_iota(jnp.int32, (CUM_BLOCK, CUM_BLOCK), 0)
    bj = lax.broadcasted_iota(jnp.int32, (CUM_BLOCK, CUM_BLOCK), 1)
    tril_blocks = ((bi // CHUNK == bj // CHUNK) & (bj <= bi)).astype(BF16)
    si = lax.broadcasted_iota(jnp.int32, (HEADS * CHUNK, CHUNK), 0) % CHUNK
    sj = lax.broadcasted_iota(jnp.int32, (HEADS * CHUNK, CHUNK), 1)
    lower = sj <= si
    first_half = (lax.broadcasted_iota(jnp.int32, (1, HK), 1) % DK) < (DK // 2)
    lane_pad = jnp.zeros((HK, CHUNK), BF16)

    def contributions(kdt, v):
        return jnp.concatenate([_dot(kdt[head_rows[h]], v[:, head_cols[h]]) for h in range(HEADS)],
                               axis=0)

    def scan_states(s_ref, decays, contribs):
        state, starts = s_ref[...], []
        for dec, con in zip(decays, contribs):
            starts.append(state.astype(BF16))
            state = state * dec + con
        s_ref[...] = state
        return starts

    xn_ref[...] = _rms(xnext_ref[0], g_ref[...]).astype(BF16)
    z = _dot(prv_ref[:, C_LR:C_LR + GATE_RANK_PAD].astype(BF16), wg_ref[...]) + bg_ref[...]
    emit_proj(STAGE_A_PIECES)
    log_a = (jnp.minimum(z, 0.0) - jnp.log1p(jnp.exp(-jnp.abs(z)))) * (1.0 / GATE_TAU)
    la_hi = log_a.astype(BF16)
    la_lo = (log_a - la_hi.astype(F32)).astype(BF16)
    la_split = jnp.concatenate([la_hi, la_lo], axis=1)
    bcum = []
    for b in range(ts // CUM_BLOCK):
        r = _dot(tril_blocks, la_split[b * CUM_BLOCK:(b + 1) * CUM_BLOCK])
        bcum.append(r[:, :HK] + r[:, HK:])
    bcum = jnp.concatenate(bcum, axis=0)
    emit_proj(1)
    q = prv_ref[:, C_GQ:C_GQ + HK] * (DK ** -0.5)
    qeb = (q * jnp.exp(bcum)).astype(BF16)
    qebi = (q * jnp.exp(-bcum)).astype(BF16)

    g_decay, g_contrib = [], []
    for c, rows in enumerate(chunk_rows):
        kt = prv_ref[rows, C_GK:C_GK + HK].T
        bct = bcum[rows].T
        blt = bct[:, CHUNK - 1:CHUNK]
        kt_ref[0, c] = (kt * jnp.exp(bct)).astype(BF16)
        kt_ref[1, c] = (kt * jnp.exp(-bct)).astype(BF16)
        kdt = (kt * jnp.exp(blt - bct)).astype(BF16)
        g_decay.append(jnp.exp(blt))
        g_contrib.append(contributions(kdt, prv_ref[rows, C_GV:C_GV + HV].astype(BF16)))
        if c % 2 == 1:
            emit_proj(1)

    cosb, sinb, coso, sino = [r[...] for r in rope_refs]
    cosb, sinb = cosb[0], sinb[0]
    cs = cosb * coso - sinb * sino
    sn = sinb * coso + cosb * sino
    sn = jnp.where(first_half, -sn, sn)

    def rope(t):
        partner = jnp.where(first_half, pltpu.roll(t, HK - DK // 2, 1), pltpu.roll(t, DK // 2, 1))
        return t * cs + partner * sn

    qrb = rope(prv_ref[:, C_RQ:C_RQ + HK]).astype(BF16)
    kr = rope(prv_ref[:, C_RK:C_RK + HK]) * (DK ** -0.5)
    r_contrib = []
    for c, rows in enumerate(chunk_rows):
        kt = kr[rows].T
        kt_ref[2, c] = kt.astype(BF16)
        kdt = (kt * kdect_ref[...]).astype(BF16)
        r_contrib.append(contributions(kdt, prv_ref[rows, C_RV:C_RV + HV].astype(BF16)))
        if c % 2 == 1:
            emit_proj(1)

    g_start = scan_states(sg_ref, g_decay, g_contrib)
    r_start = scan_states(sr_ref, [sdec_ref[...]] * n_chunks, r_contrib)

    def gla_scores(c):
        rows = chunk_rows[c]
        r = _dot(_stack_heads(qeb[rows]), jnp.concatenate([g_start[c], kt_ref[1, c], lane_pad], axis=1))
        a_up = _dot(_stack_heads(qebi[rows]), kt_ref[0, c])
        return r[:, :DV], jnp.where(lower, r[:, DV:DV + CHUNK], a_up).astype(BF16)

    def ret_scores(c):
        r = _dot(_stack_heads(qrb[chunk_rows[c]]),
                 jnp.concatenate([r_start[c], kt_ref[2, c], lane_pad], axis=1))
        return r[:, :DV] * qdec_ref[...], (r[:, DV:DV + CHUNK] * dmat_ref[...]).astype(BF16)

    def gla_values(c, inter, attn):
        rows = chunk_rows[c]
        v = prv_ref[rows, C_GV:C_GV + HV].astype(BF16)
        for h in range(HEADS):
            o = _dot(attn[head_rows[h]], v[:, head_cols[h]]) + inter[head_rows[h]]
            gate = prv_ref[rows, C_GG + h * DV:C_GG + (h + 1) * DV]
            yn = o * lax.rsqrt(jnp.mean(o * o, axis=-1, keepdims=True) + EPS) * glan_ref[h:h + 1, :]
            y_ref[rows, h * DV:(h + 1) * DV] = (yn * _swish(gate)).astype(BF16)

    def ret_values(c, inter, scores):
        rows = chunk_rows[c]
        v = prv_ref[rows, C_RV:C_RV + HV].astype(BF16)
        for h in range(HEADS):
            o = _dot(scores[head_rows[h]], v[:, head_cols[h]]) + inter[head_rows[h]]
            gate = prv_ref[rows, C_RG + h * DV:C_RG + (h + 1) * DV]
            d = o - jnp.mean(o, axis=-1, keepdims=True)
            yn = d * lax.rsqrt(jnp.mean(d * d, axis=-1, keepdims=True) + EPS) * retn_ref[h:h + 1, :]
            y_ref[rows, HV + h * DV:HV + (h + 1) * DV] = (yn * _swish(gate)).astype(BF16)

    def out_proj(done):
        o_ref[0, done, :] = xres_ref[0, done, :] + _dot(y_ref[done, :], wout_ref[...])

    finished = None
    ahead = (gla_scores(0), ret_scores(0))
    for c in range(n_chunks):
        gla_ahead, ret_ahead = ahead
        if c + 1 < n_chunks:
            ahead = (gla_scores(c + 1), ret_scores(c + 1))
        if finished is not None:
            out_proj(finished)
            finished = None
        else:
            emit_proj(1)
        gla_values(c, *gla_ahead)
        ret_values(c, *ret_ahead)
        if (c + 1) % (OUT_ROWS // CHUNK) == 0:
            finished = slice((c + 1) * CHUNK - OUT_ROWS, (c + 1) * CHUNK)
    emit_proj(len(pieces))
    out_proj(finished)


def _xattn_mlp_kernel(h_ref, wqk_ref, wvo_ref, gx_ref, gf_ref, w1_ref, w2_ref, gfin_ref, o_ref):
    bounds = np.cumsum((0,) + XATTN_BLOCKS)
    blocks = [slice(int(a), int(b)) for a, b in zip(bounds[:-1], bounds[1:])]
    n_mem = wqk_ref.shape[2] // XATTN_HEADS
    h = [h_ref[0, rows, :] for rows in blocks]
    scores = [_dot(_rms(hb, gx_ref[...]).astype(BF16), wqk_ref[0]) for hb in h]
    probs = []
    for sb in scores:
        heads = []
        for a in range(XATTN_HEADS):
            s = sb[:, a * n_mem:(a + 1) * n_mem]
            e = jnp.exp(s - jnp.max(s, axis=-1, keepdims=True))
            heads.append((e * (1.0 / jnp.sum(e, axis=-1, keepdims=True))).astype(BF16))
        probs.append(jnp.concatenate(heads, axis=1))
    h = [hb + _dot(pb, wvo_ref[0]) for hb, pb in zip(h, probs)]

    hn = [_rms(hb, gf_ref[...]).astype(BF16) for hb in h]
    for c in range(D_FF // FF_BLOCK):
        cols = slice(c * FF_BLOCK, (c + 1) * FF_BLOCK)
        u = [jnp.maximum(_dot(hb, w1_ref[:, cols]), 0.0) for hb in hn]
        h = [hb + _dot((ub * ub).astype(BF16), w2_ref[cols, :]) for hb, ub in zip(h, u)]
    for rows, hb in zip(blocks, h):
        o_ref[0, rows, :] = _rms(hb, gfin_ref[...])


def _const_spec(shape):
    n = len(shape)
    return pl.BlockSpec(shape, lambda *_: (0,) * n, pipeline_mode=pl.Buffered(1))


def _retention_tables():
    gamma = 1.0 - np.exp2(-np.linspace(RET_DECAY_EXP_LO, RET_DECAY_EXP_HI, HEADS, dtype=np.float32))
    log_g = np.log(gamma.astype(np.float32)).astype(np.float32)
    pos = np.arange(CHUNK, dtype=np.float32)
    dmat = np.exp(np.abs(pos[:, None] - pos[None, :])[None] * log_g[:, None, None])
    k_dec = np.exp((CHUNK - 1.0 - pos)[None, :] * log_g[:, None])
    q_dec = np.exp((pos + 1.0)[None, :] * log_g[:, None])
    s_dec = np.exp(CHUNK * log_g)
    dmat_s = dmat.reshape(HEADS * CHUNK, CHUNK)
    qdec_s = np.broadcast_to(q_dec.reshape(HEADS * CHUNK, 1), (HEADS * CHUNK, DV))
    kdect_s = np.repeat(k_dec, DK, axis=0)
    sdec_s = np.broadcast_to(np.repeat(s_dec, DK)[:, None], (HEADS * DK, DV))
    as32 = lambda a: jnp.asarray(np.ascontiguousarray(a), dtype=F32)
    return as32(dmat_s), as32(qdec_s), as32(kdect_s), as32(sdec_s)


def kernel(x, mem, norm_mix_g, w_in, gla_w_gate, gla_b_gate, gla_norm_g, ret_norm_g, w_out, norm_x_g,
           norm_mem_g, xa_w_q, xa_w_kv, xa_w_o, norm_ffn_g, ffn_w1, ffn_w2, final_norm_g):
    B, S, D = x.shape
    n_mem = mem.shape[1]
    assert D == D_MODEL and S % SEQ_TILE == 0 and S % XATTN_TILE == 0 and w_in.shape[0] == 1
    ts = SEQ_TILE
    n_tiles = S // ts

    inv_freq = ROPE_BASE ** (-jnp.arange(0, DK, 2, dtype=F32) / DK)
    inv_lane = jnp.tile(inv_freq, 2 * HEADS)
    ang_off = jnp.arange(ts, dtype=F32)[:, None] * inv_lane[None, :]
    ang_base = (jnp.arange(n_tiles, dtype=F32) * ts)[:, None, None] * inv_lane[None, None, :]
    cos_off, sin_off = jnp.cos(ang_off), jnp.sin(ang_off)
    cos_base, sin_base = jnp.cos(ang_base), jnp.sin(ang_base)

    w_in_r = pl.pallas_call(
        _prep_w_in_kernel,
        grid=(D // PREP_ROWS,),
        in_specs=[pl.BlockSpec((w_in.shape[2], PREP_ROWS), lambda i: (0, i))],
        out_specs=pl.BlockSpec((PREP_ROWS, D_PROJ), lambda i: (i, 0)),
        out_shape=jax.ShapeDtypeStruct((D, D_PROJ), BF16),
        compiler_params=pltpu.CompilerParams(dimension_semantics=("arbitrary",),
                                             vmem_limit_bytes=VMEM_LIMIT_BYTES),
        name="prep_w_in",
    )(jnp.swapaxes(w_in[0], 0, 1))
    w_gate_p = jnp.concatenate(
        [gla_w_gate[0], jnp.zeros((GATE_RANK_PAD - GATE_RANK, HK), F32)], axis=0).astype(BF16)
    dmat_s, qdec_s, kdect_s, sdec_s = _retention_tables()

    params = pltpu.CompilerParams(dimension_semantics=("arbitrary", "arbitrary"),
                                  vmem_limit_bytes=VMEM_LIMIT_BYTES)

    hm = XATTN_HEADS * n_mem
    wqk, wvo = pl.pallas_call(
        _mem_kv_kernel,
        grid=(B,),
        in_specs=[pl.BlockSpec((1, n_mem, D), lambda b: (b, 0, 0)),
                  _const_spec((1, D)), _const_spec((D, 2 * D)), _const_spec((D, D)),
                  _const_spec((D, D))],
        out_specs=[pl.BlockSpec((1, D, hm), lambda b: (b, 0, 0)),
                   pl.BlockSpec((1, hm, D), lambda b: (b, 0, 0))],
        out_shape=[jax.ShapeDtypeStruct((B, D, hm), BF16), jax.ShapeDtypeStruct((B, hm, D), BF16)],
        compiler_params=pltpu.CompilerParams(dimension_semantics=("arbitrary",),
                                             vmem_limit_bytes=VMEM_LIMIT_BYTES),
        name="mem_kv",
    )(mem, norm_mem_g[0][None, :], xa_w_kv[0], xa_w_q[0], xa_w_o[0])

    n_steps = B * n_tiles + 1

    def next_tile(s):
        p = jnp.minimum(s, n_steps - 2)
        return (p // n_tiles, p % n_tiles, 0)

    def prev_tile(s):
        q = jnp.maximum(s - 1, 0)
        return (q // n_tiles, q % n_tiles, 0)

    def prev_rope(s):
        return (jnp.maximum(s - 1, 0) % n_tiles, 0, 0)

    h1 = pl.pallas_call(
        functools.partial(_mixer_kernel, n_chunks=ts // CHUNK, n_tiles=n_tiles),
        grid=(n_steps,),
        in_specs=[pl.BlockSpec((1, ts, D), next_tile),
                  pl.BlockSpec((1, ts, D), prev_tile),
                  pl.BlockSpec((1, 1, HK), prev_rope), pl.BlockSpec((1, 1, HK), prev_rope),
                  _const_spec((ts, HK)), _const_spec((ts, HK)),
                  _const_spec((1, D)), _const_spec((D, D_PROJ)),
                  _const_spec((GATE_RANK_PAD, HK)), _const_spec((1, HK)),
                  _const_spec((HEADS, DV)), _const_spec((HEADS, DV)),
                  _const_spec((D, D)),
                  _const_spec((HEADS * CHUNK, CHUNK)), _const_spec((HEADS * CHUNK, DV)),
                  _const_spec((HEADS * DK, CHUNK)), _const_spec((HEADS * DK, DV))],
        out_specs=pl.BlockSpec((1, ts, D), prev_tile),
        out_shape=jax.ShapeDtypeStruct((B, S, D), F32),
        scratch_shapes=[pltpu.VMEM((ts, D_PROJ), F32), pltpu.VMEM((ts, D_PROJ), F32),
                        pltpu.VMEM((ts, D), BF16), pltpu.VMEM((ts, D), BF16),
                        pltpu.VMEM((3, ts // CHUNK, HK, CHUNK), BF16),
                        pltpu.VMEM((HEADS * DK, DV), F32), pltpu.VMEM((HEADS * DK, DV), F32)],
        compiler_params=pltpu.CompilerParams(dimension_semantics=("arbitrary",),
                                             vmem_limit_bytes=VMEM_LIMIT_BYTES),
        name="mixer",
    )(x, x, cos_base, sin_base, cos_off, sin_off, norm_mix_g[0][None, :], w_in_r, w_gate_p,
      gla_b_gate[0][None, :], gla_norm_g[0], ret_norm_g[0], w_out[0].astype(BF16),
      dmat_s, qdec_s, kdect_s, sdec_s)

    out = pl.pallas_call(
        _xattn_mlp_kernel,
        grid=(B, S // XATTN_TILE),
        in_specs=[pl.BlockSpec((1, XATTN_TILE, D), lambda b, t: (b, t, 0)),
                  pl.BlockSpec((1, D, hm), lambda b, t: (b, 0, 0)),
                  pl.BlockSpec((1, hm, D), lambda b, t: (b, 0, 0)),
                  _const_spec((1, D)),
                  _const_spec((1, D)), _const_spec((D, D_FF)), _const_spec((D_FF, D)),
                  _const_spec((1, D))],
        out_specs=pl.BlockSpec((1, XATTN_TILE, D), lambda b, t: (b, t, 0)),
        out_shape=jax.ShapeDtypeStruct((B, S, D), F32),
        compiler_params=params,
        name="xattn_mlp",
    )(h1, wqk, wvo, norm_x_g[0][None, :], norm_ffn_g[0][None, :], ffn_w1[0].astype(BF16),
      ffn_w2[0].astype(BF16), final_norm_g[None, :])
    return out
```

```python
import functools

import numpy as np
import jax
import jax.numpy as jnp
from jax import lax
from jax.experimental import pallas as pl
from jax.experimental.pallas import tpu as pltpu

D_MODEL = 1024
CHUNK = 64
HEADS = 4
DK = 64
DV = 128
GATE_RANK = 16
GATE_RANK_PAD = 128
GATE_TAU = 16.0
RET_DECAY_EXP_LO = 5.0
RET_DECAY_EXP_HI = 12.0
ROPE_BASE = 10000.0
XATTN_HEADS = 4
XATTN_DH = D_MODEL // XATTN_HEADS
D_FF = 4 * D_MODEL
FF_BLOCK = 1024
XATTN_TILE = 512
XATTN_BLOCKS = (256, 256)
PREP_ROWS = 256
EPS = 1e-6

HK = HEADS * DK
HV = HEADS * DV
C_GQ, C_GK, C_GV, C_GG = 0, HK, 2 * HK, 2 * HK + HV
C_RQ = C_GG + HV
C_RK = C_RQ + HK
C_RV = C_RK + HK
C_RG = C_RV + HV
C_LR = C_RG + HV
D_PROJ = C_LR + GATE_RANK_PAD

SEQ_TILE = 512
PROJ_PIECE = 256
STAGE_A_PIECES = 5
CUM_BLOCK = 256
OUT_ROWS = 512
VMEM_LIMIT_BYTES = 56 * 1024 * 1024

F32 = jnp.float32
BF16 = jnp.bfloat16


def _dot(a, b):
    return jnp.dot(a, b, preferred_element_type=F32)


def _dot_nt(a, b):
    return lax.dot_general(a, b, (((1,), (1,)), ((), ())), preferred_element_type=F32)


def _rms(x, g):
    return x * lax.rsqrt(jnp.mean(x * x, axis=-1, keepdims=True) + EPS) * g


def _swish(g):
    return g / (1.0 + jnp.exp(-g))


def _stack_heads(t):
    lane_head = lax.broadcasted_iota(jnp.int32, t.shape, 1) // DK
    zero = jnp.zeros_like(t)
    return jnp.concatenate([jnp.where(lane_head == h, t, zero) for h in range(HEADS)], axis=0)


def _prep_w_in_kernel(wt_ref, o_ref):
    lr0 = 2 * HK + HV
    o_ref[:, :lr0] = wt_ref[:lr0, :].T.astype(BF16)
    o_ref[:, lr0:C_LR] = wt_ref[lr0 + GATE_RANK:, :].T.astype(BF16)
    tail = wt_ref[lr0:lr0 + GATE_RANK_PAD, :]
    row = lax.broadcasted_iota(jnp.int32, tail.shape, 0)
    o_ref[:, C_LR:] = jnp.where(row < GATE_RANK, tail, 0.0).T.astype(BF16)


def _mem_kv_kernel(mem_ref, g_ref, wkv_ref, wq_ref, wo_ref, wqk_ref, wvo_ref):
    n_mem = mem_ref.shape[1]
    mn = _rms(mem_ref[0], g_ref[...])
    kv = _dot(mn, wkv_ref[...])
    for h in range(XATTN_HEADS):
        cols = slice(h * XATTN_DH, (h + 1) * XATTN_DH)
        mcols = slice(h * n_mem, (h + 1) * n_mem)
        k_h = kv[:, cols]
        v_h = kv[:, D_MODEL + h * XATTN_DH:D_MODEL + (h + 1) * XATTN_DH]
        wqk_ref[0, :, mcols] = (_dot_nt(wq_ref[:, cols], k_h) * (XATTN_DH ** -0.5)).astype(BF16)
        wvo_ref[0, mcols, :] = _dot(v_h, wo_ref[cols, :]).astype(BF16)


def _mixer_kernel(xnext_ref, xres_ref, cosb_ref, sinb_ref, coso_ref, sino_ref, g_ref, win_ref, wg_ref,
                  bg_ref, glan_ref, retn_ref, wout_ref, dmat_ref, qdec_ref, kdect_ref, sdec_ref,
                  o_ref, proj_a_ref, proj_b_ref, xn_ref, y_ref, kt_ref, sg_ref, sr_ref, *,
                  n_chunks, n_tiles):
    s = pl.program_id(0)
    prev_tile = jnp.maximum(s - 1, 0)

    @pl.when(s == 0)
    def _():
        proj_b_ref[...] = jnp.zeros_like(proj_b_ref)

    @pl.when(prev_tile % n_tiles == 0)
    def _():
        sg_ref[...] = jnp.zeros_like(sg_ref)
        sr_ref[...] = jnp.zeros_like(sr_ref)

    chunk_pass = functools.partial(
        _chunk_pass, xnext_ref=xnext_ref, g_ref=g_ref, xn_ref=xn_ref,
        rope_refs=(cosb_ref, sinb_ref, coso_ref, sino_ref),
        win_ref=win_ref, wg_ref=wg_ref, bg_ref=bg_ref, glan_ref=glan_ref, retn_ref=retn_ref,
        dmat_ref=dmat_ref, qdec_ref=qdec_ref, kdect_ref=kdect_ref, sdec_ref=sdec_ref, y_ref=y_ref,
        kt_ref=kt_ref, sg_ref=sg_ref, sr_ref=sr_ref, xres_ref=xres_ref, wout_ref=wout_ref,
        o_ref=o_ref, n_chunks=n_chunks)

    @pl.when(s % 2 == 0)
    def _():
        chunk_pass(proj_a_ref, proj_b_ref)

    @pl.when(s % 2 == 1)
    def _():
        chunk_pass(proj_b_ref, proj_a_ref)


def _chunk_pass(cur_ref, prv_ref, *, xnext_ref, g_ref, xn_ref, rope_refs, win_ref, wg_ref, bg_ref, glan_ref,
                retn_ref, dmat_ref, qdec_ref, kdect_ref, sdec_ref, y_ref, kt_ref, sg_ref, sr_ref,
                xres_ref, wout_ref, o_ref, n_chunks):
    ts = n_chunks * CHUNK
    chunk_rows = [slice(c * CHUNK, (c + 1) * CHUNK) for c in range(n_chunks)]
    head_rows = [slice(h * CHUNK, (h + 1) * CHUNK) for h in range(HEADS)]
    head_cols = [slice(h * DV, (h + 1) * DV) for h in range(HEADS)]
    pieces = [(i * PROJ_PIECE, min((i + 1) * PROJ_PIECE, D_PROJ))
              for i in range(-(-D_PROJ // PROJ_PIECE))]

    def emit_proj(n):
        for _ in range(min(n, len(pieces))):
            lo, hi = pieces.pop(0)
            cur_ref[:, lo:hi] = _dot(xn_ref[...], win_ref[:, lo:hi])

    bi = lax.broadcasted_iota(jnp.int32, (CUM_BLOCK, CUM_BLOCK), 0)
    bj = lax.broadcasted_iota(jnp.int32, (CUM_BLOCK, CUM_BLOCK), 1)
    tril_blocks = ((bi // CHUNK == bj // CHUNK) & (bj <= bi)).astype(BF16)
    si = lax.broadcasted_iota(jnp.int32, (HEADS * CHUNK, CHUNK), 0) % CHUNK
    sj = lax.broadcasted_iota(jnp.int32, (HEADS * CHUNK, CHUNK), 1)
    lower = sj <= si
    first_half = (lax.broadcasted_iota(jnp.int32, (1, HK), 1) % DK) < (DK // 2)
    lane_pad = jnp.zeros((HK, CHUNK), BF16)

    def contributions(kdt, v):
        return jnp.concatenate([_dot(kdt[head_rows[h]], v[:, head_cols[h]]) for h in range(HEADS)],
                               axis=0)

    def scan_states(s_ref, decays, contribs):
        state, starts = s_ref[...], []
        for dec, con in zip(decays, contribs):
            starts.append(state.astype(BF16))
            state = state * dec + con
        s_ref[...] = state
        return starts

    xn_ref[...] = _rms(xnext_ref[0], g_ref[...]).astype(BF16)
    z = _dot(prv_ref[:, C_LR:C_LR + GATE_RANK_PAD].astype(BF16), wg_ref[...]) + bg_ref[...]
    emit_proj(STAGE_A_PIECES)
    log_a = (jnp.minimum(z, 0.0) - jnp.log1p(jnp.exp(-jnp.abs(z)))) * (1.0 / GATE_TAU)
    la_hi = log_a.astype(BF16)
    la_lo = (log_a - la_hi.astype(F32)).astype(BF16)
    la_split = jnp.concatenate([la_hi, la_lo], axis=1)
    bcum = []
    for b in range(ts // CUM_BLOCK):
        r = _dot(tril_blocks, la_split[b * CUM_BLOCK:(b + 1) * CUM_BLOCK])
        bcum.append(r[:, :HK] + r[:, HK:])
    bcum = jnp.concatenate(bcum, axis=0)
    emit_proj(1)

    g_decay, g_contrib = [], []
    for c, rows in enumerate(chunk_rows):
        kt = prv_ref[rows, C_GK:C_GK + HK].T
        bct = bcum[rows].T
        blt = bct[:, CHUNK - 1:CHUNK]
        kt_ref[0, c] = (kt * jnp.exp(bct)).astype(BF16)
        kt_ref[1, c] = (kt * jnp.exp(-bct)).astype(BF16)
        kdt = (kt * jnp.exp(blt - bct)).astype(BF16)
        g_decay.append(jnp.exp(blt))
        g_contrib.append(contributions(kdt, prv_ref[rows, C_GV:C_GV + HV].astype(BF16)))
        if c % 2 == 1:
            emit_proj(1)

    cosb_ref, sinb_ref, coso_ref, sino_ref = rope_refs
    cosb, sinb = cosb_ref[0], sinb_ref[0]

    def rope(col0, rows):
        t = prv_ref[rows, col0:col0 + HK]
        coso, sino = coso_ref[rows, :], sino_ref[rows, :]
        cs = cosb * coso - sinb * sino
        sn = sinb * coso + cosb * sino
        sn = jnp.where(first_half, -sn, sn)
        partner = jnp.where(first_half, pltpu.roll(t, HK - DK // 2, 1), pltpu.roll(t, DK // 2, 1))
        return t * cs + partner * sn

    r_contrib = []
    for c, rows in enumerate(chunk_rows):
        kt = (rope(C_RK, rows) * (DK ** -0.5)).T
        kt_ref[2, c] = kt.astype(BF16)
        kdt = (kt * kdect_ref[...]).astype(BF16)
        r_contrib.append(contributions(kdt, prv_ref[rows, C_RV:C_RV + HV].astype(BF16)))
        if c % 2 == 1:
            emit_proj(1)

    g_start = scan_states(sg_ref, g_decay, g_contrib)
    r_start = scan_states(sr_ref, [sdec_ref[...]] * n_chunks, r_contrib)

    def gla_scores(c):
        rows = chunk_rows[c]
        q = prv_ref[rows, C_GQ:C_GQ + HK] * (DK ** -0.5)
        qeb = (q * jnp.exp(bcum[rows])).astype(BF16)
        qebi = (q * jnp.exp(-bcum[rows])).astype(BF16)
        r = _dot(_stack_heads(qeb), jnp.concatenate([g_start[c], kt_ref[1, c], lane_pad], axis=1))
        a_up = _dot(_stack_heads(qebi), kt_ref[0, c])
        return r[:, :DV], jnp.where(lower, r[:, DV:DV + CHUNK], a_up).astype(BF16)

    def ret_scores(c):
        r = _dot(_stack_heads(rope(C_RQ, chunk_rows[c]).astype(BF16)),
                 jnp.concatenate([r_start[c], kt_ref[2, c], lane_pad], axis=1))
        return r[:, :DV] * qdec_ref[...], (r[:, DV:DV + CHUNK] * dmat_ref[...]).astype(BF16)

    def gla_values(c, inter, attn):
        rows = chunk_rows[c]
        v = prv_ref[rows, C_GV:C_GV + HV].astype(BF16)
        for h in range(HEADS):
            o = _dot(attn[head_rows[h]], v[:, head_cols[h]]) + inter[head_rows[h]]
            gate = prv_ref[rows, C_GG + h * DV:C_GG + (h + 1) * DV]
            yn = o * lax.rsqrt(jnp.mean(o * o, axis=-1, keepdims=True) + EPS) * glan_ref[h:h + 1, :]
            y_ref[rows, h * DV:(h + 1) * DV] = (yn * _swish(gate)).astype(BF16)

    def ret_values(c, inter, scores):
        rows = chunk_rows[c]
        v = prv_ref[rows, C_RV:C_RV + HV].astype(BF16)
        for h in range(HEADS):
            o = _dot(scores[head_rows[h]], v[:, head_cols[h]]) + inter[head_rows[h]]
            gate = prv_ref[rows, C_RG + h * DV:C_RG + (h + 1) * DV]
            d = o - jnp.mean(o, axis=-1, keepdims=True)
            yn = d * lax.rsqrt(jnp.mean(d * d, axis=-1, keepdims=True) + EPS) * retn_ref[h:h + 1, :]
            y_ref[rows, HV + h * DV:HV + (h + 1) * DV] = (yn * _swish(gate)).astype(BF16)

    def out_proj(done):
        o_ref[0, done, :] = xres_ref[0, done, :] + _dot(y_ref[done, :], wout_ref[...])

    finished = None
    ahead = (gla_scores(0), ret_scores(0))
    for c in range(n_chunks):
        gla_ahead, ret_ahead = ahead
        if c + 1 < n_chunks:
            ahead = (gla_scores(c + 1), ret_scores(c + 1))
        if finished is not None:
            out_proj(finished)
            finished = None
        else:
            emit_proj(1)
        gla_values(c, *gla_ahead)
        ret_values(c, *ret_ahead)
        if (c + 1) % (OUT_ROWS // CHUNK) == 0:
            finished = slice((c + 1) * CHUNK - OUT_ROWS, (c + 1) * CHUNK)
    emit_proj(len(pieces))
    out_proj(finished)


def _xattn_mlp_kernel(h_ref, wqk_ref, wvo_ref, gx_ref, gf_ref, w1_ref, w2_ref, gfin_ref, o_ref):
    bounds = np.cumsum((0,) + XATTN_BLOCKS)
    blocks = [slice(int(a), int(b)) for a, b in zip(bounds[:-1], bounds[1:])]
    n_mem = wqk_ref.shape[2] // XATTN_HEADS
    h = [h_ref[0, rows, :] for rows in blocks]
    scores = [_dot(_rms(hb, gx_ref[...]).astype(BF16), wqk_ref[0]) for hb in h]
    probs = []
    for sb in scores:
        heads = []
        for a in range(XATTN_HEADS):
            s = sb[:, a * n_mem:(a + 1) * n_mem]
            e = jnp.exp(s - jnp.max(s, axis=-1, keepdims=True))
            heads.append((e * (1.0 / jnp.sum(e, axis=-1, keepdims=True))).astype(BF16))
        probs.append(jnp.concatenate(heads, axis=1))
    h = [hb + _dot(pb, wvo_ref[0]) for hb, pb in zip(h, probs)]

    hn = [_rms(hb, gf_ref[...]).astype(BF16) for hb in h]
    for c in range(D_FF // FF_BLOCK):
        cols = slice(c * FF_BLOCK, (c + 1) * FF_BLOCK)
        u = [jnp.maximum(_dot(hb, w1_ref[:, cols]), 0.0) for hb in hn]
        h = [hb + _dot((ub * ub).astype(BF16), w2_ref[cols, :]) for hb, ub in zip(h, u)]
    for rows, hb in zip(blocks, h):
        o_ref[0, rows, :] = _rms(hb, gfin_ref[...])


def _const_spec(shape):
    n = len(shape)
    return pl.BlockSpec(shape, lambda *_: (0,) * n, pipeline_mode=pl.Buffered(1))


def _retention_tables():
    gamma = 1.0 - np.exp2(-np.linspace(RET_DECAY_EXP_LO, RET_DECAY_EXP_HI, HEADS, dtype=np.float32))
    log_g = np.log(gamma.astype(np.float32)).astype(np.float32)
    pos = np.arange(CHUNK, dtype=np.float32)
    dmat = np.exp(np.abs(pos[:, None] - pos[None, :])[None] * log_g[:, None, None])
    k_dec = np.exp((CHUNK - 1.0 - pos)[None, :] * log_g[:, None])
    q_dec = np.exp((pos + 1.0)[None, :] * log_g[:, None])
    s_dec = np.exp(CHUNK * log_g)
    dmat_s = dmat.reshape(HEADS * CHUNK, CHUNK)
    qdec_s = np.broadcast_to(q_dec.reshape(HEADS * CHUNK, 1), (HEADS * CHUNK, DV))
    kdect_s = np.repeat(k_dec, DK, axis=0)
    sdec_s = np.broadcast_to(np.repeat(s_dec, DK)[:, None], (HEADS * DK, DV))
    as32 = lambda a: jnp.asarray(np.ascontiguousarray(a), dtype=F32)
    return as32(dmat_s), as32(qdec_s), as32(kdect_s), as32(sdec_s)


def kernel(x, mem, norm_mix_g, w_in, gla_w_gate, gla_b_gate, gla_norm_g, ret_norm_g, w_out, norm_x_g,
           norm_mem_g, xa_w_q, xa_w_kv, xa_w_o, norm_ffn_g, ffn_w1, ffn_w2, final_norm_g):
    B, S, D = x.shape
    n_mem = mem.shape[1]
    assert D == D_MODEL and S % SEQ_TILE == 0 and S % XATTN_TILE == 0 and w_in.shape[0] == 1
    ts = SEQ_TILE
    n_tiles = S // ts

    inv_freq = ROPE_BASE ** (-jnp.arange(0, DK, 2, dtype=F32) / DK)
    inv_lane = jnp.tile(inv_freq, 2 * HEADS)
    ang_off = jnp.arange(ts, dtype=F32)[:, None] * inv_lane[None, :]
    ang_base = (jnp.arange(n_tiles, dtype=F32) * ts)[:, None, None] * inv_lane[None, None, :]
    cos_off, sin_off = jnp.cos(ang_off), jnp.sin(ang_off)
    cos_base, sin_base = jnp.cos(ang_base), jnp.sin(ang_base)

    w_in_r = pl.pallas_call(
        _prep_w_in_kernel,
        grid=(D // PREP_ROWS,),
        in_specs=[pl.BlockSpec((w_in.shape[2], PREP_ROWS), lambda i: (0, i))],
        out_specs=pl.BlockSpec((PREP_ROWS, D_PROJ), lambda i: (i, 0)),
        out_shape=jax.ShapeDtypeStruct((D, D_PROJ), BF16),
        compiler_params=pltpu.CompilerParams(dimension_semantics=("arbitrary",),
                                             vmem_limit_bytes=VMEM_LIMIT_BYTES),
        name="prep_w_in",
    )(jnp.swapaxes(w_in[0], 0, 1))
    w_gate_p = jnp.concatenate(
        [gla_w_gate[0], jnp.zeros((GATE_RANK_PAD - GATE_RANK, HK), F32)], axis=0).astype(BF16)
    dmat_s, qdec_s, kdect_s, sdec_s = _retention_tables()

    params = pltpu.CompilerParams(dimension_semantics=("arbitrary", "arbitrary"),
                                  vmem_limit_bytes=VMEM_LIMIT_BYTES)

    hm = XATTN_HEADS * n_mem
    wqk, wvo = pl.pallas_call(
        _mem_kv_kernel,
        grid=(B,),
        in_specs=[pl.BlockSpec((1, n_mem, D), lambda b: (b, 0, 0)),
                  _const_spec((1, D)), _const_spec((D, 2 * D)), _const_spec((D, D)),
                  _const_spec((D, D))],
        out_specs=[pl.BlockSpec((1, D, hm), lambda b: (b, 0, 0)),
                   pl.BlockSpec((1, hm, D), lambda b: (b, 0, 0))],
        out_shape=[jax.ShapeDtypeStruct((B, D, hm), BF16), jax.ShapeDtypeStruct((B, hm, D), BF16)],
        compiler_params=pltpu.CompilerParams(dimension_semantics=("arbitrary",),
                                             vmem_limit_bytes=VMEM_LIMIT_BYTES),
        name="mem_kv",
    )(mem, norm_mem_g[0][None, :], xa_w_kv[0], xa_w_q[0], xa_w_o[0])

    n_steps = B * n_tiles + 1

    def next_tile(s):
        p = jnp.minimum(s, n_steps - 2)
        return (p // n_tiles, p % n_tiles, 0)

    def prev_tile(s):
        q = jnp.maximum(s - 1, 0)
        return (q // n_tiles, q % n_tiles, 0)

    def prev_rope(s):
        return (jnp.maximum(s - 1, 0) % n_tiles, 0, 0)

    h1 = pl.pallas_call(
        functools.partial(_mixer_kernel, n_chunks=ts // CHUNK, n_tiles=n_tiles),
        grid=(n_steps,),
        in_specs=[pl.BlockSpec((1, ts, D), next_tile),
                  pl.BlockSpec((1, ts, D), prev_tile),
                  pl.BlockSpec((1, 1, HK), prev_rope), pl.BlockSpec((1, 1, HK), prev_rope),
                  _const_spec((ts, HK)), _const_spec((ts, HK)),
                  _const_spec((1, D)), _const_spec((D, D_PROJ)),
                  _const_spec((GATE_RANK_PAD, HK)), _const_spec((1, HK)),
                  _const_spec((HEADS, DV)), _const_spec((HEADS, DV)),
                  _const_spec((D, D)),
                  _const_spec((HEADS * CHUNK, CHUNK)), _const_spec((HEADS * CHUNK, DV)),
                  _const_spec((HEADS * DK, CHUNK)), _const_spec((HEADS * DK, DV))],
        out_specs=pl.BlockSpec((1, ts, D), prev_tile),
        out_shape=jax.ShapeDtypeStruct((B, S, D), F32),
        scratch_shapes=[pltpu.VMEM((ts, D_PROJ), F32), pltpu.VMEM((ts, D_PROJ), F32),
                        pltpu.VMEM((ts, D), BF16), pltpu.VMEM((ts, D), BF16),
                        pltpu.VMEM((3, ts // CHUNK, HK, CHUNK), BF16),
                        pltpu.VMEM((HEADS * DK, DV), F32), pltpu.VMEM((HEADS * DK, DV), F32)],
        compiler_params=pltpu.CompilerParams(dimension_semantics=("arbitrary",),
                                             vmem_limit_bytes=VMEM_LIMIT_BYTES),
        name="mixer",
    )(x, x, cos_base, sin_base, cos_off, sin_off, norm_mix_g[0][None, :], w_in_r, w_gate_p,
      gla_b_gate[0][None, :], gla_norm_g[0], ret_norm_g[0], w_out[0].astype(BF16),
      dmat_s, qdec_s, kdect_s, sdec_s)

    out = pl.pallas_call(
        _xattn_mlp_kernel,
        grid=(B, S // XATTN_TILE),
        in_specs=[pl.BlockSpec((1, XATTN_TILE, D), lambda b, t: (b, t, 0)),
                  pl.BlockSpec((1, D, hm), lambda b, t: (b, 0, 0)),
                  pl.BlockSpec((1, hm, D), lambda b, t: (b, 0, 0)),
                  _const_spec((1, D)),
                  _const_spec((1, D)), _const_spec((D, D_FF)), _const_spec((D_FF, D)),
                  _const_spec((1, D))],
        out_specs=pl.BlockSpec((1, XATTN_TILE, D), lambda b, t: (b, t, 0)),
        out_shape=jax.ShapeDtypeStruct((B, S, D), F32),
        compiler_params=params,
        name="xattn_mlp",
    )(h1, wqk, wvo, norm_x_g[0][None, :], norm_ffn_g[0][None, :], ffn_w1[0].astype(BF16),
      ffn_w2[0].astype(BF16), final_norm_g[None, :])
    return out
```
